```python
import math
import jax, jax.numpy as jnp
from jax import lax
import numpy as np

D_MODEL = 2048
BATCH = 1
SEQ = 8192
DEPTH = 2
DEC_BATCH = 16
DEC_SEQ = 64
PAST_LEN = 2048

CHUNK = 64
D_MIX = D_MODEL
SB_WIDTH = D_MIX // 2
SB_HEAD_DIM = 128
SB_HEADS = SB_WIDTH // SB_HEAD_DIM
CONV_WIDTH = D_MIX // 4
CONV_KERNEL = 31
CONV_GROUPS = 4
POOL_WIDTH = D_MIX - SB_WIDTH - CONV_WIDTH
POOL_WINDOWS = (2, 4, 8, 16)
POOL_GROUP = POOL_WIDTH // len(POOL_WINDOWS)
POOL_HIST = max(POOL_WINDOWS) - 1
D_FF = 4 * D_MODEL
Q_BLOCK = 128
IN_COLS = 3 * SB_WIDTH + 2 * CONV_WIDTH + POOL_WIDTH
EPS = 1e-6

kernel_name = "hymba_stickbreak_conformer_pool_stream"


def rmsnorm(x, g):
    xf = x.astype(jnp.float32)
    y = xf * lax.rsqrt(jnp.mean(xf * xf, axis=-1, keepdims=True) + EPS)
    return (y * g.astype(jnp.float32)).astype(x.dtype)


def stick_breaking_block(q, q_pos, k, v, k_pos):
    z = jnp.einsum('bhqd,bhkd->bhqk', q.astype(jnp.float32), k.astype(jnp.float32)) * (SB_HEAD_DIM ** -0.5)
    mask = k_pos[None, :] < q_pos[:, None]
    log_keep = jnp.where(mask, jax.nn.log_sigmoid(-z), 0.0)
    after = lax.cumsum(log_keep, axis=3, reverse=True) - log_keep
    w = jnp.where(mask, jnp.exp(jax.nn.log_sigmoid(z) + after), 0.0)
    return jnp.einsum('bhqk,bhkd->bhqd', w, v.astype(jnp.float32)).astype(v.dtype)


def sb_prompt(q, k, v):
    B, H, T, d = q.shape
    nb = T // Q_BLOCK
    qb = jnp.moveaxis(q.reshape(B, H, nb, Q_BLOCK, d), 2, 0)
    pos = jnp.arange(T, dtype=jnp.int32)
    pb = pos.reshape(nb, Q_BLOCK)
    ob = lax.map(lambda a: stick_breaking_block(a[0], a[1], k, v, pos), (qb, pb))
    return jnp.moveaxis(ob, 0, 2).reshape(B, H, T, d)


def sb_sample(q, k, v, cache_k_l, cache_v_l):
    T = q.shape[2]
    P = cache_k_l.shape[2]
    k_all = jnp.concatenate([cache_k_l.astype(k.dtype), k], axis=2)
    v_all = jnp.concatenate([cache_v_l.astype(v.dtype), v], axis=2)
    k_pos = jnp.arange(P + T, dtype=jnp.int32)
    q_pos = P + jnp.arange(T, dtype=jnp.int32)
    return stick_breaking_block(q, q_pos, k_all, v_all, k_pos)


def conv_module(u, hist, dw_w, dw_b, n_g, n_b, pw_w, pw_b):
    a, b = jnp.split(u, 2, axis=-1)
    glu = a * jax.nn.sigmoid(b)
    ext = jnp.concatenate([hist.astype(glu.dtype), glu], axis=1)
    y = lax.conv_general_dilated(ext, dw_w[:, None, :].astype(ext.dtype), window_strides=(1,),
                                 padding='VALID', dimension_numbers=('NWC', 'WIO', 'NWC'),
                                 feature_group_count=CONV_WIDTH) + dw_b
    B, T, C = y.shape
    yg = y.astype(jnp.float32).reshape(B, T, CONV_GROUPS, C // CONV_GROUPS)
    mu = jnp.mean(yg, axis=-1, keepdims=True)
    var = jnp.mean(jnp.square(yg - mu), axis=-1, keepdims=True)
    yn = ((yg - mu) * lax.rsqrt(var + EPS)).reshape(B, T, C) * n_g.astype(jnp.float32) + n_b.astype(jnp.float32)
    yn = jax.nn.silu(yn).astype(u.dtype)
    return yn @ pw_w + pw_b, ext[:, -(CONV_KERNEL - 1):]


def pool_mixer(u, hist, start_pos, pool_w, pool_scale):
    B, T, C = u.shape
    ext_raw = jnp.concatenate([hist.astype(u.dtype), u], axis=1)
    ext = ext_raw.astype(jnp.float32)
    cs = jnp.concatenate([jnp.zeros((B, 1, C), jnp.float32), jnp.cumsum(ext, axis=1)], axis=1)
    end = cs[:, POOL_HIST + 1:]
    pos = start_pos + jnp.arange(T, dtype=jnp.int32)
    means = []
    for g, w in enumerate(POOL_WINDOWS):
        sl = slice(g * POOL_GROUP, (g + 1) * POOL_GROUP)
        s = end[..., sl] - cs[:, POOL_HIST + 1 - w: POOL_HIST + 1 - w + T, sl]
        cnt = jnp.minimum(w, pos + 1).astype(jnp.float32)[None, :, None]
        means.append(s / cnt)
    pooled = (jnp.concatenate(means, axis=-1) - ext[:, POOL_HIST:]).astype(u.dtype)
    mixed = jnp.einsum('btgc,gcd->btgd', pooled.reshape(B, T, len(POOL_WINDOWS), POOL_GROUP), pool_w)
    return mixed.reshape(B, T, C) * pool_scale, ext_raw[:, -POOL_HIST:]


def trunk_layer(h, start_pos, cache_k_l, cache_v_l, conv_hist, pool_hist,
                norm_mix_g, w_in, conv_dw_w, conv_dw_b, conv_norm_g, conv_norm_b,
                conv_pw_w, conv_pw_b, pool_w, pool_scale, w_out, norm_mlp_g, w_up, w_down):
    B, T, _ = h.shape
    xn = rmsnorm(h, norm_mix_g)
    proj = xn @ w_in
    q, k, v, u_conv, u_pool = jnp.split(
        proj, [SB_WIDTH, 2 * SB_WIDTH, 3 * SB_WIDTH, 3 * SB_WIDTH + 2 * CONV_WIDTH], axis=-1)
    to_heads = lambda t: t.reshape(B, T, SB_HEADS, SB_HEAD_DIM).transpose(0, 2, 1, 3)
    q, k, v = to_heads(q), to_heads(k), to_heads(v)
    if cache_k_l is None:
        o_sb = sb_prompt(q, k, v)
    else:
        o_sb = sb_sample(q, k, v, cache_k_l, cache_v_l)
    o_sb = o_sb.transpose(0, 2, 1, 3).reshape(B, T, SB_WIDTH)
    o_conv, conv_state = conv_module(u_conv, conv_hist, conv_dw_w, conv_dw_b, conv_norm_g,
                                     conv_norm_b, conv_pw_w, conv_pw_b)
    o_pool, pool_state = pool_mixer(u_pool, pool_hist, start_pos, pool_w, pool_scale)
    h = h + jnp.concatenate([o_sb, o_conv, o_pool], axis=-1) @ w_out
    hn = rmsnorm(h, norm_mlp_g)
    h = h + jnp.square(jax.nn.relu(hn @ w_up)) @ w_down
    return h, k, v, conv_state, pool_state


def setup_inputs(seed: int = 0) -> dict:
    key = jax.random.key(seed)
    ks = jax.random.split(key, 24)
    f = jnp.float32
    nrm = lambda k, shape, s: jax.random.normal(k, shape, f) * s
    return {
        "x_prompt": nrm(ks[0], (BATCH, SEQ, D_MODEL), 1.0),
        "x_sample": nrm(ks[1], (DEC_BATCH, DEC_SEQ, D_MODEL), 1.0),
        "cache_k": nrm(ks[2], (DEPTH, DEC_BATCH, SB_HEADS, PAST_LEN, SB_HEAD_DIM), 1.0),
        "cache_v": nrm(ks[3], (DEPTH, DEC_BATCH, SB_HEADS, PAST_LEN, SB_HEAD_DIM), 1.0),
        "cache_conv": nrm(ks[4], (DEPTH, DEC_BATCH, CONV_KERNEL - 1, CONV_WIDTH), 0.5),
        "state_pool": nrm(ks[5], (DEPTH, DEC_BATCH, POOL_HIST, POOL_WIDTH), 1.0),
        "norm_mix_g": 1.0 + nrm(ks[6], (DEPTH, D_MODEL), 0.02),
        "w_in": nrm(ks[7], (DEPTH, D_MODEL, IN_COLS), D_MODEL ** -0.5),
        "conv_dw_w": nrm(ks[8], (DEPTH, CONV_KERNEL, CONV_WIDTH), CONV_KERNEL ** -0.5),
        "conv_dw_b": nrm(ks[9], (DEPTH, CONV_WIDTH), 0.02),
        "conv_norm_g": 1.0 + nrm(ks[10], (DEPTH, CONV_WIDTH), 0.02),
        "conv_norm_b": nrm(ks[11], (DEPTH, CONV_WIDTH), 0.02),
        "conv_pw_w": nrm(ks[12], (DEPTH, CONV_WIDTH, CONV_WIDTH), CONV_WIDTH ** -0.5),
        "conv_pw_b": nrm(ks[13], (DEPTH, CONV_WIDTH), 0.02),
        "pool_w": nrm(ks[14], (DEPTH, len(POOL_WINDOWS), POOL_GROUP, POOL_GROUP), POOL_GROUP ** -0.5),
        "pool_scale": 1.0 + nrm(ks[15], (DEPTH, POOL_WIDTH), 0.1),
        "w_out": nrm(ks[16], (DEPTH, D_MIX, D_MODEL), D_MIX ** -0.5),
        "norm_mlp_g": 1.0 + nrm(ks[17], (DEPTH, D_MODEL), 0.02),
        "w_up": nrm(ks[18], (DEPTH, D_MODEL, D_FF), D_MODEL ** -0.5),
        "w_down": nrm(ks[19], (DEPTH, D_FF, D_MODEL), D_FF ** -0.5),
        "final_norm_g": 1.0 + nrm(ks[20], (D_MODEL,), 0.02),
    }


def reference(x_prompt, x_sample, cache_k, cache_v, cache_conv, state_pool,
              norm_mix_g, w_in, conv_dw_w, conv_dw_b, conv_norm_g, conv_norm_b,
              conv_pw_w, conv_pw_b, pool_w, pool_scale, w_out, norm_mlp_g, w_up, w_down,
              final_norm_g):
    Bp = x_prompt.shape[0]
    past = cache_k.shape[3]
    h = x_prompt
    kp, vp, cp, pp = [], [], [], []
    for l in range(DEPTH):
        lp = (norm_mix_g[l], w_in[l], conv_dw_w[l], conv_dw_b[l], conv_norm_g[l], conv_norm_b[l],
              conv_pw_w[l], conv_pw_b[l], pool_w[l], pool_scale[l], w_out[l], norm_mlp_g[l],
              w_up[l], w_down[l])
        conv0 = jnp.zeros((Bp, CONV_KERNEL - 1, CONV_WIDTH), x_prompt.dtype)
        pool0 = jnp.zeros((Bp, POOL_HIST, POOL_WIDTH), x_prompt.dtype)
        h, k_l, v_l, c_l, p_l = trunk_layer(h, 0, None, None, conv0, pool0, *lp)
        kp.append(k_l); vp.append(v_l); cp.append(c_l); pp.append(p_l)
    y_prompt = rmsnorm(h, final_norm_g)
    h = x_sample
    ks_, vs_, cs_, ps_ = [], [], [], []
    for l in range(DEPTH):
        lp = (norm_mix_g[l], w_in[l], conv_dw_w[l], conv_dw_b[l], conv_norm_g[l], conv_norm_b[l],
              conv_pw_w[l], conv_pw_b[l], pool_w[l], pool_scale[l], w_out[l], norm_mlp_g[l],
              w_up[l], w_down[l])
        h, k_l, v_l, c_l, p_l = trunk_layer(h, past, cache_k[l], cache_v[l], cache_conv[l],
                                            state_pool[l], *lp)
        ks_.append(k_l); vs_.append(v_l); cs_.append(c_l); ps_.append(p_l)
    y_sample = rmsnorm(h, final_norm_g)
    return (y_prompt, y_sample,
            jnp.stack(kp), jnp.stack(vp), jnp.stack(cp), jnp.stack(pp),
            jnp.stack(ks_), jnp.stack(vs_), jnp.stack(cs_), jnp.stack(ps_))
```

```python
import functools
import math

import jax
import jax.numpy as jnp
from jax import lax
from jax.experimental import pallas as pl
from jax.experimental.pallas import tpu as pltpu

F32 = jnp.float32
BF16 = jnp.bfloat16

D_MODEL = 2048
SB_WIDTH = 1024
HEAD_DIM = 128
HEADS = SB_WIDTH // HEAD_DIM
CONV_WIDTH = 512
CONV_KERNEL = 31
CONV_GROUPS = 4
POOL_WIDTH = 512
POOL_WINDOWS = (2, 4, 8, 16)
POOL_GROUP = POOL_WIDTH // len(POOL_WINDOWS)
POOL_HIST = max(POOL_WINDOWS) - 1
D_FF = 4 * D_MODEL
IN_COLS = 3 * SB_WIDTH + 2 * CONV_WIDTH + POOL_WIDTH
EPS = 1e-6
DEC_SEQ = 64

LOG2E = 1.4426950408889634
Q_SCALE = HEAD_DIM ** -0.5 * LOG2E

LANE = 128
HIST_ROWS = 32
ROW_TILE = 512
FF_TILE = 512
ATTN_TQ = 512
ATTN_TK = 256
CONV_CHUNK = 64
VMEM_LIMIT = 56 * 1024 * 1024


def _params(*sem):
    return pltpu.CompilerParams(dimension_semantics=sem, vmem_limit_bytes=VMEM_LIMIT)


def _dot(a, b):
    return jnp.dot(a, b, preferred_element_type=F32)


def _softplus2(z):
    e = jnp.exp2(-jnp.abs(z))
    return jnp.maximum(z, 0.0) + jnp.log(1.0 + e) * LOG2E


def _split_bf16(x):
    hi = x.astype(BF16)
    lo = (x - hi.astype(F32)).astype(BF16)
    return hi, lo


def _rmsnorm_rows(x, g):
    ms = jnp.mean(x * x, axis=-1, keepdims=True)
    return x * lax.rsqrt(ms + EPS) * g


def _rmsnorm_kernel(x_ref, g_ref, o_ref):
    o_ref[...] = _rmsnorm_rows(x_ref[...], g_ref[...]).astype(o_ref.dtype)


def rmsnorm_call(x, g):
    m, d = x.shape
    tm = min(ROW_TILE, m)
    return pl.pallas_call(
        _rmsnorm_kernel,
        grid=(m // tm,),
        in_specs=[pl.BlockSpec((tm, d), lambda i: (i, 0)),
                  pl.BlockSpec((1, d), lambda i: (0, 0))],
        out_specs=pl.BlockSpec((tm, d), lambda i: (i, 0)),
        out_shape=jax.ShapeDtypeStruct((m, d), BF16),
        compiler_params=_params("arbitrary"),
        name="rmsnorm",
    )(x, g.reshape(1, d))


def _inproj_kernel(xn_ref, w_ref, *outs, tm, sample):
    x = xn_ref[...]
    if sample:
        q_ref, kf_ref, vf_ref, uc_ref, up_ref = outs
        nb = tm // DEC_SEQ
    else:
        qt_ref, kb_ref, vt_ref, kf_ref, vf_ref, uc_ref, up_ref = outs
    pair = 2 * HEAD_DIM
    for hp in range(HEADS // 2):
        c0 = hp * pair
        q2 = _dot(x, w_ref[:, c0:c0 + pair]) * Q_SCALE
        k2 = _dot(x, w_ref[:, SB_WIDTH + c0:SB_WIDTH + c0 + pair])
        v2 = _dot(x, w_ref[:, 2 * SB_WIDTH + c0:2 * SB_WIDTH + c0 + pair])
        for u in range(2):
            h = 2 * hp + u
            sl = slice(u * HEAD_DIM, (u + 1) * HEAD_DIM)
            if sample:
                q_ref[:, h] = q2[:, sl].astype(BF16).reshape(nb, DEC_SEQ, HEAD_DIM)
                kf_ref[:, h] = k2[:, sl].reshape(nb, DEC_SEQ, HEAD_DIM)
                vf_ref[:, h] = v2[:, sl].reshape(nb, DEC_SEQ, HEAD_DIM)
            else:
                qt_ref[h] = q2[:, sl].T.astype(BF16)
                kb_ref[h] = k2[:, sl].astype(BF16)
                kf_ref[h] = k2[:, sl]
                vf_ref[h] = v2[:, sl]
                vt = v2[:, sl].T.astype(BF16)
                for c in range(tm // ATTN_TK):
                    vt_ref[h, c] = vt[:, c * ATTN_TK:(c + 1) * ATTN_TK]
    c0 = 3 * SB_WIDTH
    uc_ref[...] = _dot(x, w_ref[:, c0:c0 + 2 * CONV_WIDTH])
    up_ref[...] = _dot(x, w_ref[:, c0 + 2 * CONV_WIDTH:])


def inproj_call(xn, w_in, sample):
    m, d = xn.shape
    tm = min(ROW_TILE, m)
    if sample:
        nbt = m // DEC_SEQ
        nb = tm // DEC_SEQ
        hshape = (nbt, HEADS, DEC_SEQ, HEAD_DIM)
        hspec = pl.BlockSpec((nb, HEADS, DEC_SEQ, HEAD_DIM), lambda i: (i, 0, 0, 0))
        out_shape = [jax.ShapeDtypeStruct(hshape, BF16),
                     jax.ShapeDtypeStruct(hshape, F32),
                     jax.ShapeDtypeStruct(hshape, F32)]
        out_specs = [hspec, hspec, hspec]
    else:
        nkb = m // ATTN_TK
        kshape = (HEADS, m, HEAD_DIM)
        kspec = pl.BlockSpec((HEADS, tm, HEAD_DIM), lambda i: (0, i, 0))
        out_shape = [jax.ShapeDtypeStruct((HEADS, HEAD_DIM, m), BF16),
                     jax.ShapeDtypeStruct(kshape, BF16),
                     jax.ShapeDtypeStruct((HEADS, nkb, HEAD_DIM, ATTN_TK), BF16),
                     jax.ShapeDtypeStruct(kshape, F32),
                     jax.ShapeDtypeStruct(kshape, F32)]
        out_specs = [pl.BlockSpec((HEADS, HEAD_DIM, tm), lambda i: (0, 0, i)),
                     kspec,
                     pl.BlockSpec((HEADS, tm // ATTN_TK, HEAD_DIM, ATTN_TK),
                                  lambda i: (0, i, 0, 0)),
                     kspec, kspec]
    out_shape += [jax.ShapeDtypeStruct((m, 2 * CONV_WIDTH), F32),
                  jax.ShapeDtypeStruct((m, POOL_WIDTH), F32)]
    out_specs += [pl.BlockSpec((tm, 2 * CONV_WIDTH), lambda i: (i, 0)),
                  pl.BlockSpec((tm, POOL_WIDTH), lambda i: (i, 0))]
    return pl.pallas_call(
        functools.partial(_inproj_kernel, tm=tm, sample=sample),
        grid=(m // tm,),
        in_specs=[pl.BlockSpec((tm, d), lambda i: (i, 0)),
                  pl.BlockSpec((d, IN_COLS), lambda i: (0, 0),
                               pipeline_mode=pl.Buffered(1))],
        out_specs=out_specs,
        out_shape=out_shape,
        compiler_params=_params("arbitrary"),
        name="inproj_sample" if sample else "inproj_prompt",
    )(xn, w_in)


def _attn_block(z, tri2_ref, hl_ref, carry, valid):
    tk = z.shape[0]
    sp = _softplus2(z)
    if valid is not None:
        sp = jnp.where(valid, sp, 0.0)
    hi, lo = _split_bf16(sp)
    hl_ref[0:tk, :] = hi
    hl_ref[tk:2 * tk, :] = lo
    cum = _dot(tri2_ref[...], hl_ref[...])
    w = jnp.exp2(z - cum - carry)
    if valid is not None:
        w = jnp.where(valid, w, 0.0)
    return w.astype(BF16), carry + cum[0:1, :]


def _attn_prompt_kernel(qt_ref, k_ref, vt_ref, tri2_ref, o_ref, acc_ref, hl_ref, *, tq, tk):
    i = pl.program_id(1)
    qt = qt_ref[...]
    acc_ref[...] = jnp.zeros_like(acc_ref)
    nd = tq // tk

    def block(j, carry, masked):
        kblk = k_ref[pl.ds(pl.multiple_of(j * tk, tk), tk), :]
        z = _dot(kblk, qt)
        valid = None
        if masked:
            s_idx = j * tk + lax.broadcasted_iota(jnp.int32, (tk, tq), 0)
            t_idx = i * tq + lax.broadcasted_iota(jnp.int32, (tk, tq), 1)
            valid = s_idx < t_idx
        w, carry = _attn_block(z, tri2_ref, hl_ref, carry, valid)
        acc_ref[...] += _dot(vt_ref[j], w)
        return carry

    carry = jnp.zeros((1, tq), F32)
    for r in range(nd - 1, -1, -1):
        carry = block(i * nd + r, carry, True)
    lax.fori_loop(0, i * nd, lambda n, c: block(i * nd - 1 - n, c, False), carry)
    o_ref[...] = acc_ref[...].T.astype(o_ref.dtype)


def _tri(tk):
    s = lax.broadcasted_iota(jnp.int32, (tk, tk), 0)
    j = lax.broadcasted_iota(jnp.int32, (tk, tk), 1)
    return (j >= s).astype(BF16)


def attn_prompt_call(qt, kb, vt):
    h, d, t = qt.shape
    tq = min(ATTN_TQ, t)
    tk = ATTN_TK
    tri = _tri(tk)
    tri2 = jnp.concatenate([tri, tri], axis=1)
    return pl.pallas_call(
        functools.partial(_attn_prompt_kernel, tq=tq, tk=tk),
        grid=(h, t // tq),
        in_specs=[pl.BlockSpec((None, d, tq), lambda hh, i: (hh, 0, i)),
                  pl.BlockSpec((None, t, d), lambda hh, i: (hh, 0, 0)),
                  pl.BlockSpec((None, t // tk, d, tk), lambda hh, i: (hh, 0, 0, 0)),
                  pl.BlockSpec((tk, 2 * tk), lambda hh, i: (0, 0))],
        out_specs=pl.BlockSpec((tq, d), lambda hh, i: (i, hh)),
        out_shape=jax.ShapeDtypeStruct((t, h * d), BF16),
        scratch_shapes=[pltpu.VMEM((d, tq), F32), pltpu.VMEM((2 * tk, tq), BF16)],
        compiler_params=_params("arbitrary", "arbitrary"),
        name="attn_prompt",
    )(qt, kb, vt, tri2)


def _attn_sample_kernel(q_ref, kn_ref, vn_ref, ck_ref, cv_ref, tri_ref, o_ref, *, past, tk):
    nt = (((1,), (1,)), ((), ()))
    q = q_ref[...]
    ts = q.shape[0]
    pad = jnp.zeros((LANE - ts, HEAD_DIM), BF16)
    kn = jnp.concatenate([kn_ref[...].astype(BF16), pad], axis=0)
    vn = jnp.concatenate([vn_ref[...].astype(BF16), pad], axis=0)
    z = lax.dot_general(q, kn, nt, preferred_element_type=F32)
    t_idx = lax.broadcasted_iota(jnp.int32, (ts, LANE), 0)
    s_idx = lax.broadcasted_iota(jnp.int32, (ts, LANE), 1)
    valid = s_idx < t_idx
    sp = jnp.where(valid, _softplus2(z), 0.0)
    hi, lo = _split_bf16(sp)
    tri_n = tri_ref[0:LANE, 0:LANE]
    cum = _dot(hi, tri_n) + _dot(lo, tri_n)
    w = jnp.where(valid, jnp.exp2(z - cum), 0.0)
    acc = _dot(w.astype(BF16), vn)
    carry = cum[:, 0:1]
    tri = tri_ref[...]
    for j in range(past // tk - 1, -1, -1):
        kb = ck_ref[j * tk:(j + 1) * tk, :].astype(BF16)
        z = lax.dot_general(q, kb, nt, preferred_element_type=F32)
        hi, lo = _split_bf16(_softplus2(z))
        cum = _dot(hi, tri) + _dot(lo, tri)
        w = jnp.exp2(z - cum - carry)
        acc = acc + _dot(w.astype(BF16), cv_ref[j * tk:(j + 1) * tk, :].astype(BF16))
        carry = carry + cum[:, 0:1]
    o_ref[...] = acc.astype(o_ref.dtype)


def attn_sample_call(q, kf, vf, cache_k, cache_v, layer):
    b, h, ts, d = q.shape
    past = cache_k.shape[3]
    tk = ATTN_TK
    tri = _tri(tk).T
    new_spec = pl.BlockSpec((None, None, ts, d), lambda bb, hh: (bb, hh, 0, 0))
    cache_spec = pl.BlockSpec((None, None, None, past, d),
                              lambda bb, hh: (layer, bb, hh, 0, 0))
    return pl.pallas_call(
        functools.partial(_attn_sample_kernel, past=past, tk=tk),
        grid=(b, h),
        in_specs=[new_spec, new_spec, new_spec, cache_spec, cache_spec,
                  pl.BlockSpec((tk, tk), lambda bb, hh: (0, 0))],
        out_specs=pl.BlockSpec((ts, d), lambda bb, hh: (bb, hh)),
        out_shape=jax.ShapeDtypeStruct((b * ts, h * d), BF16),
        compiler_params=_params("arbitrary", "arbitrary"),
        name="attn_sample",
    )(q, kf, vf, cache_k, cache_v, tri)


def _mixers_kernel(uc_ref, up_ref, hc_ref, hp_ref, dww_ref, dwb_ref, ng_ref, nb_ref,
                   pww_ref, pwb_ref, plw_ref, pls_ref,
                   oc_ref, op_ref, cs_ref, ps_ref,
                   ext_ref, sh_ref, yn_ref, pext_ref, pooled_ref,
                   *, tm, carry, start_pos):
    i = pl.program_id(0)
    hdr = HIST_ROWS

    def load_hist():
        ext_ref[0:hdr, :] = hc_ref[...]
        pext_ref[0:hdr, :] = hp_ref[...]

    if carry:
        pl.when(i == 0)(load_hist)
    else:
        load_hist()

    ext_ref[hdr:hdr + tm, :] = uc_ref[:, 0:CONV_WIDTH] * jax.nn.sigmoid(uc_ref[:, CONV_WIDTH:])
    pext_ref[hdr:hdr + tm, :] = up_ref[...]
    cs_ref[...] = ext_ref[pl.ds(hdr + tm - (CONV_KERNEL - 1), CONV_KERNEL - 1), :]
    ps_ref[...] = pext_ref[pl.ds(hdr + tm - POOL_HIST, POOL_HIST), :]

    sh_rows = sh_ref.shape[1]
    for r in range(1, 8):
        sh_ref[r - 1] = ext_ref[pl.ds(r, sh_rows), :]

    rc = min(CONV_CHUNK, tm)
    first = hdr - (CONV_KERNEL - 1)

    def conv_chunk(c, _):
        base = pl.multiple_of(c * rc, rc)
        acc = jnp.zeros((rc, CONV_WIDTH), F32)
        for j in range(CONV_KERNEL):
            o = first + j
            r, off = o % 8, (o // 8) * 8
            if r == 0:
                x = ext_ref[pl.ds(base + off, rc), :]
            else:
                x = sh_ref[r - 1, pl.ds(base + off, rc), :]
            acc = acc + x * dww_ref[j:j + 1, :]
        y = acc + dwb_ref[...]
        gw = CONV_WIDTH // CONV_GROUPS
        for g in range(CONV_GROUPS):
            sl = slice(g * gw, (g + 1) * gw)
            yg = y[:, sl]
            mu = jnp.mean(yg, axis=-1, keepdims=True)
            dv = yg - mu
            var = jnp.mean(dv * dv, axis=-1, keepdims=True)
            yn = dv * lax.rsqrt(var + EPS) * ng_ref[:, sl] + nb_ref[:, sl]
            yn = yn * jax.nn.sigmoid(yn)
            yn_ref[pl.ds(base, rc), sl] = yn.astype(BF16)
        return 0

    lax.fori_loop(0, tm // rc, conv_chunk, 0)
    oc_ref[...] = (_dot(yn_ref[...], pww_ref[...]) + pwb_ref[...]).astype(oc_ref.dtype)

    pc = min(LANE, tm)
    for c in range(tm // pc):
        b0 = hdr + c * pc
        row = lax.broadcasted_iota(jnp.int32, (pc, POOL_GROUP), 0) + c * pc
        pos = row + (start_pos + (i * tm if carry else 0))
        for g, w in enumerate(POOL_WINDOWS):
            sl = slice(g * POOL_GROUP, (g + 1) * POOL_GROUP)
            u = pext_ref[b0:b0 + pc, sl]
            s = u
            for dlt in range(1, w):
                s = s + pext_ref[pl.ds(b0 - dlt, pc), sl]
            cnt = jnp.minimum(w, pos + 1).astype(F32)
            pooled_ref[c * pc:(c + 1) * pc, sl] = (s / cnt - u).astype(BF16)
    op_ref[...] = (_dot(pooled_ref[...], plw_ref[...]) * pls_ref[...]).astype(op_ref.dtype)

    if carry:
        ext_ref[0:hdr, :] = ext_ref[tm:tm + hdr, :]
        pext_ref[0:hdr, :] = pext_ref[tm:tm + hdr, :]


def mixers_call(uc, up, hist_c, hist_p, lw, carry, start_pos):
    m = uc.shape[0]
    nseq = hist_c.shape[0]
    tm = min(ROW_TILE, m) if carry else m // nseq
    row = lambda i: (i, 0)
    const = lambda i: (0, 0)
    seq = (lambda i: (0, 0, 0)) if carry else (lambda i: (i, 0, 0))
    vec = pl.BlockSpec((1, CONV_WIDTH), const)
    sq = pl.BlockSpec((CONV_WIDTH, CONV_WIDTH), const)
    return pl.pallas_call(
        functools.partial(_mixers_kernel, tm=tm, carry=carry, start_pos=start_pos),
        grid=(m // tm,),
        in_specs=[pl.BlockSpec((tm, 2 * CONV_WIDTH), row),
                  pl.BlockSpec((tm, POOL_WIDTH), row),
                  pl.BlockSpec((None, HIST_ROWS, CONV_WIDTH), seq),
                  pl.BlockSpec((None, HIST_ROWS, POOL_WIDTH), seq),
                  pl.BlockSpec((HIST_ROWS, CONV_WIDTH), const),
                  vec, vec, vec, sq, vec, sq, vec],
        out_specs=[pl.BlockSpec((tm, CONV_WIDTH), row),
                   pl.BlockSpec((tm, POOL_WIDTH), row),
                   pl.BlockSpec((None, CONV_KERNEL - 1, CONV_WIDTH), seq),
                   pl.BlockSpec((None, POOL_HIST, POOL_WIDTH), seq)],
        out_shape=[jax.ShapeDtypeStruct((m, CONV_WIDTH), BF16),
                   jax.ShapeDtypeStruct((m, POOL_WIDTH), BF16),
                   jax.ShapeDtypeStruct((nseq, CONV_KERNEL - 1, CONV_WIDTH), F32),
                   jax.ShapeDtypeStruct((nseq, POOL_HIST, POOL_WIDTH), F32)],
        scratch_shapes=[pltpu.VMEM((HIST_ROWS + tm + 8, CONV_WIDTH), F32),
                        pltpu.VMEM((7, tm + HIST_ROWS - 8, CONV_WIDTH), F32),
                        pltpu.VMEM((tm, CONV_WIDTH), BF16),
                        pltpu.VMEM((HIST_ROWS + tm, POOL_WIDTH), F32),
                        pltpu.VMEM((tm, POOL_WIDTH), BF16)],
        compiler_params=_params("arbitrary"),
        name="mixers_prompt" if carry else "mixers_sample",
    )(uc, up, hist_c, hist_p, lw["dw_w"], lw["dw_b"], lw["n_g"], lw["n_b"],
      lw["pw_w"], lw["pw_b"], lw["pool_w"], lw["pool_s"])


def _outproj_kernel(h_ref, osb_ref, oc_ref, op_ref, w_ref, g_ref, hn_ref, xn_ref):
    c1 = SB_WIDTH
    c2 = SB_WIDTH + CONV_WIDTH
    mix = (_dot(osb_ref[...], w_ref[0:c1, :]) + _dot(oc_ref[...], w_ref[c1:c2, :])
           + _dot(op_ref[...], w_ref[c2:, :]))
    h = h_ref[...] + mix
    hn_ref[...] = h
    xn_ref[...] = _rmsnorm_rows(h, g_ref[...]).astype(xn_ref.dtype)


def outproj_call(h, osb, oc, op, w_out, g):
    m, d = h.shape
    tm = min(ROW_TILE, m)
    row = lambda i: (i, 0)
    return pl.pallas_call(
        _outproj_kernel,
        grid=(m // tm,),
        in_specs=[pl.BlockSpec((tm, d), row),
                  pl.BlockSpec((tm, SB_WIDTH), row),
                  pl.BlockSpec((tm, CONV_WIDTH), row),
                  pl.BlockSpec((tm, POOL_WIDTH), row),
                  pl.BlockSpec((d, d), lambda i: (0, 0), pipeline_mode=pl.Buffered(1)),
                  pl.BlockSpec((1, d), lambda i: (0, 0))],
        out_specs=[pl.BlockSpec((tm, d), row), pl.BlockSpec((tm, d), row)],
        out_shape=[jax.ShapeDtypeStruct((m, d), F32), jax.ShapeDtypeStruct((m, d), BF16)],
        compiler_params=_params("arbitrary"),
        name="outproj",
    )(h, osb, oc, op, w_out, g.reshape(1, d))


def _mlp_kernel(xn_ref, h_ref, wu_ref, wd_ref, g_ref, *rest, final):
    if final:
        y_ref, acc_ref = rest
    else:
        hn_ref, y_ref, acc_ref = rest
    f = pl.program_id(1)

    @pl.when(f == 0)
    def _():
        acc_ref[...] = jnp.zeros_like(acc_ref)

    a = jnp.maximum(_dot(xn_ref[...], wu_ref[...]), 0.0)
    acc_ref[...] += _dot((a * a).astype(BF16), wd_ref[...])

    @pl.when(f == pl.num_programs(1) - 1)
    def _():
        h = h_ref[...] + acc_ref[...]
        if not final:
            hn_ref[...] = h
        y_ref[...] = _rmsnorm_rows(h, g_ref[...]).astype(y_ref.dtype)


def mlp_call(xn, h, w_up, w_down, g, final):
    m, d = h.shape
    ff = w_up.shape[1]
    tm = min(ROW_TILE, m)
    tf = FF_TILE
    row = lambda i, f: (i, 0)
    y_dtype = F32 if final else BF16
    out_specs = [pl.BlockSpec((tm, d), row)]
    out_shape = [jax.ShapeDtypeStruct((m, d), y_dtype)]
    if not final:
        out_specs = [pl.BlockSpec((tm, d), row)] + out_specs
        out_shape = [jax.ShapeDtypeStruct((m, d), F32)] + out_shape
    return pl.pallas_call(
        functools.partial(_mlp_kernel, final=final),
        grid=(m // tm, ff // tf),
        in_specs=[pl.BlockSpec((tm, d), row),
                  pl.BlockSpec((tm, d), row),
                  pl.BlockSpec((d, tf), lambda i, f: (0, f)),
                  pl.BlockSpec((tf, d), lambda i, f: (f, 0)),
                  pl.BlockSpec((1, d), lambda i, f: (0, 0))],
        out_specs=out_specs,
        out_shape=out_shape,
        scratch_shapes=[pltpu.VMEM((tm, d), F32)],
        compiler_params=_params("arbitrary", "arbitrary"),
        name="mlp_final" if final else "mlp",
    )(xn, h, w_up, w_down, g.reshape(1, d))


def _pad_hist(hist, rows):
    return jnp.pad(hist, ((0, 0), (rows - hist.shape[1], 0), (0, 0)))


def _block_diag(w):
    g, a, b = w.shape
    eye = jnp.eye(g, dtype=w.dtype)
    return (eye[:, None, :, None] * w[:, :, None, :]).reshape(g * a, g * b)


def _group_trunk(x, layers, final_g, sample, caches):
    depth = len(layers)
    h = x
    xn = rmsnorm_call(x, layers[0]["norm_mix_g"])
    ks, vs, cs, ps = [], [], [], []
    y = None
    for l, lw in enumerate(layers):
        if sample:
            cache_k, cache_v, cache_conv, state_pool = caches
            q, kf, vf, uc, up = inproj_call(xn, lw["w_in"], True)
            osb = attn_sample_call(q, kf, vf, cache_k, cache_v, l)
            hist_c = _pad_hist(cache_conv[l], HIST_ROWS)
            hist_p = _pad_hist(state_pool[l], HIST_ROWS)
            start = cache_k.shape[3]
        else:
            qt, kb, vt, kf, vf, uc, up = inproj_call(xn, lw["w_in"], False)
            osb = attn_prompt_call(qt, kb, vt)
            hist_c = jnp.zeros((1, HIST_ROWS, CONV_WIDTH), F32)
            hist_p = jnp.zeros((1, HIST_ROWS, POOL_WIDTH), F32)
            start = 0
        oc, op, c_state, p_state = mixers_call(uc, up, hist_c, hist_p, lw, not sample, start)
        h, xn2 = outproj_call(h, osb, oc, op, lw["w_out"], lw["norm_mlp_g"])
        if l + 1 < depth:
            h, xn = mlp_call(xn2, h, lw["w_up"], lw["w_down"], layers[l + 1]["norm_mix_g"], False)
        else:
            (y,) = mlp_call(xn2, h, lw["w_up"], lw["w_down"], final_g, True)
        ks.append(kf); vs.append(vf); cs.append(c_state); ps.append(p_state)
    return y, jnp.stack(ks), jnp.stack(vs), jnp.stack(cs), jnp.stack(ps)


def kernel(x_prompt, x_sample, cache_k, cache_v, cache_conv, state_pool, norm_mix_g, w_in,
           conv_dw_w, conv_dw_b, conv_norm_g, conv_norm_b, conv_pw_w, conv_pw_b, pool_w,
           pool_scale, w_out, norm_mlp_g, w_up, w_down, final_norm_g):
    depth = w_in.shape[0]
    bp, seq, d = x_prompt.shape
    bs, dseq, _ = x_sample.shape
    assert bp == 1 and dseq == DEC_SEQ and d == D_MODEL
    layers = []
    for l in range(depth):
        layers.append(dict(
            norm_mix_g=norm_mix_g[l],
            w_in=w_in[l].astype(BF16),
            dw_w=jnp.pad(conv_dw_w[l], ((0, HIST_ROWS - CONV_KERNEL), (0, 0))),
            dw_b=conv_dw_b[l].reshape(1, -1),
            n_g=conv_norm_g[l].reshape(1, -1),
            n_b=conv_norm_b[l].reshape(1, -1),
            pw_w=conv_pw_w[l].astype(BF16),
            pw_b=conv_pw_b[l].reshape(1, -1),
            pool_w=_block_diag(pool_w[l]).astype(BF16),
            pool_s=pool_scale[l].reshape(1, -1),
            w_out=w_out[l].astype(BF16),
            norm_mlp_g=norm_mlp_g[l],
            w_up=w_up[l].astype(BF16),
            w_down=w_down[l].astype(BF16),
        ))
    yp, kp, vp, cp, pp = _group_trunk(x_prompt.reshape(seq, d), layers, final_norm_g, False, None)
    ys, ks, vs, cs, ps = _group_trunk(x_sample.reshape(bs * dseq, d), layers, final_norm_g, True,
                                      (cache_k, cache_v, cache_conv, state_pool))
    return (yp.reshape(bp, seq, d), ys.reshape(bs, dseq, d),
            kp[:, None], vp[:, None], cp, pp, ks, vs, cs, ps)
```

```python
import functools
import math

import jax
import jax.numpy as jnp
from jax import lax
from jax.experimental import pallas as pl
from jax.experimental.pallas import tpu as pltpu

F32 = jnp.float32
BF16 = jnp.bfloat16

D_MODEL = 2048
SB_WIDTH = 1024
HEAD_DIM = 128
HEADS = SB_WIDTH // HEAD_DIM
CONV_WIDTH = 512
CONV_KERNEL = 31
CONV_GROUPS = 4
POOL_WIDTH = 512
POOL_WINDOWS = (2, 4, 8, 16)
POOL_GROUP = POOL_WIDTH // len(POOL_WINDOWS)
POOL_HIST = max(POOL_WINDOWS) - 1
D_FF = 4 * D_MODEL
IN_COLS = 3 * SB_WIDTH + 2 * CONV_WIDTH + POOL_WIDTH
EPS = 1e-6
DEC_SEQ = 64

LOG2E = 1.4426950408889634
Q_SCALE = HEAD_DIM ** -0.5 * LOG2E

LANE = 128
HIST_ROWS = 32
ROW_TILE = 512
FF_TILE = 512
ATTN_TK = 256
ATTN_VT = 2 * ATTN_TK
CONV_CHUNK = 64
VMEM_LIMIT = 56 * 1024 * 1024


def _params(*sem):
    return pltpu.CompilerParams(dimension_semantics=sem, vmem_limit_bytes=VMEM_LIMIT)


def _dot(a, b):
    return jnp.dot(a, b, preferred_element_type=F32)


def _softplus2(z):
    e = jnp.exp2(-jnp.abs(z))
    return jnp.maximum(z, 0.0) + jnp.log(1.0 + e) * LOG2E


def _split_bf16(x):
    hi = x.astype(BF16)
    lo = (x - hi.astype(F32)).astype(BF16)
    return hi, lo


def _rmsnorm_rows(x, g):
    ms = jnp.mean(x * x, axis=-1, keepdims=True)
    return x * lax.rsqrt(ms + EPS) * g


def _rmsnorm_kernel(x_ref, g_ref, o_ref):
    o_ref[...] = _rmsnorm_rows(x_ref[...], g_ref[...]).astype(o_ref.dtype)


def rmsnorm_call(x, g):
    m, d = x.shape
    tm = min(ROW_TILE, m)
    return pl.pallas_call(
        _rmsnorm_kernel,
        grid=(m // tm,),
        in_specs=[pl.BlockSpec((tm, d), lambda i: (i, 0)),
                  pl.BlockSpec((1, d), lambda i: (0, 0))],
        out_specs=pl.BlockSpec((tm, d), lambda i: (i, 0)),
        out_shape=jax.ShapeDtypeStruct((m, d), BF16),
        compiler_params=_params("arbitrary"),
        name="rmsnorm",
    )(x, g.reshape(1, d))


def _inproj_kernel(xn_ref, w_ref, *rest, tm, sample, n_alias):
    outs = rest[n_alias:]
    x = xn_ref[...]
    if sample:
        q_ref, kf_ref, vf_ref, uc_ref, up_ref = outs
        nb = tm // DEC_SEQ
    else:
        qt_ref, kb_ref, vt_ref, kf_ref, vf_ref, uc_ref, up_ref = outs
    pair = 2 * HEAD_DIM
    for hp in range(HEADS // 2):
        c0 = hp * pair
        q2 = _dot(x, w_ref[:, c0:c0 + pair]) * Q_SCALE
        k2 = _dot(x, w_ref[:, SB_WIDTH + c0:SB_WIDTH + c0 + pair])
        v2 = _dot(x, w_ref[:, 2 * SB_WIDTH + c0:2 * SB_WIDTH + c0 + pair])
        for u in range(2):
            h = 2 * hp + u
            sl = slice(u * HEAD_DIM, (u + 1) * HEAD_DIM)
            if sample:
                q_ref[:, h] = q2[:, sl].astype(BF16).reshape(nb, DEC_SEQ, HEAD_DIM)
                kf_ref[:, h] = k2[:, sl].reshape(nb, DEC_SEQ, HEAD_DIM)
                vf_ref[:, h] = v2[:, sl].reshape(nb, DEC_SEQ, HEAD_DIM)
            else:
                qt_ref[h] = q2[:, sl].T.astype(BF16)
                kb_ref[h] = k2[:, sl].astype(BF16)
                kf_ref[h] = k2[:, sl]
                vf_ref[h] = v2[:, sl]
                vt = v2[:, sl].T.astype(BF16)
                for c in range(tm // ATTN_VT):
                    vt_ref[h, c] = vt[:, c * ATTN_VT:(c + 1) * ATTN_VT]
    c0 = 3 * SB_WIDTH
    uc_ref[...] = _dot(x, w_ref[:, c0:c0 + 2 * CONV_WIDTH])
    up_ref[...] = _dot(x, w_ref[:, c0 + 2 * CONV_WIDTH:])


def kv_stack_shape(m, sample, depth):
    if sample:
        return (depth, m // DEC_SEQ, HEADS, DEC_SEQ, HEAD_DIM)
    return (depth, HEADS, m, HEAD_DIM)


def inproj_call(xn, w_in, sample, layer, kv_stack):
    depth = kv_stack[0].shape[0]
    m, d = xn.shape
    tm = min(ROW_TILE, m)
    if sample:
        nbt = m // DEC_SEQ
        nb = tm // DEC_SEQ
        hshape = (nbt, HEADS, DEC_SEQ, HEAD_DIM)
        hblock = (nb, HEADS, DEC_SEQ, HEAD_DIM)
        kv_shape = jax.ShapeDtypeStruct((depth,) + hshape, F32)
        kv_spec = pl.BlockSpec((None,) + hblock, lambda i: (layer, i, 0, 0, 0))
        out_shape = [jax.ShapeDtypeStruct(hshape, BF16), kv_shape, kv_shape]
        out_specs = [pl.BlockSpec(hblock, lambda i: (i, 0, 0, 0)), kv_spec, kv_spec]
        kv_index = (1, 2)
    else:
        nkb = m // ATTN_VT
        kshape = (HEADS, m, HEAD_DIM)
        kblock = (HEADS, tm, HEAD_DIM)
        kv_shape = jax.ShapeDtypeStruct((depth,) + kshape, F32)
        kv_spec = pl.BlockSpec((None,) + kblock, lambda i: (layer, 0, i, 0))
        out_shape = [jax.ShapeDtypeStruct((HEADS, HEAD_DIM, m), BF16),
                     jax.ShapeDtypeStruct(kshape, BF16),
                     jax.ShapeDtypeStruct((HEADS, nkb, HEAD_DIM, ATTN_VT), BF16),
                     kv_shape, kv_shape]
        out_specs = [pl.BlockSpec((HEADS, HEAD_DIM, tm), lambda i: (0, 0, i)),
                     pl.BlockSpec(kblock, lambda i: (0, i, 0)),
                     pl.BlockSpec((HEADS, tm // ATTN_VT, HEAD_DIM, ATTN_VT),
                                  lambda i: (0, i, 0, 0)),
                     kv_spec, kv_spec]
        kv_index = (3, 4)
    out_shape += [jax.ShapeDtypeStruct((m, 2 * CONV_WIDTH), F32),
                  jax.ShapeDtypeStruct((m, POOL_WIDTH), F32)]
    out_specs += [pl.BlockSpec((tm, 2 * CONV_WIDTH), lambda i: (i, 0)),
                  pl.BlockSpec((tm, POOL_WIDTH), lambda i: (i, 0))]
    in_specs = [pl.BlockSpec((tm, d), lambda i: (i, 0)),
                pl.BlockSpec((d, IN_COLS), lambda i: (0, 0), pipeline_mode=pl.Buffered(1))]
    args = [xn, w_in]
    aliases = {}
    for n, stack in enumerate(kv_stack):
        assert stack.shape == kv_shape.shape and stack.dtype == kv_shape.dtype
        aliases[len(args)] = kv_index[n]
        in_specs.append(pl.BlockSpec(memory_space=pl.ANY))
        args.append(stack)
    outs = pl.pallas_call(
        functools.partial(_inproj_kernel, tm=tm, sample=sample, n_alias=len(aliases)),
        grid=(m // tm,),
        in_specs=in_specs,
        out_specs=out_specs,
        out_shape=out_shape,
        input_output_aliases=aliases,
        compiler_params=_params("arbitrary"),
        name="inproj_sample" if sample else "inproj_prompt",
    )(*args)
    kv = tuple(outs[n] for n in kv_index)
    return [o for n, o in enumerate(outs) if n not in kv_index], kv


MASKED_LOGIT = -1e30


def _attn_stage1(k_ref, qt, p, i, masked, z_ref, hl_ref, *, tq, tk):
    for half in (1, 0):
        row0 = pl.multiple_of(p * (2 * tk) + half * tk, tk)
        z = _dot(k_ref[pl.ds(row0, tk), :], qt)
        sp = _softplus2(z)
        if masked:
            s_idx = row0 + lax.broadcasted_iota(jnp.int32, (tk, tq), 0)
            t_idx = i * tq + lax.broadcasted_iota(jnp.int32, (tk, tq), 1)
            valid = s_idx < t_idx
            sp = jnp.where(valid, sp, 0.0)
            z = jnp.where(valid, z, MASKED_LOGIT)
        hi, lo = _split_bf16(sp)
        z_ref[half * tk:(half + 1) * tk, :] = z
        hl_ref[half, 0:tk, :] = hi
        hl_ref[half, tk:2 * tk, :] = lo


def _attn_stage2(tri2_ref, z_ref, hl_ref, half, carry, *, tk):
    cum = _dot(tri2_ref[...], hl_ref[half])
    w = jnp.exp2(z_ref[half * tk:(half + 1) * tk, :] - cum - carry)
    return w.astype(BF16), carry + cum[0:1, :]


ATTN_HEADS_PER_STEP = 2
ATTN_STAGE_BUFS = 2
PV_LAG = 2


def _attn_prompt_kernel(qt_ref, k_ref, vt_ref, tri2_ref, o_ref, *scratch, tq, tk):
    i = pl.program_id(1)
    nh = ATTN_HEADS_PER_STEP
    heads = range(nh)
    acc = scratch[0:nh]
    zb = [scratch[nh + b * nh:nh + (b + 1) * nh] for b in range(ATTN_STAGE_BUFS)]
    hl0 = nh * (1 + ATTN_STAGE_BUFS)
    hlb = [scratch[hl0 + b * nh:hl0 + (b + 1) * nh] for b in range(ATTN_STAGE_BUFS)]
    s1 = functools.partial(_attn_stage1, tq=tq, tk=tk)
    s2 = functools.partial(_attn_stage2, tk=tk)
    qts = [qt_ref[h] for h in heads]
    for h in heads:
        acc[h][...] = jnp.zeros_like(acc[h])

    def second(buf, p, carry):
        carry = list(carry)
        units = [(h, half) for h in heads for half in (1, 0)]
        ws = {}
        for n in range(len(units) + PV_LAG):
            if n < len(units):
                h, half = units[n]
                ws[units[n]], carry[h] = s2(tri2_ref, zb[buf][h], hlb[buf][h], half, carry[h])
            if n >= PV_LAG:
                h, half = units[n - PV_LAG]
                vt = vt_ref[h, p, :, half * tk:(half + 1) * tk]
                acc[h][...] += _dot(vt, ws.pop(units[n - PV_LAG]))
        return tuple(carry)

    def step(buf, p, carry):
        for h in heads:
            s1(k_ref.at[h], qts[h], jnp.maximum(p - 1, 0), i, False,
               zb[1 - buf][h], hlb[1 - buf][h])
        return second(buf, p, carry)

    for h in heads:
        s1(k_ref.at[h], qts[h], i, i, True, zb[0][h], hlb[0][h])

    def body(m, carry):
        p = i - 2 * m
        return step(1, p - 1, step(0, p, carry))

    carry = lax.fori_loop(0, (i + 1) // 2, body,
                          tuple(jnp.zeros((1, tq), F32) for _ in heads))

    @pl.when(i % 2 == 0)
    def _():
        second(0, 0, carry)

    d = acc[0].shape[0]
    for h in heads:
        o_ref[:, h * d:(h + 1) * d] = acc[h][...].T.astype(o_ref.dtype)


def _tri(tk):
    s = lax.broadcasted_iota(jnp.int32, (tk, tk), 0)
    j = lax.broadcasted_iota(jnp.int32, (tk, tk), 1)
    return (j >= s).astype(BF16)


def attn_prompt_call(qt, kb, vt):
    h, d, t = qt.shape
    tq = ATTN_VT
    tk = ATTN_TK
    tri = _tri(tk)
    tri2 = jnp.concatenate([tri, tri], axis=1)
    nh = ATTN_HEADS_PER_STEP
    nstage = nh * ATTN_STAGE_BUFS
    return pl.pallas_call(
        functools.partial(_attn_prompt_kernel, tq=tq, tk=tk),
        grid=(h // nh, t // tq),
        in_specs=[pl.BlockSpec((nh, d, tq), lambda hh, i: (hh, 0, i)),
                  pl.BlockSpec((nh, t, d), lambda hh, i: (hh, 0, 0)),
                  pl.BlockSpec((nh, t // ATTN_VT, d, ATTN_VT), lambda hh, i: (hh, 0, 0, 0)),
                  pl.BlockSpec((tk, 2 * tk), lambda hh, i: (0, 0))],
        out_specs=pl.BlockSpec((tq, nh * d), lambda hh, i: (i, hh)),
        out_shape=jax.ShapeDtypeStruct((t, h * d), BF16),
        scratch_shapes=([pltpu.VMEM((d, tq), F32)] * nh
                        + [pltpu.VMEM((2 * tk, tq), F32)] * nstage
                        + [pltpu.VMEM((2, 2 * tk, tq), BF16)] * nstage),
        compiler_params=_params("arbitrary", "arbitrary"),
        name="attn_prompt",
    )(qt, kb, vt, tri2)


SAMPLE_HEADS_PER_STEP = 4


def _attn_sample_kernel(q_ref, kn_ref, vn_ref, ck_ref, cv_ref, tri_ref, o_ref,
                        kb_ref, vb_ref, z_ref, hl_ref, cum_ref, w_ref, *, past, tk):
    nt = (((1,), (1,)), ((), ()))
    nh, ts, d = q_ref.shape
    rows = nh * ts
    nblk = past // tk + 1
    tot = nblk * tk
    for h in range(nh):
        for src, new, dst in ((ck_ref, kn_ref, kb_ref), (cv_ref, vn_ref, vb_ref)):
            dst[h, 0:past, :] = src[h].astype(BF16)
            dst[h, past:past + ts, :] = new[h].astype(BF16)
            dst[h, past + ts:tot, :] = jnp.zeros((tot - past - ts, d), BF16)
        z_ref[h * ts:(h + 1) * ts, :] = lax.dot_general(
            q_ref[h], kb_ref[h], nt, preferred_element_type=F32)

    for b in range(nblk):
        cols = slice(b * tk, (b + 1) * tk)
        z = z_ref[:, cols]
        sp = _softplus2(z)
        if b == nblk - 1:
            t_idx = lax.broadcasted_iota(jnp.int32, (rows, tk), 0) % ts
            s_idx = lax.broadcasted_iota(jnp.int32, (rows, tk), 1)
            valid = s_idx < t_idx
            sp = jnp.where(valid, sp, 0.0)
            z_ref[:, cols] = jnp.where(valid, z, MASKED_LOGIT)
        hi, lo = _split_bf16(sp)
        hl_ref[b * rows:(b + 1) * rows, 0:tk] = hi
        hl_ref[b * rows:(b + 1) * rows, tk:2 * tk] = lo

    cum_ref[...] = _dot(hl_ref[...], tri_ref[...])

    carry = jnp.zeros((rows, 1), F32)
    for b in range(nblk - 1, -1, -1):
        cols = slice(b * tk, (b + 1) * tk)
        cum = cum_ref[b * rows:(b + 1) * rows, :]
        w_ref[:, cols] = jnp.exp2(z_ref[:, cols] - cum - carry).astype(BF16)
        carry = carry + cum[:, 0:1]

    for h in range(nh):
        o_ref[:, h * d:(h + 1) * d] = _dot(w_ref[h * ts:(h + 1) * ts, :],
                                           vb_ref[h]).astype(o_ref.dtype)


def attn_sample_call(q, k_stack, v_stack, cache_k, cache_v, layer):
    b, h, ts, d = q.shape
    past = cache_k.shape[3]
    tk = ATTN_TK
    nh = SAMPLE_HEADS_PER_STEP
    rows = nh * ts
    tot = past + tk
    tri = _tri(tk).T
    tri2 = jnp.concatenate([tri, tri], axis=0)
    q_spec = pl.BlockSpec((None, nh, ts, d), lambda bb, hh: (bb, hh, 0, 0))
    new_spec = pl.BlockSpec((None, None, nh, ts, d), lambda bb, hh: (layer, bb, hh, 0, 0))
    cache_spec = pl.BlockSpec((None, None, nh, past, d), lambda bb, hh: (layer, bb, hh, 0, 0))
    return pl.pallas_call(
        functools.partial(_attn_sample_kernel, past=past, tk=tk),
        grid=(b, h // nh),
        in_specs=[q_spec, new_spec, new_spec, cache_spec, cache_spec,
                  pl.BlockSpec((2 * tk, tk), lambda bb, hh: (0, 0))],
        out_specs=pl.BlockSpec((ts, nh * d), lambda bb, hh: (bb, hh)),
        out_shape=jax.ShapeDtypeStruct((b * ts, h * d), BF16),
        scratch_shapes=[pltpu.VMEM((nh, tot, d), BF16), pltpu.VMEM((nh, tot, d), BF16),
                        pltpu.VMEM((rows, tot), F32),
                        pltpu.VMEM((tot // tk * rows, 2 * tk), BF16),
                        pltpu.VMEM((tot // tk * rows, tk), F32),
                        pltpu.VMEM((rows, tot), BF16)],
        compiler_params=_params("arbitrary", "arbitrary"),
        name="attn_sample",
    )(q, k_stack, v_stack, cache_k, cache_v, tri2)


def _mixers_kernel(uc_ref, up_ref, hc_ref, hp_ref, dww_ref, dwb_ref, ng_ref, nb_ref,
                   pww_ref, pwb_ref, plw_ref, pls_ref,
                   oc_ref, op_ref, cs_ref, ps_ref,
                   ext_ref, sh_ref, yn_ref, pext_ref, pooled_ref,
                   *, tm, carry, start_pos):
    i = pl.program_id(0)
    hdr = HIST_ROWS

    def load_hist():
        ext_ref[0:hdr, :] = hc_ref[...]
        pext_ref[0:hdr, :] = hp_ref[...]

    if carry:
        pl.when(i == 0)(load_hist)
    else:
        load_hist()

    ext_ref[hdr:hdr + tm, :] = uc_ref[:, 0:CONV_WIDTH] * jax.nn.sigmoid(uc_ref[:, CONV_WIDTH:])
    pext_ref[hdr:hdr + tm, :] = up_ref[...]
    cs_ref[...] = ext_ref[pl.ds(hdr + tm - (CONV_KERNEL - 1), CONV_KERNEL - 1), :]
    ps_ref[...] = pext_ref[pl.ds(hdr + tm - POOL_HIST, POOL_HIST), :]

    sh_rows = sh_ref.shape[1]
    for r in range(1, 8):
        sh_ref[r - 1] = ext_ref[pl.ds(r, sh_rows), :]

    rc = min(CONV_CHUNK, tm)
    first = hdr - (CONV_KERNEL - 1)

    def conv_chunk(c, _):
        base = pl.multiple_of(c * rc, rc)
        acc = jnp.zeros((rc, CONV_WIDTH), F32)
        for j in range(CONV_KERNEL):
            o = first + j
            r, off = o % 8, (o // 8) * 8
            if r == 0:
                x = ext_ref[pl.ds(base + off, rc), :]
            else:
                x = sh_ref[r - 1, pl.ds(base + off, rc), :]
            acc = acc + x * dww_ref[j:j + 1, :]
        y = acc + dwb_ref[...]
        gw = CONV_WIDTH // CONV_GROUPS
        for g in range(CONV_GROUPS):
            sl = slice(g * gw, (g + 1) * gw)
            yg = y[:, sl]
            mu = jnp.mean(yg, axis=-1, keepdims=True)
            dv = yg - mu
            var = jnp.mean(dv * dv, axis=-1, keepdims=True)
            yn = dv * lax.rsqrt(var + EPS) * ng_ref[:, sl] + nb_ref[:, sl]
            yn = yn * jax.nn.sigmoid(yn)
            yn_ref[pl.ds(base, rc), sl] = yn.astype(BF16)
        return 0

    lax.fori_loop(0, tm // rc, conv_chunk, 0)
    oc_ref[...] = (_dot(yn_ref[...], pww_ref[...]) + pwb_ref[...]).astype(oc_ref.dtype)

    pc = min(LANE, tm)
    for c in range(tm // pc):
        b0 = hdr + c * pc
        row = lax.broadcasted_iota(jnp.int32, (pc, POOL_GROUP), 0) + c * pc
        pos = row + (start_pos + (i * tm if carry else 0))
        for g, w in enumerate(POOL_WINDOWS):
            sl = slice(g * POOL_GROUP, (g + 1) * POOL_GROUP)
            u = pext_ref[b0:b0 + pc, sl]
            s = u
            for dlt in range(1, w):
                s = s + pext_ref[pl.ds(b0 - dlt, pc), sl]
            cnt = jnp.minimum(w, pos + 1).astype(F32)
            pooled_ref[c * pc:(c + 1) * pc, sl] = (s / cnt - u).astype(BF16)
    op_ref[...] = (_dot(pooled_ref[...], plw_ref[...]) * pls_ref[...]).astype(op_ref.dtype)

    if carry:
        ext_ref[0:hdr, :] = ext_ref[tm:tm + hdr, :]
        pext_ref[0:hdr, :] = pext_ref[tm:tm + hdr, :]


def mixers_call(uc, up, hist_c, hist_p, lw, carry, start_pos):
    m = uc.shape[0]
    nseq = hist_c.shape[0]
    tm = min(ROW_TILE, m) if carry else m // nseq
    row = lambda i: (i, 0)
    const = lambda i: (0, 0)
    seq = (lambda i: (0, 0, 0)) if carry else (lambda i: (i, 0, 0))
    vec = pl.BlockSpec((1, CONV_WIDTH), const)
    sq = pl.BlockSpec((CONV_WIDTH, CONV_WIDTH), const)
    return pl.pallas_call(
        functools.partial(_mixers_kernel, tm=tm, carry=carry, start_pos=start_pos),
        grid=(m // tm,),
        in_specs=[pl.BlockSpec((tm, 2 * CONV_WIDTH), row),
                  pl.BlockSpec((tm, POOL_WIDTH), row),
                  pl.BlockSpec((None, HIST_ROWS, CONV_WIDTH), seq),
                  pl.BlockSpec((None, HIST_ROWS, POOL_WIDTH), seq),
                  pl.BlockSpec((HIST_ROWS, CONV_WIDTH), const),
                  vec, vec, vec, sq, vec, sq, vec],
        out_specs=[pl.BlockSpec((tm, CONV_WIDTH), row),
                   pl.BlockSpec((tm, POOL_WIDTH), row),
                   pl.BlockSpec((None, CONV_KERNEL - 1, CONV_WIDTH), seq),
                   pl.BlockSpec((None, POOL_HIST, POOL_WIDTH), seq)],
        out_shape=[jax.ShapeDtypeStruct((m, CONV_WIDTH), BF16),
                   jax.ShapeDtypeStruct((m, POOL_WIDTH), BF16),
                   jax.ShapeDtypeStruct((nseq, CONV_KERNEL - 1, CONV_WIDTH), F32),
                   jax.ShapeDtypeStruct((nseq, POOL_HIST, POOL_WIDTH), F32)],
        scratch_shapes=[pltpu.VMEM((HIST_ROWS + tm + 8, CONV_WIDTH), F32),
                        pltpu.VMEM((7, tm + HIST_ROWS - 8, CONV_WIDTH), F32),
                        pltpu.VMEM((tm, CONV_WIDTH), BF16),
                        pltpu.VMEM((HIST_ROWS + tm, POOL_WIDTH), F32),
                        pltpu.VMEM((tm, POOL_WIDTH), BF16)],
        compiler_params=_params("arbitrary"),
        name="mixers_prompt" if carry else "mixers_sample",
    )(uc, up, hist_c, hist_p, lw["dw_w"], lw["dw_b"], lw["n_g"], lw["n_b"],
      lw["pw_w"], lw["pw_b"], lw["pool_w"], lw["pool_s"])


def _outproj_kernel(h_ref, osb_ref, oc_ref, op_ref, w_ref, g_ref, hn_ref, xn_ref):
    c1 = SB_WIDTH
    c2 = SB_WIDTH + CONV_WIDTH
    mix = (_dot(osb_ref[...], w_ref[0:c1, :]) + _dot(oc_ref[...], w_ref[c1:c2, :])
           + _dot(op_ref[...], w_ref[c2:, :]))
    h = h_ref[...] + mix
    hn_ref[...] = h
    xn_ref[...] = _rmsnorm_rows(h, g_ref[...]).astype(xn_ref.dtype)


def outproj_call(h, osb, oc, op, w_out, g):
    m, d = h.shape
    tm = min(ROW_TILE, m)
    row = lambda i: (i, 0)
    return pl.pallas_call(
        _outproj_kernel,
        grid=(m // tm,),
        in_specs=[pl.BlockSpec((tm, d), row),
                  pl.BlockSpec((tm, SB_WIDTH), row),
                  pl.BlockSpec((tm, CONV_WIDTH), row),
                  pl.BlockSpec((tm, POOL_WIDTH), row),
                  pl.BlockSpec((d, d), lambda i: (0, 0), pipeline_mode=pl.Buffered(1)),
                  pl.BlockSpec((1, d), lambda i: (0, 0))],
        out_specs=[pl.BlockSpec((tm, d), row), pl.BlockSpec((tm, d), row)],
        out_shape=[jax.ShapeDtypeStruct((m, d), F32), jax.ShapeDtypeStruct((m, d), BF16)],
        compiler_params=_params("arbitrary"),
        name="outproj",
    )(h, osb, oc, op, w_out, g.reshape(1, d))


def _mlp_kernel(xn_ref, h_ref, wu_ref, wd_ref, g_ref, *rest, final):
    if final:
        y_ref, acc_ref = rest
    else:
        hn_ref, y_ref, acc_ref = rest
    f = pl.program_id(1)

    @pl.when(f == 0)
    def _():
        acc_ref[...] = jnp.zeros_like(acc_ref)

    a = jnp.maximum(_dot(xn_ref[...], wu_ref[...]), 0.0)
    acc_ref[...] += _dot((a * a).astype(BF16), wd_ref[...])

    @pl.when(f == pl.num_programs(1) - 1)
    def _():
        h = h_ref[...] + acc_ref[...]
        if not final:
            hn_ref[...] = h
        y_ref[...] = _rmsnorm_rows(h, g_ref[...]).astype(y_ref.dtype)


def mlp_call(xn, h, w_up, w_down, g, final):
    m, d = h.shape
    ff = w_up.shape[1]
    tm = min(ROW_TILE, m)
    tf = FF_TILE
    row = lambda i, f: (i, 0)
    y_dtype = F32 if final else BF16
    out_specs = [pl.BlockSpec((tm, d), row)]
    out_shape = [jax.ShapeDtypeStruct((m, d), y_dtype)]
    if not final:
        out_specs = [pl.BlockSpec((tm, d), row)] + out_specs
        out_shape = [jax.ShapeDtypeStruct((m, d), F32)] + out_shape
    return pl.pallas_call(
        functools.partial(_mlp_kernel, final=final),
        grid=(m // tm, ff // tf),
        in_specs=[pl.BlockSpec((tm, d), row),
                  pl.BlockSpec((tm, d), row),
                  pl.BlockSpec((d, tf), lambda i, f: (0, f)),
                  pl.BlockSpec((tf, d), lambda i, f: (f, 0)),
                  pl.BlockSpec((1, d), lambda i, f: (0, 0))],
        out_specs=out_specs,
        out_shape=out_shape,
        scratch_shapes=[pltpu.VMEM((tm, d), F32)],
        compiler_params=_params("arbitrary", "arbitrary"),
        name="mlp_final" if final else "mlp",
    )(xn, h, w_up, w_down, g.reshape(1, d))


def _pad_hist(hist, rows):
    return jnp.pad(hist, ((0, 0), (rows - hist.shape[1], 0), (0, 0)))


def _block_diag(w):
    g, a, b = w.shape
    eye = jnp.eye(g, dtype=w.dtype)
    return (eye[:, None, :, None] * w[:, :, None, :]).reshape(g * a, g * b)


def _group_trunk(x, layers, final_g, sample, caches):
    depth = len(layers)
    h = x
    xn = rmsnorm_call(x, layers[0]["norm_mix_g"])
    cs, ps = [], []
    y = None
    shape = kv_stack_shape(x.shape[0], sample, depth)
    kv = (jnp.zeros(shape, F32), jnp.zeros(shape, F32))
    for l, lw in enumerate(layers):
        if sample:
            cache_k, cache_v, cache_conv, state_pool = caches
            (q, uc, up), kv = inproj_call(xn, lw["w_in"], True, l, kv)
            osb = attn_sample_call(q, kv[0], kv[1], cache_k, cache_v, l)
            hist_c = _pad_hist(cache_conv[l], HIST_ROWS)
            hist_p = _pad_hist(state_pool[l], HIST_ROWS)
            start = cache_k.shape[3]
        else:
            (qt, kb, vt, uc, up), kv = inproj_call(xn, lw["w_in"], False, l, kv)
            osb = attn_prompt_call(qt, kb, vt)
            hist_c = jnp.zeros((1, HIST_ROWS, CONV_WIDTH), F32)
            hist_p = jnp.zeros((1, HIST_ROWS, POOL_WIDTH), F32)
            start = 0
        oc, op, c_state, p_state = mixers_call(uc, up, hist_c, hist_p, lw, not sample, start)
        h, xn2 = outproj_call(h, osb, oc, op, lw["w_out"], lw["norm_mlp_g"])
        if l + 1 < depth:
            h, xn = mlp_call(xn2, h, lw["w_up"], lw["w_down"], layers[l + 1]["norm_mix_g"], False)
        else:
            (y,) = mlp_call(xn2, h, lw["w_up"], lw["w_down"], final_g, True)
        cs.append(c_state)
        ps.append(p_state)
    return y, kv[0], kv[1], jnp.stack(cs), jnp.stack(ps)


def kernel(x_prompt, x_sample, cache_k, cache_v, cache_conv, state_pool, norm_mix_g, w_in,
           conv_dw_w, conv_dw_b, conv_norm_g, conv_norm_b, conv_pw_w, conv_pw_b, pool_w,
           pool_scale, w_out, norm_mlp_g, w_up, w_down, final_norm_g):
    depth = w_in.shape[0]
    bp, seq, d = x_prompt.shape
    bs, dseq, _ = x_sample.shape
    assert bp == 1 and dseq == DEC_SEQ and d == D_MODEL
    layers = []
    for l in range(depth):
        layers.append(dict(
            norm_mix_g=norm_mix_g[l],
            w_in=w_in[l].astype(BF16),
            dw_w=jnp.pad(conv_dw_w[l], ((0, HIST_ROWS - CONV_KERNEL), (0, 0))),
            dw_b=conv_dw_b[l].reshape(1, -1),
            n_g=conv_norm_g[l].reshape(1, -1),
            n_b=conv_norm_b[l].reshape(1, -1),
            pw_w=conv_pw_w[l].astype(BF16),
            pw_b=conv_pw_b[l].reshape(1, -1),
            pool_w=_block_diag(pool_w[l]).astype(BF16),
            pool_s=pool_scale[l].reshape(1, -1),
            w_out=w_out[l].astype(BF16),
            norm_mlp_g=norm_mlp_g[l],
            w_up=w_up[l].astype(BF16),
            w_down=w_down[l].astype(BF16),
        ))
    yp, kp, vp, cp, pp = _group_trunk(x_prompt.reshape(seq, d), layers, final_norm_g, False, None)
    ys, ks, vs, cs, ps = _group_trunk(x_sample.reshape(bs * dseq, d), layers, final_norm_g, True,
                                      (cache_k, cache_v, cache_conv, state_pool))
    return (yp.reshape(bp, seq, d), ys.reshape(bs, dseq, d),
            kp[:, None], vp[:, None], cp, pp, ks, vs, cs, ps)
```

```python
import functools
import math

import jax
import jax.numpy as jnp
from jax import lax
from jax.experimental import pallas as pl
from jax.experimental.pallas import tpu as pltpu

F32 = jnp.float32
BF16 = jnp.bfloat16

D_MODEL = 2048
SB_WIDTH = 1024
HEAD_DIM = 128
HEADS = SB_WIDTH // HEAD_DIM
CONV_WIDTH = 512
CONV_KERNEL = 31
CONV_GROUPS = 4
POOL_WIDTH = 512
POOL_WINDOWS = (2, 4, 8, 16)
POOL_GROUP = POOL_WIDTH // len(POOL_WINDOWS)
POOL_HIST = max(POOL_WINDOWS) - 1
D_FF = 4 * D_MODEL
IN_COLS = 3 * SB_WIDTH + 2 * CONV_WIDTH + POOL_WIDTH
EPS = 1e-6
DEC_SEQ = 64

LOG2E = 1.4426950408889634
Q_SCALE = HEAD_DIM ** -0.5 * LOG2E

LANE = 128
HIST_ROWS = 32
ROW_TILE = 512
FF_TILE = 1024
ATTN_TK = 256
ATTN_VT = 2 * ATTN_TK
ATTN_SUB = 128
CONV_CHUNK = 64
VMEM_LIMIT = 56 * 1024 * 1024


def _params(*sem):
    return pltpu.CompilerParams(dimension_semantics=sem, vmem_limit_bytes=VMEM_LIMIT)


def _dot(a, b):
    return jnp.dot(a, b, preferred_element_type=F32)


def _softplus2(z):
    e = jnp.exp2(-jnp.abs(z))
    return jnp.maximum(z, 0.0) + jnp.log(1.0 + e) * LOG2E


def _split_bf16(x):
    hi = x.astype(BF16)
    lo = (x - hi.astype(F32)).astype(BF16)
    return hi, lo


def _rmsnorm_rows(x, g):
    ms = jnp.mean(x * x, axis=-1, keepdims=True)
    return x * lax.rsqrt(ms + EPS) * g


def _rmsnorm_kernel(x_ref, g_ref, o_ref):
    o_ref[...] = _rmsnorm_rows(x_ref[...], g_ref[...]).astype(o_ref.dtype)


def rmsnorm_call(x, g):
    m, d = x.shape
    tm = min(ROW_TILE, m)
    return pl.pallas_call(
        _rmsnorm_kernel,
        grid=(m // tm,),
        in_specs=[pl.BlockSpec((tm, d), lambda i: (i, 0)),
                  pl.BlockSpec((1, d), lambda i: (0, 0))],
        out_specs=pl.BlockSpec((tm, d), lambda i: (i, 0)),
        out_shape=jax.ShapeDtypeStruct((m, d), BF16),
        compiler_params=_params("arbitrary"),
        name="rmsnorm",
    )(x, g.reshape(1, d))


def _inproj_kernel(xn_ref, w_ref, *rest, tm, sample, n_alias):
    outs = rest[n_alias:]
    x = xn_ref[...]
    if sample:
        q_ref, kf_ref, vf_ref, uc_ref, up_ref = outs
        nb = tm // DEC_SEQ
    else:
        qt_ref, kb_ref, vt_ref, kf_ref, vf_ref, uc_ref, up_ref = outs
    pair = 2 * HEAD_DIM
    for hp in range(HEADS // 2):
        c0 = hp * pair
        q2 = _dot(x, w_ref[:, c0:c0 + pair]) * Q_SCALE
        k2 = _dot(x, w_ref[:, SB_WIDTH + c0:SB_WIDTH + c0 + pair])
        v2 = _dot(x, w_ref[:, 2 * SB_WIDTH + c0:2 * SB_WIDTH + c0 + pair])
        for u in range(2):
            h = 2 * hp + u
            sl = slice(u * HEAD_DIM, (u + 1) * HEAD_DIM)
            if sample:
                q_ref[:, h] = q2[:, sl].astype(BF16).reshape(nb, DEC_SEQ, HEAD_DIM)
                kf_ref[:, h] = k2[:, sl].reshape(nb, DEC_SEQ, HEAD_DIM)
                vf_ref[:, h] = v2[:, sl].reshape(nb, DEC_SEQ, HEAD_DIM)
            else:
                qt_ref[h] = q2[:, sl].T.astype(BF16)
                kb_ref[h] = k2[:, sl].astype(BF16)
                kf_ref[h] = k2[:, sl]
                vf_ref[h] = v2[:, sl]
                vt = v2[:, sl].T.astype(BF16)
                for c in range(tm // ATTN_VT):
                    vt_ref[h, c] = vt[:, c * ATTN_VT:(c + 1) * ATTN_VT]
    c0 = 3 * SB_WIDTH
    uc_ref[...] = _dot(x, w_ref[:, c0:c0 + 2 * CONV_WIDTH])
    up_ref[...] = _dot(x, w_ref[:, c0 + 2 * CONV_WIDTH:])


def kv_stack_shape(m, sample, depth):
    if sample:
        return (depth, m // DEC_SEQ, HEADS, DEC_SEQ, HEAD_DIM)
    return (depth, HEADS, m, HEAD_DIM)


def inproj_call(xn, w_in, sample, layer, kv_stack):
    depth = kv_stack[0].shape[0]
    m, d = xn.shape
    tm = min(ROW_TILE, m)
    if sample:
        nbt = m // DEC_SEQ
        nb = tm // DEC_SEQ
        hshape = (nbt, HEADS, DEC_SEQ, HEAD_DIM)
        hblock = (nb, HEADS, DEC_SEQ, HEAD_DIM)
        kv_shape = jax.ShapeDtypeStruct((depth,) + hshape, F32)
        kv_spec = pl.BlockSpec((None,) + hblock, lambda i: (layer, i, 0, 0, 0))
        out_shape = [jax.ShapeDtypeStruct(hshape, BF16), kv_shape, kv_shape]
        out_specs = [pl.BlockSpec(hblock, lambda i: (i, 0, 0, 0)), kv_spec, kv_spec]
        kv_index = (1, 2)
    else:
        nkb = m // ATTN_VT
        kshape = (HEADS, m, HEAD_DIM)
        kblock = (HEADS, tm, HEAD_DIM)
        kv_shape = jax.ShapeDtypeStruct((depth,) + kshape, F32)
        kv_spec = pl.BlockSpec((None,) + kblock, lambda i: (layer, 0, i, 0))
        out_shape = [jax.ShapeDtypeStruct((HEADS, HEAD_DIM, m), BF16),
                     jax.ShapeDtypeStruct(kshape, BF16),
                     jax.ShapeDtypeStruct((HEADS, nkb, HEAD_DIM, ATTN_VT), BF16),
                     kv_shape, kv_shape]
        out_specs = [pl.BlockSpec((HEADS, HEAD_DIM, tm), lambda i: (0, 0, i)),
                     pl.BlockSpec(kblock, lambda i: (0, i, 0)),
                     pl.BlockSpec((HEADS, tm // ATTN_VT, HEAD_DIM, ATTN_VT),
                                  lambda i: (0, i, 0, 0)),
                     kv_spec, kv_spec]
        kv_index = (3, 4)
    out_shape += [jax.ShapeDtypeStruct((m, 2 * CONV_WIDTH), F32),
                  jax.ShapeDtypeStruct((m, POOL_WIDTH), F32)]
    out_specs += [pl.BlockSpec((tm, 2 * CONV_WIDTH), lambda i: (i, 0)),
                  pl.BlockSpec((tm, POOL_WIDTH), lambda i: (i, 0))]
    in_specs = [pl.BlockSpec((tm, d), lambda i: (i, 0)),
                pl.BlockSpec((None, d, IN_COLS), lambda i: (layer, 0, 0),
                             pipeline_mode=pl.Buffered(1))]
    args = [xn, w_in]
    aliases = {}
    for n, stack in enumerate(kv_stack):
        assert stack.shape == kv_shape.shape and stack.dtype == kv_shape.dtype
        aliases[len(args)] = kv_index[n]
        in_specs.append(pl.BlockSpec(memory_space=pl.ANY))
        args.append(stack)
    outs = pl.pallas_call(
        functools.partial(_inproj_kernel, tm=tm, sample=sample, n_alias=len(aliases)),
        grid=(m // tm,),
        in_specs=in_specs,
        out_specs=out_specs,
        out_shape=out_shape,
        input_output_aliases=aliases,
        compiler_params=_params("arbitrary"),
        name="inproj_sample" if sample else "inproj_prompt",
    )(*args)
    kv = tuple(outs[n] for n in kv_index)
    return [o for n, o in enumerate(outs) if n not in kv_index], kv


MASKED_LOGIT = -1e30


def _attn_stage1(k_ref, qt, p, i, masked, z_ref, hl_ref, *, tq, tk):
    for half in (1, 0):
        row0 = pl.multiple_of(p * (2 * tk) + half * tk, tk)
        z = _dot(k_ref[pl.ds(row0, tk), :], qt)
        sp = _softplus2(z)
        if masked:
            s_idx = row0 + lax.broadcasted_iota(jnp.int32, (tk, tq), 0)
            t_idx = i * tq + lax.broadcasted_iota(jnp.int32, (tk, tq), 1)
            valid = s_idx < t_idx
            sp = jnp.where(valid, sp, 0.0)
            z = jnp.where(valid, z, MASKED_LOGIT)
        hi, lo = _split_bf16(sp)
        z_ref[half * tk:(half + 1) * tk, :] = z
        ts = ATTN_SUB
        for sub in range(tk // ts):
            hl_ref[half * (tk // ts) + sub, 0:ts, :] = hi[sub * ts:(sub + 1) * ts]
            hl_ref[half * (tk // ts) + sub, ts:2 * ts, :] = lo[sub * ts:(sub + 1) * ts]


def _attn_stage2(tri2_ref, z_ref, hl_ref, half, carry, *, tk):
    ts = ATTN_SUB
    nsub = tk // ts
    ws = [None] * nsub
    for sub in range(nsub - 1, -1, -1):
        row0 = half * tk + sub * ts
        cum = _dot(tri2_ref[...], hl_ref[half * nsub + sub])
        ws[sub] = jnp.exp2(z_ref[row0:row0 + ts, :] - cum - carry).astype(BF16)
        carry = carry + cum[0:1, :]
    return jnp.concatenate(ws, axis=0), carry


ATTN_HEADS_PER_STEP = 2
ATTN_STAGE_BUFS = 2
PV_LAG = 2


def _attn_prompt_kernel(qt_ref, k_ref, vt_ref, tri2_ref, o_ref, *scratch, tq, tk):
    i = pl.program_id(1)
    nh = ATTN_HEADS_PER_STEP
    heads = range(nh)
    acc = scratch[0:nh]
    zb = [scratch[nh + b * nh:nh + (b + 1) * nh] for b in range(ATTN_STAGE_BUFS)]
    hl0 = nh * (1 + ATTN_STAGE_BUFS)
    hlb = [scratch[hl0 + b * nh:hl0 + (b + 1) * nh] for b in range(ATTN_STAGE_BUFS)]
    s1 = functools.partial(_attn_stage1, tq=tq, tk=tk)
    s2 = functools.partial(_attn_stage2, tk=tk)
    qts = [qt_ref[h] for h in heads]
    for h in heads:
        acc[h][...] = jnp.zeros_like(acc[h])

    def second(buf, p, carry):
        carry = list(carry)
        units = [(h, half) for h in heads for half in (1, 0)]
        ws = {}
        for n in range(len(units) + PV_LAG):
            if n < len(units):
                h, half = units[n]
                ws[units[n]], carry[h] = s2(tri2_ref, zb[buf][h], hlb[buf][h], half, carry[h])
            if n >= PV_LAG:
                h, half = units[n - PV_LAG]
                vt = vt_ref[h, p, :, half * tk:(half + 1) * tk]
                acc[h][...] += _dot(vt, ws.pop(units[n - PV_LAG]))
        return tuple(carry)

    def step(buf, p, carry):
        for h in heads:
            s1(k_ref.at[h], qts[h], jnp.maximum(p - 1, 0), i, False,
               zb[1 - buf][h], hlb[1 - buf][h])
        return second(buf, p, carry)

    for h in heads:
        s1(k_ref.at[h], qts[h], i, i, True, zb[0][h], hlb[0][h])

    def body(m, carry):
        p = i - 2 * m
        return step(1, p - 1, step(0, p, carry))

    carry = lax.fori_loop(0, (i + 1) // 2, body,
                          tuple(jnp.zeros((1, tq), F32) for _ in heads))

    @pl.when(i % 2 == 0)
    def _():
        second(0, 0, carry)

    d = acc[0].shape[0]
    for h in heads:
        o_ref[:, h * d:(h + 1) * d] = acc[h][...].T.astype(o_ref.dtype)


def _tri(tk):
    s = lax.broadcasted_iota(jnp.int32, (tk, tk), 0)
    j = lax.broadcasted_iota(jnp.int32, (tk, tk), 1)
    return (j >= s).astype(BF16)


def attn_prompt_call(qt, kb, vt):
    h, d, t = qt.shape
    tq = ATTN_VT
    tk = ATTN_TK
    ts = ATTN_SUB
    tri = _tri(ts)
    tri2 = jnp.concatenate([tri, tri], axis=1)
    nh = ATTN_HEADS_PER_STEP
    nstage = nh * ATTN_STAGE_BUFS
    return pl.pallas_call(
        functools.partial(_attn_prompt_kernel, tq=tq, tk=tk),
        grid=(h // nh, t // tq),
        in_specs=[pl.BlockSpec((nh, d, tq), lambda hh, i: (hh, 0, i)),
                  pl.BlockSpec((nh, t, d), lambda hh, i: (hh, 0, 0)),
                  pl.BlockSpec((nh, t // ATTN_VT, d, ATTN_VT), lambda hh, i: (hh, 0, 0, 0)),
                  pl.BlockSpec((ts, 2 * ts), lambda hh, i: (0, 0))],
        out_specs=pl.BlockSpec((tq, nh * d), lambda hh, i: (i, hh)),
        out_shape=jax.ShapeDtypeStruct((t, h * d), BF16),
        scratch_shapes=([pltpu.VMEM((d, tq), F32)] * nh
                        + [pltpu.VMEM((2 * tk, tq), F32)] * nstage
                        + [pltpu.VMEM((2 * tk // ts, 2 * ts, tq), BF16)] * nstage),
        compiler_params=_params("arbitrary", "arbitrary"),
        name="attn_prompt",
    )(qt, kb, vt, tri2)


SAMPLE_HEADS_PER_STEP = 4


def _attn_sample_kernel(q_ref, kn_ref, vn_ref, ck_ref, cv_ref, tri_ref, o_ref,
                        kb_ref, vb_ref, z_ref, hl_ref, cum_ref, w_ref, *, past, tk):
    nt = (((1,), (1,)), ((), ()))
    nh, ts, d = q_ref.shape
    rows = nh * ts
    nblk = past // tk + 1
    tot = nblk * tk
    for h in range(nh):
        for src, new, dst in ((ck_ref, kn_ref, kb_ref), (cv_ref, vn_ref, vb_ref)):
            dst[h, 0:past, :] = src[h].astype(BF16)
            dst[h, past:past + ts, :] = new[h].astype(BF16)
            dst[h, past + ts:tot, :] = jnp.zeros((tot - past - ts, d), BF16)
        z_ref[h * ts:(h + 1) * ts, :] = lax.dot_general(
            q_ref[h], kb_ref[h], nt, preferred_element_type=F32)

    for b in range(nblk):
        cols = slice(b * tk, (b + 1) * tk)
        z = z_ref[:, cols]
        sp = _softplus2(z)
        if b == nblk - 1:
            t_idx = lax.broadcasted_iota(jnp.int32, (rows, tk), 0) % ts
            s_idx = lax.broadcasted_iota(jnp.int32, (rows, tk), 1)
            valid = s_idx < t_idx
            sp = jnp.where(valid, sp, 0.0)
            z_ref[:, cols] = jnp.where(valid, z, MASKED_LOGIT)
        hi, lo = _split_bf16(sp)
        hl_ref[b * rows:(b + 1) * rows, 0:tk] = hi
        hl_ref[b * rows:(b + 1) * rows, tk:2 * tk] = lo

    cum_ref[...] = _dot(hl_ref[...], tri_ref[...])

    carry = jnp.zeros((rows, 1), F32)
    for b in range(nblk - 1, -1, -1):
        cols = slice(b * tk, (b + 1) * tk)
        cum = cum_ref[b * rows:(b + 1) * rows, :]
        w_ref[:, cols] = jnp.exp2(z_ref[:, cols] - cum - carry).astype(BF16)
        carry = carry + cum[:, 0:1]

    for h in range(nh):
        o_ref[:, h * d:(h + 1) * d] = _dot(w_ref[h * ts:(h + 1) * ts, :],
                                           vb_ref[h]).astype(o_ref.dtype)


def attn_sample_call(q, k_stack, v_stack, cache_k, cache_v, layer):
    b, h, ts, d = q.shape
    past = cache_k.shape[3]
    tk = ATTN_TK
    nh = SAMPLE_HEADS_PER_STEP
    rows = nh * ts
    tot = past + tk
    tri = _tri(tk).T
    tri2 = jnp.concatenate([tri, tri], axis=0)
    q_spec = pl.BlockSpec((None, nh, ts, d), lambda bb, hh: (bb, hh, 0, 0))
    new_spec = pl.BlockSpec((None, None, nh, ts, d), lambda bb, hh: (layer, bb, hh, 0, 0))
    cache_spec = pl.BlockSpec((None, None, nh, past, d), lambda bb, hh: (layer, bb, hh, 0, 0))
    return pl.pallas_call(
        functools.partial(_attn_sample_kernel, past=past, tk=tk),
        grid=(b, h // nh),
        in_specs=[q_spec, new_spec, new_spec, cache_spec, cache_spec,
                  pl.BlockSpec((2 * tk, tk), lambda bb, hh: (0, 0))],
        out_specs=pl.BlockSpec((ts, nh * d), lambda bb, hh: (bb, hh)),
        out_shape=jax.ShapeDtypeStruct((b * ts, h * d), BF16),
        scratch_shapes=[pltpu.VMEM((nh, tot, d), BF16), pltpu.VMEM((nh, tot, d), BF16),
                        pltpu.VMEM((rows, tot), F32),
                        pltpu.VMEM((tot // tk * rows, 2 * tk), BF16),
                        pltpu.VMEM((tot // tk * rows, tk), F32),
                        pltpu.VMEM((rows, tot), BF16)],
        compiler_params=_params("arbitrary", "arbitrary"),
        name="attn_sample",
    )(q, k_stack, v_stack, cache_k, cache_v, tri2)


def _mixers_kernel(uc_ref, up_ref, hc_ref, hp_ref, dww_ref, dwb_ref, ng_ref, nb_ref,
                   pww_ref, pwb_ref, plw_ref, pls_ref,
                   oc_ref, op_ref, cs_ref, ps_ref,
                   ext_ref, sh_ref, yn_ref, pext_ref, pooled_ref,
                   *, tm, carry, start_pos):
    i = pl.program_id(0)
    hdr = HIST_ROWS

    def load_hist():
        ext_ref[0:hdr, :] = hc_ref[...]
        pext_ref[0:hdr, :] = hp_ref[...]

    if carry:
        pl.when(i == 0)(load_hist)
    else:
        load_hist()

    ext_ref[hdr:hdr + tm, :] = uc_ref[:, 0:CONV_WIDTH] * jax.nn.sigmoid(uc_ref[:, CONV_WIDTH:])
    pext_ref[hdr:hdr + tm, :] = up_ref[...]
    cs_ref[...] = ext_ref[pl.ds(hdr + tm - (CONV_KERNEL - 1), CONV_KERNEL - 1), :]
    ps_ref[...] = pext_ref[pl.ds(hdr + tm - POOL_HIST, POOL_HIST), :]

    sh_rows = sh_ref.shape[1]
    for r in range(1, 8):
        sh_ref[r - 1] = ext_ref[pl.ds(r, sh_rows), :]

    rc = min(CONV_CHUNK, tm)
    first = hdr - (CONV_KERNEL - 1)

    def conv_chunk(c, _):
        base = pl.multiple_of(c * rc, rc)
        acc = jnp.zeros((rc, CONV_WIDTH), F32)
        for r in range(8):
            taps = [(j, ((first + j) // 8) * 8) for j in range(CONV_KERNEL)
                    if (first + j) % 8 == r]
            lo = min(off for _, off in taps)
            hi = max(off for _, off in taps)
            src = ext_ref if r == 0 else sh_ref.at[r - 1]
            slab = src[pl.ds(base + lo, rc + hi - lo), :]
            for j, off in taps:
                acc = acc + slab[off - lo:off - lo + rc] * dww_ref[j:j + 1, :]
        y = acc + dwb_ref[...]
        gw = CONV_WIDTH // CONV_GROUPS
        for g in range(CONV_GROUPS):
            sl = slice(g * gw, (g + 1) * gw)
            yg = y[:, sl]
            mu = jnp.mean(yg, axis=-1, keepdims=True)
            dv = yg - mu
            var = jnp.mean(dv * dv, axis=-1, keepdims=True)
            yn = dv * lax.rsqrt(var + EPS) * ng_ref[:, sl] + nb_ref[:, sl]
            yn = yn * jax.nn.sigmoid(yn)
            yn_ref[pl.ds(base, rc), sl] = yn.astype(BF16)
        return 0

    lax.fori_loop(0, tm // rc, conv_chunk, 0)
    oc_ref[...] = (_dot(yn_ref[...], pww_ref[...]) + pwb_ref[...]).astype(oc_ref.dtype)

    pc = min(LANE, tm)
    for c in range(tm // pc):
        b0 = hdr + c * pc
        row = lax.broadcasted_iota(jnp.int32, (pc, POOL_GROUP), 0) + c * pc
        pos = row + (start_pos + (i * tm if carry else 0))
        for g, w in enumerate(POOL_WINDOWS):
            sl = slice(g * POOL_GROUP, (g + 1) * POOL_GROUP)
            u = pext_ref[b0:b0 + pc, sl]
            s = u
            for dlt in range(1, w):
                s = s + pext_ref[pl.ds(b0 - dlt, pc), sl]
            cnt = jnp.minimum(w, pos + 1).astype(F32)
            pooled_ref[c * pc:(c + 1) * pc, sl] = (s / cnt - u).astype(BF16)
    op_ref[...] = (_dot(pooled_ref[...], plw_ref[...]) * pls_ref[...]).astype(op_ref.dtype)

    if carry:
        ext_ref[0:hdr, :] = ext_ref[tm:tm + hdr, :]
        pext_ref[0:hdr, :] = pext_ref[tm:tm + hdr, :]


def mixers_call(uc, up, hist_c, hist_p, lw, carry, start_pos):
    m = uc.shape[0]
    nseq = hist_c.shape[0]
    tm = min(ROW_TILE, m) if carry else m // nseq
    row = lambda i: (i, 0)
    const = lambda i: (0, 0)
    seq = (lambda i: (0, 0, 0)) if carry else (lambda i: (i, 0, 0))
    vec = pl.BlockSpec((1, CONV_WIDTH), const)
    sq = pl.BlockSpec((CONV_WIDTH, CONV_WIDTH), const)
    return pl.pallas_call(
        functools.partial(_mixers_kernel, tm=tm, carry=carry, start_pos=start_pos),
        grid=(m // tm,),
        in_specs=[pl.BlockSpec((tm, 2 * CONV_WIDTH), row),
                  pl.BlockSpec((tm, POOL_WIDTH), row),
                  pl.BlockSpec((None, HIST_ROWS, CONV_WIDTH), seq),
                  pl.BlockSpec((None, HIST_ROWS, POOL_WIDTH), seq),
                  pl.BlockSpec((HIST_ROWS, CONV_WIDTH), const),
                  vec, vec, vec, sq, vec, sq, vec],
        out_specs=[pl.BlockSpec((tm, CONV_WIDTH), row),
                   pl.BlockSpec((tm, POOL_WIDTH), row),
                   pl.BlockSpec((None, CONV_KERNEL - 1, CONV_WIDTH), seq),
                   pl.BlockSpec((None, POOL_HIST, POOL_WIDTH), seq)],
        out_shape=[jax.ShapeDtypeStruct((m, CONV_WIDTH), BF16),
                   jax.ShapeDtypeStruct((m, POOL_WIDTH), BF16),
                   jax.ShapeDtypeStruct((nseq, CONV_KERNEL - 1, CONV_WIDTH), F32),
                   jax.ShapeDtypeStruct((nseq, POOL_HIST, POOL_WIDTH), F32)],
        scratch_shapes=[pltpu.VMEM((HIST_ROWS + tm + 8, CONV_WIDTH), F32),
                        pltpu.VMEM((7, tm + HIST_ROWS - 8, CONV_WIDTH), F32),
                        pltpu.VMEM((tm, CONV_WIDTH), BF16),
                        pltpu.VMEM((HIST_ROWS + tm, POOL_WIDTH), F32),
                        pltpu.VMEM((tm, POOL_WIDTH), BF16)],
        compiler_params=_params("arbitrary"),
        name="mixers_prompt" if carry else "mixers_sample",
    )(uc, up, hist_c, hist_p, lw["dw_w"], lw["dw_b"], lw["n_g"], lw["n_b"],
      lw["pw_w"], lw["pw_b"], lw["pool_w"], lw["pool_s"])


def _outproj_kernel(h_ref, osb_ref, oc_ref, op_ref, w_ref, g_ref, hn_ref, xn_ref):
    c1 = SB_WIDTH
    c2 = SB_WIDTH + CONV_WIDTH
    mix = (_dot(osb_ref[...], w_ref[0:c1, :]) + _dot(oc_ref[...], w_ref[c1:c2, :])
           + _dot(op_ref[...], w_ref[c2:, :]))
    h = h_ref[...] + mix
    hn_ref[...] = h
    xn_ref[...] = _rmsnorm_rows(h, g_ref[...]).astype(xn_ref.dtype)


def outproj_call(h, osb, oc, op, w_out, g, layer):
    m, d = h.shape
    tm = min(ROW_TILE, m)
    row = lambda i: (i, 0)
    return pl.pallas_call(
        _outproj_kernel,
        grid=(m // tm,),
        in_specs=[pl.BlockSpec((tm, d), row),
                  pl.BlockSpec((tm, SB_WIDTH), row),
                  pl.BlockSpec((tm, CONV_WIDTH), row),
                  pl.BlockSpec((tm, POOL_WIDTH), row),
                  pl.BlockSpec((None, d, d), lambda i: (layer, 0, 0),
                               pipeline_mode=pl.Buffered(1)),
                  pl.BlockSpec((1, d), lambda i: (0, 0))],
        out_specs=[pl.BlockSpec((tm, d), row), pl.BlockSpec((tm, d), row)],
        out_shape=[jax.ShapeDtypeStruct((m, d), F32), jax.ShapeDtypeStruct((m, d), BF16)],
        compiler_params=_params("arbitrary"),
        name="outproj",
    )(h, osb, oc, op, w_out, g.reshape(1, d))


def _mlp_kernel(xn_ref, h_ref, wu_ref, wd_ref, g_ref, *rest, final):
    if final:
        y_ref, acc_ref = rest
    else:
        hn_ref, y_ref, acc_ref = rest
    f = pl.program_id(1)

    @pl.when(f == 0)
    def _():
        acc_ref[...] = jnp.zeros_like(acc_ref)

    a = jnp.maximum(_dot(xn_ref[...], wu_ref[...]), 0.0)
    acc_ref[...] += _dot((a * a).astype(BF16), wd_ref[...])

    @pl.when(f == pl.num_programs(1) - 1)
    def _():
        h = h_ref[...] + acc_ref[...]
        if not final:
            hn_ref[...] = h
        y_ref[...] = _rmsnorm_rows(h, g_ref[...]).astype(y_ref.dtype)


def mlp_call(xn, h, w_up, w_down, g, final, layer):
    m, d = h.shape
    ff = w_up.shape[2]
    tm = min(ROW_TILE, m)
    tf = FF_TILE
    row = lambda i, f: (i, 0)
    y_dtype = F32 if final else BF16
    out_specs = [pl.BlockSpec((tm, d), row)]
    out_shape = [jax.ShapeDtypeStruct((m, d), y_dtype)]
    if not final:
        out_specs = [pl.BlockSpec((tm, d), row)] + out_specs
        out_shape = [jax.ShapeDtypeStruct((m, d), F32)] + out_shape
    return pl.pallas_call(
        functools.partial(_mlp_kernel, final=final),
        grid=(m // tm, ff // tf),
        in_specs=[pl.BlockSpec((tm, d), row),
                  pl.BlockSpec((tm, d), row),
                  pl.BlockSpec((None, d, tf), lambda i, f: (layer, 0, f)),
                  pl.BlockSpec((None, tf, d), lambda i, f: (layer, f, 0)),
                  pl.BlockSpec((1, d), lambda i, f: (0, 0))],
        out_specs=out_specs,
        out_shape=out_shape,
        scratch_shapes=[pltpu.VMEM((tm, d), F32)],
        compiler_params=_params("arbitrary", "arbitrary"),
        name="mlp_final" if final else "mlp",
    )(xn, h, w_up, w_down, g.reshape(1, d))


def _pad_hist(hist, rows):
    return jnp.pad(hist, ((0, 0), (rows - hist.shape[1], 0), (0, 0)))


def _block_diag(w):
    g, a, b = w.shape
    eye = jnp.eye(g, dtype=w.dtype)
    return (eye[:, None, :, None] * w[:, :, None, :]).reshape(g * a, g * b)


def _group_trunk(x, layers, big, final_g, sample, caches):
    depth = len(layers)
    h = x
    xn = rmsnorm_call(x, layers[0]["norm_mix_g"])
    cs, ps = [], []
    y = None
    shape = kv_stack_shape(x.shape[0], sample, depth)
    kv = (jnp.zeros(shape, F32), jnp.zeros(shape, F32))
    for l, lw in enumerate(layers):
        if sample:
            cache_k, cache_v, cache_conv, state_pool = caches
            (q, uc, up), kv = inproj_call(xn, big["w_in"], True, l, kv)
            osb = attn_sample_call(q, kv[0], kv[1], cache_k, cache_v, l)
            hist_c = _pad_hist(cache_conv[l], HIST_ROWS)
            hist_p = _pad_hist(state_pool[l], HIST_ROWS)
            start = cache_k.shape[3]
        else:
            (qt, kb, vt, uc, up), kv = inproj_call(xn, big["w_in"], False, l, kv)
            osb = attn_prompt_call(qt, kb, vt)
            hist_c = jnp.zeros((1, HIST_ROWS, CONV_WIDTH), F32)
            hist_p = jnp.zeros((1, HIST_ROWS, POOL_WIDTH), F32)
            start = 0
        oc, op, c_state, p_state = mixers_call(uc, up, hist_c, hist_p, lw, not sample, start)
        h, xn2 = outproj_call(h, osb, oc, op, big["w_out"], lw["norm_mlp_g"], l)
        if l + 1 < depth:
            h, xn = mlp_call(xn2, h, big["w_up"], big["w_down"], layers[l + 1]["norm_mix_g"],
                             False, l)
        else:
            (y,) = mlp_call(xn2, h, big["w_up"], big["w_down"], final_g, True, l)
        cs.append(c_state)
        ps.append(p_state)
    return y, kv[0], kv[1], jnp.stack(cs), jnp.stack(ps)


def kernel(x_prompt, x_sample, cache_k, cache_v, cache_conv, state_pool, norm_mix_g, w_in,
           conv_dw_w, conv_dw_b, conv_norm_g, conv_norm_b, conv_pw_w, conv_pw_b, pool_w,
           pool_scale, w_out, norm_mlp_g, w_up, w_down, final_norm_g):
    depth = w_in.shape[0]
    bp, seq, d = x_prompt.shape
    bs, dseq, _ = x_sample.shape
    assert bp == 1 and dseq == DEC_SEQ and d == D_MODEL
    layers = []
    for l in range(depth):
        layers.append(dict(
            norm_mix_g=norm_mix_g[l],
            dw_w=jnp.pad(conv_dw_w[l], ((0, HIST_ROWS - CONV_KERNEL), (0, 0))),
            dw_b=conv_dw_b[l].reshape(1, -1),
            n_g=conv_norm_g[l].reshape(1, -1),
            n_b=conv_norm_b[l].reshape(1, -1),
            pw_w=conv_pw_w[l].astype(BF16),
            pw_b=conv_pw_b[l].reshape(1, -1),
            pool_w=_block_diag(pool_w[l]).astype(BF16),
            pool_s=pool_scale[l].reshape(1, -1),
            norm_mlp_g=norm_mlp_g[l],
        ))
    big = dict(w_in=w_in.astype(BF16), w_out=w_out.astype(BF16),
               w_up=w_up.astype(BF16), w_down=w_down.astype(BF16))
    yp, kp, vp, cp, pp = _group_trunk(x_prompt.reshape(seq, d), layers, big, final_norm_g,
                                      False, None)
    ys, ks, vs, cs, ps = _group_trunk(x_sample.reshape(bs * dseq, d), layers, big, final_norm_g, True,
                                      (cache_k, cache_v, cache_conv, state_pool))
    return (yp.reshape(bp, seq, d), ys.reshape(bs, dseq, d),
            kp[:, None], vp[:, None], cp, pp, ks, vs, cs, ps)
```

```python
import functools
import math

import jax
import jax.numpy as jnp
from jax import lax
from jax.experimental import pallas as pl
from jax.experimental.pallas import tpu as pltpu

F32 = jnp.float32
BF16 = jnp.bfloat16

D_MODEL = 2048
SB_WIDTH = 1024
HEAD_DIM = 128
HEADS = SB_WIDTH // HEAD_DIM
CONV_WIDTH = 512
CONV_KERNEL = 31
CONV_GROUPS = 4
POOL_WIDTH = 512
POOL_WINDOWS = (2, 4, 8, 16)
POOL_GROUP = POOL_WIDTH // len(POOL_WINDOWS)
POOL_HIST = max(POOL_WINDOWS) - 1
D_FF = 4 * D_MODEL
IN_COLS = 3 * SB_WIDTH + 2 * CONV_WIDTH + POOL_WIDTH
EPS = 1e-6
DEC_SEQ = 64

LOG2E = 1.4426950408889634
Q_SCALE = HEAD_DIM ** -0.5 * LOG2E

LANE = 128
HIST_ROWS = 32
ROW_TILE = 512
FF_TILE = 1024
ATTN_TK = 256
ATTN_VT = 2 * ATTN_TK
CONV_CHUNK = 64
VMEM_LIMIT = 56 * 1024 * 1024


def _params(*sem):
    return pltpu.CompilerParams(dimension_semantics=sem, vmem_limit_bytes=VMEM_LIMIT)


def _dot(a, b):
    return jnp.dot(a, b, preferred_element_type=F32)


def _softplus2(z):
    e = jnp.exp2(-jnp.abs(z))
    return jnp.maximum(z, 0.0) + jnp.log(1.0 + e) * LOG2E


def _rmsnorm_rows(x, g):
    ms = jnp.mean(x * x, axis=-1, keepdims=True)
    return x * lax.rsqrt(ms + EPS) * g


def _rmsnorm_kernel(x_ref, g_ref, o_ref):
    o_ref[...] = _rmsnorm_rows(x_ref[...], g_ref[...]).astype(o_ref.dtype)


def rmsnorm_call(x, g):
    m, d = x.shape
    tm = min(ROW_TILE, m)
    return pl.pallas_call(
        _rmsnorm_kernel,
        grid=(m // tm,),
        in_specs=[pl.BlockSpec((tm, d), lambda i: (i, 0)),
                  pl.BlockSpec((1, d), lambda i: (0, 0))],
        out_specs=pl.BlockSpec((tm, d), lambda i: (i, 0)),
        out_shape=jax.ShapeDtypeStruct((m, d), BF16),
        compiler_params=_params("arbitrary"),
        name="rmsnorm",
    )(x, g.reshape(1, d))


def _inproj_kernel(xn_ref, w_ref, *rest, tm, sample, n_alias):
    outs = rest[n_alias:]
    x = xn_ref[...]
    if sample:
        q_ref, kf_ref, vf_ref, uc_ref, up_ref = outs
        nb = tm // DEC_SEQ
    else:
        qt_ref, kb_ref, vt_ref, kf_ref, vf_ref, uc_ref, up_ref = outs
    pair = 2 * HEAD_DIM
    for hp in range(HEADS // 2):
        c0 = hp * pair
        q2 = _dot(x, w_ref[:, c0:c0 + pair]) * Q_SCALE
        k2 = _dot(x, w_ref[:, SB_WIDTH + c0:SB_WIDTH + c0 + pair])
        v2 = _dot(x, w_ref[:, 2 * SB_WIDTH + c0:2 * SB_WIDTH + c0 + pair])
        for u in range(2):
            h = 2 * hp + u
            sl = slice(u * HEAD_DIM, (u + 1) * HEAD_DIM)
            if sample:
                q_ref[:, h] = q2[:, sl].astype(BF16).reshape(nb, DEC_SEQ, HEAD_DIM)
                kf_ref[:, h] = k2[:, sl].reshape(nb, DEC_SEQ, HEAD_DIM)
                vf_ref[:, h] = v2[:, sl].reshape(nb, DEC_SEQ, HEAD_DIM)
            else:
                qt_ref[h] = q2[:, sl].T.astype(BF16)
                kb_ref[h] = k2[:, sl].astype(BF16)
                kf_ref[h] = k2[:, sl]
                vf_ref[h] = v2[:, sl]
                vt = v2[:, sl].T.astype(BF16)
                for c in range(tm // ATTN_VT):
                    vt_ref[h, c] = vt[:, c * ATTN_VT:(c + 1) * ATTN_VT]
    c0 = 3 * SB_WIDTH
    uc_ref[...] = _dot(x, w_ref[:, c0:c0 + 2 * CONV_WIDTH])
    up_ref[...] = _dot(x, w_ref[:, c0 + 2 * CONV_WIDTH:])


def kv_stack_shape(m, sample, depth):
    if sample:
        return (depth, m // DEC_SEQ, HEADS, DEC_SEQ, HEAD_DIM)
    return (depth, HEADS, m, HEAD_DIM)


def inproj_call(xn, w_in, sample, layer, kv_stack):
    depth = kv_stack[0].shape[0]
    m, d = xn.shape
    tm = min(ROW_TILE, m)
    if sample:
        nbt = m // DEC_SEQ
        nb = tm // DEC_SEQ
        hshape = (nbt, HEADS, DEC_SEQ, HEAD_DIM)
        hblock = (nb, HEADS, DEC_SEQ, HEAD_DIM)
        kv_shape = jax.ShapeDtypeStruct((depth,) + hshape, F32)
        kv_spec = pl.BlockSpec((None,) + hblock, lambda i: (layer, i, 0, 0, 0))
        out_shape = [jax.ShapeDtypeStruct(hshape, BF16), kv_shape, kv_shape]
        out_specs = [pl.BlockSpec(hblock, lambda i: (i, 0, 0, 0)), kv_spec, kv_spec]
        kv_index = (1, 2)
    else:
        nkb = m // ATTN_VT
        kshape = (HEADS, m, HEAD_DIM)
        kblock = (HEADS, tm, HEAD_DIM)
        kv_shape = jax.ShapeDtypeStruct((depth,) + kshape, F32)
        kv_spec = pl.BlockSpec((None,) + kblock, lambda i: (layer, 0, i, 0))
        out_shape = [jax.ShapeDtypeStruct((HEADS, HEAD_DIM, m), BF16),
                     jax.ShapeDtypeStruct(kshape, BF16),
                     jax.ShapeDtypeStruct((HEADS, nkb, HEAD_DIM, ATTN_VT), BF16),
                     kv_shape, kv_shape]
        out_specs = [pl.BlockSpec((HEADS, HEAD_DIM, tm), lambda i: (0, 0, i)),
                     pl.BlockSpec(kblock, lambda i: (0, i, 0)),
                     pl.BlockSpec((HEADS, tm // ATTN_VT, HEAD_DIM, ATTN_VT),
                                  lambda i: (0, i, 0, 0)),
                     kv_spec, kv_spec]
        kv_index = (3, 4)
    out_shape += [jax.ShapeDtypeStruct((m, 2 * CONV_WIDTH), F32),
                  jax.ShapeDtypeStruct((m, POOL_WIDTH), F32)]
    out_specs += [pl.BlockSpec((tm, 2 * CONV_WIDTH), lambda i: (i, 0)),
                  pl.BlockSpec((tm, POOL_WIDTH), lambda i: (i, 0))]
    in_specs = [pl.BlockSpec((tm, d), lambda i: (i, 0)),
                pl.BlockSpec((None, d, IN_COLS), lambda i: (layer, 0, 0),
                             pipeline_mode=pl.Buffered(1))]
    args = [xn, w_in]
    aliases = {}
    for n, stack in enumerate(kv_stack):
        assert stack.shape == kv_shape.shape and stack.dtype == kv_shape.dtype
        aliases[len(args)] = kv_index[n]
        in_specs.append(pl.BlockSpec(memory_space=pl.ANY))
        args.append(stack)
    outs = pl.pallas_call(
        functools.partial(_inproj_kernel, tm=tm, sample=sample, n_alias=len(aliases)),
        grid=(m // tm,),
        in_specs=in_specs,
        out_specs=out_specs,
        out_shape=out_shape,
        input_output_aliases=aliases,
        compiler_params=_params("arbitrary"),
        name="inproj_sample" if sample else "inproj_prompt",
    )(*args)
    kv = tuple(outs[n] for n in kv_index)
    return [o for n, o in enumerate(outs) if n not in kv_index], kv


MASKED_LOGIT = -1e30


def _attn_stage1(k_ref, qt, p, i, masked, z_ref, sp_ref, *, tq, tk):
    for half in (1, 0):
        row0 = pl.multiple_of(p * (2 * tk) + half * tk, tk)
        z = _dot(k_ref[pl.ds(row0, tk), :], qt)
        sp = _softplus2(z)
        if masked:
            s_idx = row0 + lax.broadcasted_iota(jnp.int32, (tk, tq), 0)
            t_idx = i * tq + lax.broadcasted_iota(jnp.int32, (tk, tq), 1)
            valid = s_idx < t_idx
            sp = jnp.where(valid, sp, 0.0)
            z = jnp.where(valid, z, MASKED_LOGIT)
        z_ref[half * tk:(half + 1) * tk, :] = z
        sp_ref[half] = sp.astype(BF16)


def _attn_stage2(tri_ref, z_ref, sp_ref, half, carry, *, tk):
    cum = _dot(tri_ref[...], sp_ref[half])
    w = jnp.exp2(z_ref[half * tk:(half + 1) * tk, :] - cum - carry)
    return w.astype(BF16), carry + cum[0:1, :]


ATTN_HEADS_PER_STEP = 4
ATTN_STAGE_BUFS = 2
PV_LAG = 2


def _attn_prompt_kernel(qt_ref, k_ref, vt_ref, tri_ref, o_ref, *scratch, tq, tk):
    i = pl.program_id(1)
    nh = ATTN_HEADS_PER_STEP
    heads = range(nh)
    acc = scratch[0:nh]
    zb = [scratch[nh + b * nh:nh + (b + 1) * nh] for b in range(ATTN_STAGE_BUFS)]
    sp0 = nh * (1 + ATTN_STAGE_BUFS)
    spb = [scratch[sp0 + b * nh:sp0 + (b + 1) * nh] for b in range(ATTN_STAGE_BUFS)]
    s1 = functools.partial(_attn_stage1, tq=tq, tk=tk)
    s2 = functools.partial(_attn_stage2, tk=tk)
    qts = [qt_ref[h] for h in heads]
    for h in heads:
        acc[h][...] = jnp.zeros_like(acc[h])

    def second(buf, p, carry):
        carry = list(carry)
        units = [(h, half) for h in heads for half in (1, 0)]
        ws = {}
        for n in range(len(units) + PV_LAG):
            if n < len(units):
                h, half = units[n]
                ws[units[n]], carry[h] = s2(tri_ref, zb[buf][h], spb[buf][h], half, carry[h])
            if n >= PV_LAG:
                h, half = units[n - PV_LAG]
                vt = vt_ref[h, p, :, half * tk:(half + 1) * tk]
                acc[h][...] += _dot(vt, ws.pop(units[n - PV_LAG]))
        return tuple(carry)

    def step(buf, p, carry):
        for h in heads:
            s1(k_ref.at[h], qts[h], jnp.maximum(p - 1, 0), i, False,
               zb[1 - buf][h], spb[1 - buf][h])
        return second(buf, p, carry)

    for h in heads:
        s1(k_ref.at[h], qts[h], i, i, True, zb[0][h], spb[0][h])

    def body(m, carry):
        p = i - 2 * m
        return step(1, p - 1, step(0, p, carry))

    carry = lax.fori_loop(0, (i + 1) // 2, body,
                          tuple(jnp.zeros((1, tq), F32) for _ in heads))

    @pl.when(i % 2 == 0)
    def _():
        second(0, 0, carry)

    d = acc[0].shape[0]
    for h in heads:
        o_ref[:, h * d:(h + 1) * d] = acc[h][...].T.astype(o_ref.dtype)


def _tri(tk):
    s = lax.broadcasted_iota(jnp.int32, (tk, tk), 0)
    j = lax.broadcasted_iota(jnp.int32, (tk, tk), 1)
    return (j >= s).astype(BF16)


def attn_prompt_call(qt, kb, vt):
    h, d, t = qt.shape
    tq = ATTN_VT
    tk = ATTN_TK
    tri = _tri(tk)
    nh = ATTN_HEADS_PER_STEP
    nstage = nh * ATTN_STAGE_BUFS
    return pl.pallas_call(
        functools.partial(_attn_prompt_kernel, tq=tq, tk=tk),
        grid=(h // nh, t // tq),
        in_specs=[pl.BlockSpec((nh, d, tq), lambda hh, i: (hh, 0, i)),
                  pl.BlockSpec((nh, t, d), lambda hh, i: (hh, 0, 0)),
                  pl.BlockSpec((nh, t // ATTN_VT, d, ATTN_VT), lambda hh, i: (hh, 0, 0, 0)),
                  pl.BlockSpec((tk, tk), lambda hh, i: (0, 0))],
        out_specs=pl.BlockSpec((tq, nh * d), lambda hh, i: (i, hh)),
        out_shape=jax.ShapeDtypeStruct((t, h * d), BF16),
        scratch_shapes=([pltpu.VMEM((d, tq), F32)] * nh
                        + [pltpu.VMEM((2 * tk, tq), F32)] * nstage
                        + [pltpu.VMEM((2, tk, tq), BF16)] * nstage),
        compiler_params=_params("arbitrary", "arbitrary"),
        name="attn_prompt",
    )(qt, kb, vt, tri)


SAMPLE_HEADS_PER_STEP = 4


def _attn_sample_kernel(q_ref, kn_ref, vn_ref, ck_ref, cv_ref, tri_ref, o_ref,
                        kb_ref, vb_ref, z_ref, sp_ref, cum_ref, w_ref, *, past, tk):
    nt = (((1,), (1,)), ((), ()))
    nh, ts, d = q_ref.shape
    rows = nh * ts
    nblk = past // tk + 1
    tot = nblk * tk
    for h in range(nh):
        for src, new, dst in ((ck_ref, kn_ref, kb_ref), (cv_ref, vn_ref, vb_ref)):
            dst[h, 0:past, :] = src[h].astype(BF16)
            dst[h, past:past + ts, :] = new[h].astype(BF16)
            dst[h, past + ts:tot, :] = jnp.zeros((tot - past - ts, d), BF16)
        z_ref[h * ts:(h + 1) * ts, :] = lax.dot_general(
            q_ref[h], kb_ref[h], nt, preferred_element_type=F32)

    for b in range(nblk):
        cols = slice(b * tk, (b + 1) * tk)
        z = z_ref[:, cols]
        sp = _softplus2(z)
        if b == nblk - 1:
            t_idx = lax.broadcasted_iota(jnp.int32, (rows, tk), 0) % ts
            s_idx = lax.broadcasted_iota(jnp.int32, (rows, tk), 1)
            valid = s_idx < t_idx
            sp = jnp.where(valid, sp, 0.0)
            z_ref[:, cols] = jnp.where(valid, z, MASKED_LOGIT)
        sp_ref[b * rows:(b + 1) * rows, :] = sp.astype(BF16)

    cum_ref[...] = _dot(sp_ref[...], tri_ref[...])

    carry = jnp.zeros((rows, 1), F32)
    for b in range(nblk - 1, -1, -1):
        cols = slice(b * tk, (b + 1) * tk)
        cum = cum_ref[b * rows:(b + 1) * rows, :]
        w_ref[:, cols] = jnp.exp2(z_ref[:, cols] - cum - carry).astype(BF16)
        carry = carry + cum[:, 0:1]

    for h in range(nh):
        o_ref[:, h * d:(h + 1) * d] = _dot(w_ref[h * ts:(h + 1) * ts, :],
                                           vb_ref[h]).astype(o_ref.dtype)


def attn_sample_call(q, k_stack, v_stack, cache_k, cache_v, layer):
    b, h, ts, d = q.shape
    past = cache_k.shape[3]
    tk = ATTN_TK
    nh = SAMPLE_HEADS_PER_STEP
    rows = nh * ts
    tot = past + tk
    tri = _tri(tk).T
    q_spec = pl.BlockSpec((None, nh, ts, d), lambda bb, hh: (bb, hh, 0, 0))
    new_spec = pl.BlockSpec((None, None, nh, ts, d), lambda bb, hh: (layer, bb, hh, 0, 0))
    cache_spec = pl.BlockSpec((None, None, nh, past, d), lambda bb, hh: (layer, bb, hh, 0, 0))
    return pl.pallas_call(
        functools.partial(_attn_sample_kernel, past=past, tk=tk),
        grid=(b, h // nh),
        in_specs=[q_spec, new_spec, new_spec, cache_spec, cache_spec,
                  pl.BlockSpec((tk, tk), lambda bb, hh: (0, 0))],
        out_specs=pl.BlockSpec((ts, nh * d), lambda bb, hh: (bb, hh)),
        out_shape=jax.ShapeDtypeStruct((b * ts, h * d), BF16),
        scratch_shapes=[pltpu.VMEM((nh, tot, d), BF16), pltpu.VMEM((nh, tot, d), BF16),
                        pltpu.VMEM((rows, tot), F32),
                        pltpu.VMEM((tot // tk * rows, tk), BF16),
                        pltpu.VMEM((tot // tk * rows, tk), F32),
                        pltpu.VMEM((rows, tot), BF16)],
        compiler_params=_params("arbitrary", "arbitrary"),
        name="attn_sample",
    )(q, k_stack, v_stack, cache_k, cache_v, tri)


def _mixers_kernel(uc_ref, up_ref, hc_ref, hp_ref, dww_ref, dwb_ref, ng_ref, nb_ref,
                   pww_ref, pwb_ref, plw_ref, pls_ref,
                   oc_ref, op_ref, cs_ref, ps_ref,
                   ext_ref, sh_ref, yn_ref, pext_ref, pooled_ref,
                   *, tm, carry, start_pos):
    i = pl.program_id(0)
    hdr = HIST_ROWS

    def load_hist():
        ext_ref[0:hdr, :] = hc_ref[...]
        pext_ref[0:hdr, :] = hp_ref[...]

    if carry:
        pl.when(i == 0)(load_hist)
    else:
        load_hist()

    ext_ref[hdr:hdr + tm, :] = uc_ref[:, 0:CONV_WIDTH] * jax.nn.sigmoid(uc_ref[:, CONV_WIDTH:])
    pext_ref[hdr:hdr + tm, :] = up_ref[...]
    cs_ref[...] = ext_ref[pl.ds(hdr + tm - (CONV_KERNEL - 1), CONV_KERNEL - 1), :]
    ps_ref[...] = pext_ref[pl.ds(hdr + tm - POOL_HIST, POOL_HIST), :]

    sh_rows = sh_ref.shape[1]
    for r in range(1, 8):
        sh_ref[r - 1] = ext_ref[pl.ds(r, sh_rows), :]

    rc = min(CONV_CHUNK, tm)
    first = hdr - (CONV_KERNEL - 1)

    def conv_chunk(c, _):
        base = pl.multiple_of(c * rc, rc)
        acc = jnp.zeros((rc, CONV_WIDTH), F32)
        for r in range(8):
            taps = [(j, ((first + j) // 8) * 8) for j in range(CONV_KERNEL)
                    if (first + j) % 8 == r]
            lo = min(off for _, off in taps)
            hi = max(off for _, off in taps)
            src = ext_ref if r == 0 else sh_ref.at[r - 1]
            slab = src[pl.ds(base + lo, rc + hi - lo), :]
            for j, off in taps:
                acc = acc + slab[off - lo:off - lo + rc] * dww_ref[j:j + 1, :]
        y = acc + dwb_ref[...]
        gw = CONV_WIDTH // CONV_GROUPS
        for g in range(CONV_GROUPS):
            sl = slice(g * gw, (g + 1) * gw)
            yg = y[:, sl]
            mu = jnp.mean(yg, axis=-1, keepdims=True)
            dv = yg - mu
            var = jnp.mean(dv * dv, axis=-1, keepdims=True)
            yn = dv * lax.rsqrt(var + EPS) * ng_ref[:, sl] + nb_ref[:, sl]
            yn = yn * jax.nn.sigmoid(yn)
            yn_ref[pl.ds(base, rc), sl] = yn.astype(BF16)
        return 0

    lax.fori_loop(0, tm // rc, conv_chunk, 0)
    oc_ref[...] = (_dot(yn_ref[...], pww_ref[...]) + pwb_ref[...]).astype(oc_ref.dtype)

    pc = min(LANE, tm)
    for c in range(tm // pc):
        b0 = hdr + c * pc
        row = lax.broadcasted_iota(jnp.int32, (pc, POOL_GROUP), 0) + c * pc
        pos = row + (start_pos + (i * tm if carry else 0))
        for g, w in enumerate(POOL_WINDOWS):
            sl = slice(g * POOL_GROUP, (g + 1) * POOL_GROUP)
            u = pext_ref[b0:b0 + pc, sl]
            s = u
            for dlt in range(1, w):
                s = s + pext_ref[pl.ds(b0 - dlt, pc), sl]
            cnt = jnp.minimum(w, pos + 1).astype(F32)
            pooled_ref[c * pc:(c + 1) * pc, sl] = (s / cnt - u).astype(BF16)
    op_ref[...] = (_dot(pooled_ref[...], plw_ref[...]) * pls_ref[...]).astype(op_ref.dtype)

    if carry:
        ext_ref[0:hdr, :] = ext_ref[tm:tm + hdr, :]
        pext_ref[0:hdr, :] = pext_ref[tm:tm + hdr, :]


def mixers_call(uc, up, hist_c, hist_p, lw, carry, start_pos):
    m = uc.shape[0]
    nseq = hist_c.shape[0]
    tm = min(ROW_TILE, m) if carry else m // nseq
    row = lambda i: (i, 0)
    const = lambda i: (0, 0)
    seq = (lambda i: (0, 0, 0)) if carry else (lambda i: (i, 0, 0))
    vec = pl.BlockSpec((1, CONV_WIDTH), const)
    sq = pl.BlockSpec((CONV_WIDTH, CONV_WIDTH), const)
    return pl.pallas_call(
        functools.partial(_mixers_kernel, tm=tm, carry=carry, start_pos=start_pos),
        grid=(m // tm,),
        in_specs=[pl.BlockSpec((tm, 2 * CONV_WIDTH), row),
                  pl.BlockSpec((tm, POOL_WIDTH), row),
                  pl.BlockSpec((None, HIST_ROWS, CONV_WIDTH), seq),
                  pl.BlockSpec((None, HIST_ROWS, POOL_WIDTH), seq),
                  pl.BlockSpec((HIST_ROWS, CONV_WIDTH), const),
                  vec, vec, vec, sq, vec, sq, vec],
        out_specs=[pl.BlockSpec((tm, CONV_WIDTH), row),
                   pl.BlockSpec((tm, POOL_WIDTH), row),
                   pl.BlockSpec((None, CONV_KERNEL - 1, CONV_WIDTH), seq),
                   pl.BlockSpec((None, POOL_HIST, POOL_WIDTH), seq)],
        out_shape=[jax.ShapeDtypeStruct((m, CONV_WIDTH), BF16),
                   jax.ShapeDtypeStruct((m, POOL_WIDTH), BF16),
                   jax.ShapeDtypeStruct((nseq, CONV_KERNEL - 1, CONV_WIDTH), F32),
                   jax.ShapeDtypeStruct((nseq, POOL_HIST, POOL_WIDTH), F32)],
        scratch_shapes=[pltpu.VMEM((HIST_ROWS + tm + 8, CONV_WIDTH), F32),
                        pltpu.VMEM((7, tm + HIST_ROWS - 8, CONV_WIDTH), F32),
                        pltpu.VMEM((tm, CONV_WIDTH), BF16),
                        pltpu.VMEM((HIST_ROWS + tm, POOL_WIDTH), F32),
                        pltpu.VMEM((tm, POOL_WIDTH), BF16)],
        compiler_params=_params("arbitrary"),
        name="mixers_prompt" if carry else "mixers_sample",
    )(uc, up, hist_c, hist_p, lw["dw_w"], lw["dw_b"], lw["n_g"], lw["n_b"],
      lw["pw_w"], lw["pw_b"], lw["pool_w"], lw["pool_s"])


def _outproj_kernel(h_ref, osb_ref, oc_ref, op_ref, w_ref, g_ref, hn_ref, xn_ref):
    c1 = SB_WIDTH
    c2 = SB_WIDTH + CONV_WIDTH
    mix = (_dot(osb_ref[...], w_ref[0:c1, :]) + _dot(oc_ref[...], w_ref[c1:c2, :])
           + _dot(op_ref[...], w_ref[c2:, :]))
    h = h_ref[...] + mix
    hn_ref[...] = h
    xn_ref[...] = _rmsnorm_rows(h, g_ref[...]).astype(xn_ref.dtype)


def outproj_call(h, osb, oc, op, w_out, g, layer):
    m, d = h.shape
    tm = min(ROW_TILE, m)
    row = lambda i: (i, 0)
    return pl.pallas_call(
        _outproj_kernel,
        grid=(m // tm,),
        in_specs=[pl.BlockSpec((tm, d), row),
                  pl.BlockSpec((tm, SB_WIDTH), row),
                  pl.BlockSpec((tm, CONV_WIDTH), row),
                  pl.BlockSpec((tm, POOL_WIDTH), row),
                  pl.BlockSpec((None, d, d), lambda i: (layer, 0, 0),
                               pipeline_mode=pl.Buffered(1)),
                  pl.BlockSpec((1, d), lambda i: (0, 0))],
        out_specs=[pl.BlockSpec((tm, d), row), pl.BlockSpec((tm, d), row)],
        out_shape=[jax.ShapeDtypeStruct((m, d), F32), jax.ShapeDtypeStruct((m, d), BF16)],
        compiler_params=_params("arbitrary"),
        name="outproj",
    )(h, osb, oc, op, w_out, g.reshape(1, d))


def _mlp_kernel(xn_ref, h_ref, wu_ref, wd_ref, g_ref, *rest, final):
    if final:
        y_ref, acc_ref = rest
    else:
        hn_ref, y_ref, acc_ref = rest
    f = pl.program_id(1)

    @pl.when(f == 0)
    def _():
        acc_ref[...] = jnp.zeros_like(acc_ref)

    a = jnp.maximum(_dot(xn_ref[...], wu_ref[...]), 0.0)
    acc_ref[...] += _dot((a * a).astype(BF16), wd_ref[...])

    @pl.when(f == pl.num_programs(1) - 1)
    def _():
        h = h_ref[...] + acc_ref[...]
        if not final:
            hn_ref[...] = h
        y_ref[...] = _rmsnorm_rows(h, g_ref[...]).astype(y_ref.dtype)


def mlp_call(xn, h, w_up, w_down, g, final, layer):
    m, d = h.shape
    ff = w_up.shape[2]
    tm = min(ROW_TILE, m)
    tf = FF_TILE
    row = lambda i, f: (i, 0)
    y_dtype = F32 if final else BF16
    out_specs = [pl.BlockSpec((tm, d), row)]
    out_shape = [jax.ShapeDtypeStruct((m, d), y_dtype)]
    if not final:
        out_specs = [pl.BlockSpec((tm, d), row)] + out_specs
        out_shape = [jax.ShapeDtypeStruct((m, d), F32)] + out_shape
    return pl.pallas_call(
        functools.partial(_mlp_kernel, final=final),
        grid=(m // tm, ff // tf),
        in_specs=[pl.BlockSpec((tm, d), row),
                  pl.BlockSpec((tm, d), row),
                  pl.BlockSpec((None, d, tf), lambda i, f: (layer, 0, f)),
                  pl.BlockSpec((None, tf, d), lambda i, f: (layer, f, 0)),
                  pl.BlockSpec((1, d), lambda i, f: (0, 0))],
        out_specs=out_specs,
        out_shape=out_shape,
        scratch_shapes=[pltpu.VMEM((tm, d), F32)],
        compiler_params=_params("arbitrary", "arbitrary"),
        name="mlp_final" if final else "mlp",
    )(xn, h, w_up, w_down, g.reshape(1, d))


def _pad_hist(hist, rows):
    return jnp.pad(hist, ((0, 0), (rows - hist.shape[1], 0), (0, 0)))


def _block_diag(w):
    g, a, b = w.shape
    eye = jnp.eye(g, dtype=w.dtype)
    return (eye[:, None, :, None] * w[:, :, None, :]).reshape(g * a, g * b)


def _group_trunk(x, layers, big, final_g, sample, caches):
    depth = len(layers)
    h = x
    xn = rmsnorm_call(x, layers[0]["norm_mix_g"])
    cs, ps = [], []
    y = None
    shape = kv_stack_shape(x.shape[0], sample, depth)
    kv = (jnp.zeros(shape, F32), jnp.zeros(shape, F32))
    for l, lw in enumerate(layers):
        if sample:
            cache_k, cache_v, cache_conv, state_pool = caches
            (q, uc, up), kv = inproj_call(xn, big["w_in"], True, l, kv)
            osb = attn_sample_call(q, kv[0], kv[1], cache_k, cache_v, l)
            hist_c = _pad_hist(cache_conv[l], HIST_ROWS)
            hist_p = _pad_hist(state_pool[l], HIST_ROWS)
            start = cache_k.shape[3]
        else:
            (qt, kb, vt, uc, up), kv = inproj_call(xn, big["w_in"], False, l, kv)
            osb = attn_prompt_call(qt, kb, vt)
            hist_c = jnp.zeros((1, HIST_ROWS, CONV_WIDTH), F32)
            hist_p = jnp.zeros((1, HIST_ROWS, POOL_WIDTH), F32)
            start = 0
        oc, op, c_state, p_state = mixers_call(uc, up, hist_c, hist_p, lw, not sample, start)
        h, xn2 = outproj_call(h, osb, oc, op, big["w_out"], lw["norm_mlp_g"], l)
        if l + 1 < depth:
            h, xn = mlp_call(xn2, h, big["w_up"], big["w_down"], layers[l + 1]["norm_mix_g"],
                             False, l)
        else:
            (y,) = mlp_call(xn2, h, big["w_up"], big["w_down"], final_g, True, l)
        cs.append(c_state)
        ps.append(p_state)
    return y, kv[0], kv[1], jnp.stack(cs), jnp.stack(ps)


def kernel(x_prompt, x_sample, cache_k, cache_v, cache_conv, state_pool, norm_mix_g, w_in,
           conv_dw_w, conv_dw_b, conv_norm_g, conv_norm_b, conv_pw_w, conv_pw_b, pool_w,
           pool_scale, w_out, norm_mlp_g, w_up, w_down, final_norm_g):
    depth = w_in.shape[0]
    bp, seq, d = x_prompt.shape
    bs, dseq, _ = x_sample.shape
    assert bp == 1 and dseq == DEC_SEQ and d == D_MODEL
    layers = []
    for l in range(depth):
        layers.append(dict(
            norm_mix_g=norm_mix_g[l],
            dw_w=jnp.pad(conv_dw_w[l], ((0, HIST_ROWS - CONV_KERNEL), (0, 0))),
            dw_b=conv_dw_b[l].reshape(1, -1),
            n_g=conv_norm_g[l].reshape(1, -1),
            n_b=conv_norm_b[l].reshape(1, -1),
            pw_w=conv_pw_w[l].astype(BF16),
            pw_b=conv_pw_b[l].reshape(1, -1),
            pool_w=_block_diag(pool_w[l]).astype(BF16),
            pool_s=pool_scale[l].reshape(1, -1),
            norm_mlp_g=norm_mlp_g[l],
        ))
    big = dict(w_in=w_in.astype(BF16), w_out=w_out.astype(BF16),
               w_up=w_up.astype(BF16), w_down=w_down.astype(BF16))
    yp, kp, vp, cp, pp = _group_trunk(x_prompt.reshape(seq, d), layers, big, final_norm_g,
                                      False, None)
    ys, ks, vs, cs, ps = _group_trunk(x_sample.reshape(bs * dseq, d), layers, big, final_norm_g, True,
                                      (cache_k, cache_v, cache_conv, state_pool))
    return (yp.reshape(bp, seq, d), ys.reshape(bs, dseq, d),
            kp[:, None], vp[:, None], cp, pp, ks, vs, cs, ps)
```

```python
import functools
import math

import jax
import jax.numpy as jnp
from jax import lax
from jax.experimental import pallas as pl
from jax.experimental.pallas import tpu as pltpu

F32 = jnp.float32
BF16 = jnp.bfloat16

D_MODEL = 2048
SB_WIDTH = 1024
HEAD_DIM = 128
HEADS = SB_WIDTH // HEAD_DIM
CONV_WIDTH = 512
CONV_KERNEL = 31
CONV_GROUPS = 4
POOL_WIDTH = 512
POOL_WINDOWS = (2, 4, 8, 16)
POOL_GROUP = POOL_WIDTH // len(POOL_WINDOWS)
POOL_HIST = max(POOL_WINDOWS) - 1
D_FF = 4 * D_MODEL
IN_COLS = 3 * SB_WIDTH + 2 * CONV_WIDTH + POOL_WIDTH
EPS = 1e-6
DEC_SEQ = 64

LOG2E = 1.4426950408889634
Q_SCALE = HEAD_DIM ** -0.5 * LOG2E

LANE = 128
HIST_ROWS = 32
ROW_TILE = 512
FF_TILE = 1024
ATTN_TK = 256
ATTN_VT = 2 * ATTN_TK
CONV_CHUNK = 64
VMEM_LIMIT = 56 * 1024 * 1024


def _params(*sem):
    return pltpu.CompilerParams(dimension_semantics=sem, vmem_limit_bytes=VMEM_LIMIT)


def _dot(a, b):
    return jnp.dot(a, b, preferred_element_type=F32)


def _softplus2(z):
    e = jnp.exp2(-jnp.abs(z))
    return jnp.maximum(z, 0.0) + jnp.log(1.0 + e) * LOG2E


def _rmsnorm_rows(x, g):
    ms = jnp.mean(x * x, axis=-1, keepdims=True)
    return x * lax.rsqrt(ms + EPS) * g


def _rmsnorm_kernel(x_ref, g_ref, o_ref):
    o_ref[...] = _rmsnorm_rows(x_ref[...], g_ref[...]).astype(o_ref.dtype)


def rmsnorm_call(x, g):
    m, d = x.shape
    tm = min(ROW_TILE, m)
    return pl.pallas_call(
        _rmsnorm_kernel,
        grid=(m // tm,),
        in_specs=[pl.BlockSpec((tm, d), lambda i: (i, 0)),
                  pl.BlockSpec((1, d), lambda i: (0, 0))],
        out_specs=pl.BlockSpec((tm, d), lambda i: (i, 0)),
        out_shape=jax.ShapeDtypeStruct((m, d), BF16),
        compiler_params=_params("arbitrary"),
        name="rmsnorm",
    )(x, g.reshape(1, d))


def _inproj_kernel(xn_ref, w_ref, *rest, tm, sample, n_alias):
    outs = rest[n_alias:]
    x = xn_ref[...]
    if sample:
        q_ref, kf_ref, vf_ref, uc_ref, up_ref = outs
        nb = tm // DEC_SEQ
    else:
        qt_ref, kb_ref, vt_ref, kf_ref, vf_ref, uc_ref, up_ref = outs
    pair = 2 * HEAD_DIM
    for hp in range(HEADS // 2):
        c0 = hp * pair
        q2 = _dot(x, w_ref[:, c0:c0 + pair]) * Q_SCALE
        k2 = _dot(x, w_ref[:, SB_WIDTH + c0:SB_WIDTH + c0 + pair])
        v2 = _dot(x, w_ref[:, 2 * SB_WIDTH + c0:2 * SB_WIDTH + c0 + pair])
        for u in range(2):
            h = 2 * hp + u
            sl = slice(u * HEAD_DIM, (u + 1) * HEAD_DIM)
            if sample:
                q_ref[:, h] = q2[:, sl].astype(BF16).reshape(nb, DEC_SEQ, HEAD_DIM)
                kf_ref[:, h] = k2[:, sl].reshape(nb, DEC_SEQ, HEAD_DIM)
                vf_ref[:, h] = v2[:, sl].reshape(nb, DEC_SEQ, HEAD_DIM)
            else:
                qt_ref[h] = q2[:, sl].T.astype(BF16)
                kb_ref[h] = k2[:, sl].astype(BF16)
                kf_ref[h] = k2[:, sl]
                vf_ref[h] = v2[:, sl]
                vt = v2[:, sl].T.astype(BF16)
                for c in range(tm // ATTN_VT):
                    vt_ref[h, c] = vt[:, c * ATTN_VT:(c + 1) * ATTN_VT]
    c0 = 3 * SB_WIDTH
    uc_ref[...] = _dot(x, w_ref[:, c0:c0 + 2 * CONV_WIDTH])
    up_ref[...] = _dot(x, w_ref[:, c0 + 2 * CONV_WIDTH:])


def kv_stack_shape(m, sample, depth):
    if sample:
        return (depth, m // DEC_SEQ, HEADS, DEC_SEQ, HEAD_DIM)
    return (depth, HEADS, m, HEAD_DIM)


def inproj_call(xn, w_in, sample, layer, kv_stack):
    depth = kv_stack[0].shape[0]
    m, d = xn.shape
    tm = min(ROW_TILE, m)
    if sample:
        nbt = m // DEC_SEQ
        nb = tm // DEC_SEQ
        hshape = (nbt, HEADS, DEC_SEQ, HEAD_DIM)
        hblock = (nb, HEADS, DEC_SEQ, HEAD_DIM)
        kv_shape = jax.ShapeDtypeStruct((depth,) + hshape, F32)
        kv_spec = pl.BlockSpec((None,) + hblock, lambda i: (layer, i, 0, 0, 0))
        out_shape = [jax.ShapeDtypeStruct(hshape, BF16), kv_shape, kv_shape]
        out_specs = [pl.BlockSpec(hblock, lambda i: (i, 0, 0, 0)), kv_spec, kv_spec]
        kv_index = (1, 2)
    else:
        nkb = m // ATTN_VT
        kshape = (HEADS, m, HEAD_DIM)
        kblock = (HEADS, tm, HEAD_DIM)
        kv_shape = jax.ShapeDtypeStruct((depth,) + kshape, F32)
        kv_spec = pl.BlockSpec((None,) + kblock, lambda i: (layer, 0, i, 0))
        out_shape = [jax.ShapeDtypeStruct((HEADS, HEAD_DIM, m), BF16),
                     jax.ShapeDtypeStruct(kshape, BF16),
                     jax.ShapeDtypeStruct((HEADS, nkb, HEAD_DIM, ATTN_VT), BF16),
                     kv_shape, kv_shape]
        out_specs = [pl.BlockSpec((HEADS, HEAD_DIM, tm), lambda i: (0, 0, i)),
                     pl.BlockSpec(kblock, lambda i: (0, i, 0)),
                     pl.BlockSpec((HEADS, tm // ATTN_VT, HEAD_DIM, ATTN_VT),
                                  lambda i: (0, i, 0, 0)),
                     kv_spec, kv_spec]
        kv_index = (3, 4)
    out_shape += [jax.ShapeDtypeStruct((m, 2 * CONV_WIDTH), F32),
                  jax.ShapeDtypeStruct((m, POOL_WIDTH), F32)]
    out_specs += [pl.BlockSpec((tm, 2 * CONV_WIDTH), lambda i: (i, 0)),
                  pl.BlockSpec((tm, POOL_WIDTH), lambda i: (i, 0))]
    in_specs = [pl.BlockSpec((tm, d), lambda i: (i, 0)),
                pl.BlockSpec((None, d, IN_COLS), lambda i: (layer, 0, 0),
                             pipeline_mode=pl.Buffered(1))]
    args = [xn, w_in]
    aliases = {}
    for n, stack in enumerate(kv_stack):
        assert stack.shape == kv_shape.shape and stack.dtype == kv_shape.dtype
        aliases[len(args)] = kv_index[n]
        in_specs.append(pl.BlockSpec(memory_space=pl.ANY))
        args.append(stack)
    outs = pl.pallas_call(
        functools.partial(_inproj_kernel, tm=tm, sample=sample, n_alias=len(aliases)),
        grid=(m // tm,),
        in_specs=in_specs,
        out_specs=out_specs,
        out_shape=out_shape,
        input_output_aliases=aliases,
        compiler_params=_params("arbitrary"),
        name="inproj_sample" if sample else "inproj_prompt",
    )(*args)
    kv = tuple(outs[n] for n in kv_index)
    return [o for n, o in enumerate(outs) if n not in kv_index], kv


MASKED_LOGIT = -1e30
STICK_EXHAUSTED = 160.0


def _attn_stage1(k_ref, qt, p, i, masked, a_ref, sp_ref, *, tq, tk):
    for half in (1, 0):
        row0 = pl.multiple_of(p * (2 * tk) + half * tk, tk)
        z = _dot(k_ref[pl.ds(row0, tk), :], qt)
        sp = _softplus2(z)
        a = z - sp
        if masked:
            s_idx = row0 + lax.broadcasted_iota(jnp.int32, (tk, tq), 0)
            t_idx = i * tq + lax.broadcasted_iota(jnp.int32, (tk, tq), 1)
            valid = s_idx < t_idx
            sp = jnp.where(valid, sp, 0.0)
            a = jnp.where(valid, a, MASKED_LOGIT)
        a_ref[half * tk:(half + 1) * tk, :] = a
        sp_ref[half] = sp.astype(BF16)


def _attn_stage2(tri_ref, a_ref, sp_ref, half, carry, *, tk):
    sp = sp_ref[half]
    later = _dot(tri_ref[...], sp)
    w = jnp.exp2(a_ref[half * tk:(half + 1) * tk, :] - later - carry)
    return w.astype(BF16), carry + later[0:1, :] + sp[0:1, :].astype(F32)


ATTN_HEADS_PER_STEP = 4
ATTN_STAGE_BUFS = 2
PV_LAG = 2


def _attn_prompt_kernel(qt_ref, k_ref, vt_ref, tri_ref, o_ref, *scratch, tq, tk):
    i = pl.program_id(1)
    nh = ATTN_HEADS_PER_STEP
    heads = range(nh)
    acc = scratch[0:nh]
    zb = [scratch[nh + b * nh:nh + (b + 1) * nh] for b in range(ATTN_STAGE_BUFS)]
    sp0 = nh * (1 + ATTN_STAGE_BUFS)
    spb = [scratch[sp0 + b * nh:sp0 + (b + 1) * nh] for b in range(ATTN_STAGE_BUFS)]
    s1 = functools.partial(_attn_stage1, tq=tq, tk=tk)
    s2 = functools.partial(_attn_stage2, tk=tk)
    qts = [qt_ref[h] for h in heads]
    for h in heads:
        acc[h][...] = jnp.zeros_like(acc[h])

    def second(buf, p, carry):
        carry = list(carry)
        units = [(h, half) for h in heads for half in (1, 0)]
        ws = {}
        for n in range(len(units) + PV_LAG):
            if n < len(units):
                h, half = units[n]
                ws[units[n]], carry[h] = s2(tri_ref, zb[buf][h], spb[buf][h], half, carry[h])
            if n >= PV_LAG:
                h, half = units[n - PV_LAG]
                vt = vt_ref[h, p, :, half * tk:(half + 1) * tk]
                acc[h][...] += _dot(vt, ws.pop(units[n - PV_LAG]))
        return tuple(carry)

    def step(buf, p, carry):
        for h in heads:
            s1(k_ref.at[h], qts[h], jnp.maximum(p - 1, 0), i, False,
               zb[1 - buf][h], spb[1 - buf][h])
        return second(buf, p, carry)

    for h in heads:
        s1(k_ref.at[h], qts[h], i, i, True, zb[0][h], spb[0][h])

    def live(state):
        m, _, least = state
        return jnp.logical_and(m < (i + 1) // 2, least <= STICK_EXHAUSTED)

    def body(state):
        m, carry, _ = state
        p = i - 2 * m
        carry = step(1, p - 1, step(0, p, carry))
        least = functools.reduce(jnp.minimum, [jnp.min(c) for c in carry])
        return m + 1, carry, least

    zero = tuple(jnp.zeros((1, tq), F32) for _ in heads)
    _, carry, least = lax.while_loop(live, body, (jnp.int32(0), zero, jnp.float32(0.0)))

    @pl.when(jnp.logical_and(i % 2 == 0, least <= STICK_EXHAUSTED))
    def _():
        second(0, 0, carry)

    d = acc[0].shape[0]
    for h in heads:
        o_ref[:, h * d:(h + 1) * d] = acc[h][...].T.astype(o_ref.dtype)


def _tri(tk):
    s = lax.broadcasted_iota(jnp.int32, (tk, tk), 0)
    j = lax.broadcasted_iota(jnp.int32, (tk, tk), 1)
    return (j > s).astype(BF16)


def attn_prompt_call(qt, kb, vt):
    h, d, t = qt.shape
    tq = ATTN_VT
    tk = ATTN_TK
    tri = _tri(tk)
    nh = ATTN_HEADS_PER_STEP
    nstage = nh * ATTN_STAGE_BUFS
    return pl.pallas_call(
        functools.partial(_attn_prompt_kernel, tq=tq, tk=tk),
        grid=(h // nh, t // tq),
        in_specs=[pl.BlockSpec((nh, d, tq), lambda hh, i: (hh, 0, i)),
                  pl.BlockSpec((nh, t, d), lambda hh, i: (hh, 0, 0)),
                  pl.BlockSpec((nh, t // ATTN_VT, d, ATTN_VT), lambda hh, i: (hh, 0, 0, 0)),
                  pl.BlockSpec((tk, tk), lambda hh, i: (0, 0))],
        out_specs=pl.BlockSpec((tq, nh * d), lambda hh, i: (i, hh)),
        out_shape=jax.ShapeDtypeStruct((t, h * d), BF16),
        scratch_shapes=([pltpu.VMEM((d, tq), F32)] * nh
                        + [pltpu.VMEM((2 * tk, tq), F32)] * nstage
                        + [pltpu.VMEM((2, tk, tq), BF16)] * nstage),
        compiler_params=_params("arbitrary", "arbitrary"),
        name="attn_prompt",
    )(qt, kb, vt, tri)


SAMPLE_HEADS_PER_STEP = 4


def _attn_sample_kernel(q_ref, kn_ref, vn_ref, ck_ref, cv_ref, tri_ref, o_ref,
                        kb_ref, vb_ref, z_ref, sp_ref, cum_ref, w_ref, *, past, tk):
    nt = (((1,), (1,)), ((), ()))
    nh, ts, d = q_ref.shape
    rows = nh * ts
    nblk = past // tk + 1
    tot = nblk * tk
    for h in range(nh):
        for src, new, dst in ((ck_ref, kn_ref, kb_ref), (cv_ref, vn_ref, vb_ref)):
            dst[h, 0:past, :] = src[h].astype(BF16)
            dst[h, past:past + ts, :] = new[h].astype(BF16)
            dst[h, past + ts:tot, :] = jnp.zeros((tot - past - ts, d), BF16)
        z_ref[h * ts:(h + 1) * ts, :] = lax.dot_general(
            q_ref[h], kb_ref[h], nt, preferred_element_type=F32)

    for b in range(nblk):
        cols = slice(b * tk, (b + 1) * tk)
        z = z_ref[:, cols]
        sp = _softplus2(z)
        a = z - sp
        if b == nblk - 1:
            t_idx = lax.broadcasted_iota(jnp.int32, (rows, tk), 0) % ts
            s_idx = lax.broadcasted_iota(jnp.int32, (rows, tk), 1)
            valid = s_idx < t_idx
            sp = jnp.where(valid, sp, 0.0)
            a = jnp.where(valid, a, MASKED_LOGIT)
        z_ref[:, cols] = a
        sp_ref[b * rows:(b + 1) * rows, :] = sp.astype(BF16)

    cum_ref[...] = _dot(sp_ref[...], tri_ref[...])

    carry = jnp.zeros((rows, 1), F32)
    for b in range(nblk - 1, -1, -1):
        cols = slice(b * tk, (b + 1) * tk)
        later = cum_ref[b * rows:(b + 1) * rows, :]
        w_ref[:, cols] = jnp.exp2(z_ref[:, cols] - later - carry).astype(BF16)
        first = sp_ref[b * rows:(b + 1) * rows, 0:1].astype(F32)
        carry = carry + later[:, 0:1] + first

    for h in range(nh):
        o_ref[:, h * d:(h + 1) * d] = _dot(w_ref[h * ts:(h + 1) * ts, :],
                                           vb_ref[h]).astype(o_ref.dtype)


def attn_sample_call(q, k_stack, v_stack, cache_k, cache_v, layer):
    b, h, ts, d = q.shape
    past = cache_k.shape[3]
    tk = ATTN_TK
    nh = SAMPLE_HEADS_PER_STEP
    rows = nh * ts
    tot = past + tk
    tri = _tri(tk).T
    q_spec = pl.BlockSpec((None, nh, ts, d), lambda bb, hh: (bb, hh, 0, 0))
    new_spec = pl.BlockSpec((None, None, nh, ts, d), lambda bb, hh: (layer, bb, hh, 0, 0))
    cache_spec = pl.BlockSpec((None, None, nh, past, d), lambda bb, hh: (layer, bb, hh, 0, 0))
    return pl.pallas_call(
        functools.partial(_attn_sample_kernel, past=past, tk=tk),
        grid=(b, h // nh),
        in_specs=[q_spec, new_spec, new_spec, cache_spec, cache_spec,
                  pl.BlockSpec((tk, tk), lambda bb, hh: (0, 0))],
        out_specs=pl.BlockSpec((ts, nh * d), lambda bb, hh: (bb, hh)),
        out_shape=jax.ShapeDtypeStruct((b * ts, h * d), BF16),
        scratch_shapes=[pltpu.VMEM((nh, tot, d), BF16), pltpu.VMEM((nh, tot, d), BF16),
                        pltpu.VMEM((rows, tot), F32),
                        pltpu.VMEM((tot // tk * rows, tk), BF16),
                        pltpu.VMEM((tot // tk * rows, tk), F32),
                        pltpu.VMEM((rows, tot), BF16)],
        compiler_params=_params("arbitrary", "arbitrary"),
        name="attn_sample",
    )(q, k_stack, v_stack, cache_k, cache_v, tri)


def _mixers_kernel(uc_ref, up_ref, hc_ref, hp_ref, dww_ref, dwb_ref, ng_ref, nb_ref,
                   pww_ref, pwb_ref, plw_ref, pls_ref,
                   oc_ref, op_ref, cs_ref, ps_ref,
                   ext_ref, sh_ref, yn_ref, pext_ref, pooled_ref,
                   *, tm, carry, start_pos):
    i = pl.program_id(0)
    hdr = HIST_ROWS

    def load_hist():
        ext_ref[0:hdr, :] = hc_ref[...]
        pext_ref[0:hdr, :] = hp_ref[...]

    if carry:
        pl.when(i == 0)(load_hist)
    else:
        load_hist()

    ext_ref[hdr:hdr + tm, :] = uc_ref[:, 0:CONV_WIDTH] * jax.nn.sigmoid(uc_ref[:, CONV_WIDTH:])
    pext_ref[hdr:hdr + tm, :] = up_ref[...]
    cs_ref[...] = ext_ref[pl.ds(hdr + tm - (CONV_KERNEL - 1), CONV_KERNEL - 1), :]
    ps_ref[...] = pext_ref[pl.ds(hdr + tm - POOL_HIST, POOL_HIST), :]

    sh_rows = sh_ref.shape[1]
    for r in range(1, 8):
        sh_ref[r - 1] = ext_ref[pl.ds(r, sh_rows), :]

    rc = min(CONV_CHUNK, tm)
    first = hdr - (CONV_KERNEL - 1)

    def conv_chunk(c, _):
        base = pl.multiple_of(c * rc, rc)
        acc = jnp.zeros((rc, CONV_WIDTH), F32)
        for r in range(8):
            taps = [(j, ((first + j) // 8) * 8) for j in range(CONV_KERNEL)
                    if (first + j) % 8 == r]
            lo = min(off for _, off in taps)
            hi = max(off for _, off in taps)
            src = ext_ref if r == 0 else sh_ref.at[r - 1]
            slab = src[pl.ds(base + lo, rc + hi - lo), :]
            for j, off in taps:
                acc = acc + slab[off - lo:off - lo + rc] * dww_ref[j:j + 1, :]
        y = acc + dwb_ref[...]
        gw = CONV_WIDTH // CONV_GROUPS
        for g in range(CONV_GROUPS):
            sl = slice(g * gw, (g + 1) * gw)
            yg = y[:, sl]
            mu = jnp.mean(yg, axis=-1, keepdims=True)
            dv = yg - mu
            var = jnp.mean(dv * dv, axis=-1, keepdims=True)
            yn = dv * lax.rsqrt(var + EPS) * ng_ref[:, sl] + nb_ref[:, sl]
            yn = yn * jax.nn.sigmoid(yn)
            yn_ref[pl.ds(base, rc), sl] = yn.astype(BF16)
        return 0

    lax.fori_loop(0, tm // rc, conv_chunk, 0)
    oc_ref[...] = (_dot(yn_ref[...], pww_ref[...]) + pwb_ref[...]).astype(oc_ref.dtype)

    pc = min(LANE, tm)
    for c in range(tm // pc):
        b0 = hdr + c * pc
        row = lax.broadcasted_iota(jnp.int32, (pc, POOL_GROUP), 0) + c * pc
        pos = row + (start_pos + (i * tm if carry else 0))
        for g, w in enumerate(POOL_WINDOWS):
            sl = slice(g * POOL_GROUP, (g + 1) * POOL_GROUP)
            u = pext_ref[b0:b0 + pc, sl]
            s = u
            for dlt in range(1, w):
                s = s + pext_ref[pl.ds(b0 - dlt, pc), sl]
            cnt = jnp.minimum(w, pos + 1).astype(F32)
            pooled_ref[c * pc:(c + 1) * pc, sl] = (s / cnt - u).astype(BF16)
    op_ref[...] = (_dot(pooled_ref[...], plw_ref[...]) * pls_ref[...]).astype(op_ref.dtype)

    if carry:
        ext_ref[0:hdr, :] = ext_ref[tm:tm + hdr, :]
        pext_ref[0:hdr, :] = pext_ref[tm:tm + hdr, :]


def mixers_call(uc, up, hist_c, hist_p, lw, carry, start_pos):
    m = uc.shape[0]
    nseq = hist_c.shape[0]
    tm = min(ROW_TILE, m) if carry else m // nseq
    row = lambda i: (i, 0)
    const = lambda i: (0, 0)
    seq = (lambda i: (0, 0, 0)) if carry else (lambda i: (i, 0, 0))
    vec = pl.BlockSpec((1, CONV_WIDTH), const)
    sq = pl.BlockSpec((CONV_WIDTH, CONV_WIDTH), const)
    return pl.pallas_call(
        functools.partial(_mixers_kernel, tm=tm, carry=carry, start_pos=start_pos),
        grid=(m // tm,),
        in_specs=[pl.BlockSpec((tm, 2 * CONV_WIDTH), row),
                  pl.BlockSpec((tm, POOL_WIDTH), row),
                  pl.BlockSpec((None, HIST_ROWS, CONV_WIDTH), seq),
                  pl.BlockSpec((None, HIST_ROWS, POOL_WIDTH), seq),
                  pl.BlockSpec((HIST_ROWS, CONV_WIDTH), const),
                  vec, vec, vec, sq, vec, sq, vec],
        out_specs=[pl.BlockSpec((tm, CONV_WIDTH), row),
                   pl.BlockSpec((tm, POOL_WIDTH), row),
                   pl.BlockSpec((None, CONV_KERNEL - 1, CONV_WIDTH), seq),
                   pl.BlockSpec((None, POOL_HIST, POOL_WIDTH), seq)],
        out_shape=[jax.ShapeDtypeStruct((m, CONV_WIDTH), BF16),
                   jax.ShapeDtypeStruct((m, POOL_WIDTH), BF16),
                   jax.ShapeDtypeStruct((nseq, CONV_KERNEL - 1, CONV_WIDTH), F32),
                   jax.ShapeDtypeStruct((nseq, POOL_HIST, POOL_WIDTH), F32)],
        scratch_shapes=[pltpu.VMEM((HIST_ROWS + tm + 8, CONV_WIDTH), F32),
                        pltpu.VMEM((7, tm + HIST_ROWS - 8, CONV_WIDTH), F32),
                        pltpu.VMEM((tm, CONV_WIDTH), BF16),
                        pltpu.VMEM((HIST_ROWS + tm, POOL_WIDTH), F32),
                        pltpu.VMEM((tm, POOL_WIDTH), BF16)],
        compiler_params=_params("arbitrary"),
        name="mixers_prompt" if carry else "mixers_sample",
    )(uc, up, hist_c, hist_p, lw["dw_w"], lw["dw_b"], lw["n_g"], lw["n_b"],
      lw["pw_w"], lw["pw_b"], lw["pool_w"], lw["pool_s"])


def _outproj_kernel(h_ref, osb_ref, oc_ref, op_ref, w_ref, g_ref, hn_ref, xn_ref):
    c1 = SB_WIDTH
    c2 = SB_WIDTH + CONV_WIDTH
    mix = (_dot(osb_ref[...], w_ref[0:c1, :]) + _dot(oc_ref[...], w_ref[c1:c2, :])
           + _dot(op_ref[...], w_ref[c2:, :]))
    h = h_ref[...] + mix
    hn_ref[...] = h
    xn_ref[...] = _rmsnorm_rows(h, g_ref[...]).astype(xn_ref.dtype)


def outproj_call(h, osb, oc, op, w_out, g, layer):
    m, d = h.shape
    tm = min(ROW_TILE, m)
    row = lambda i: (i, 0)
    return pl.pallas_call(
        _outproj_kernel,
        grid=(m // tm,),
        in_specs=[pl.BlockSpec((tm, d), row),
                  pl.BlockSpec((tm, SB_WIDTH), row),
                  pl.BlockSpec((tm, CONV_WIDTH), row),
                  pl.BlockSpec((tm, POOL_WIDTH), row),
                  pl.BlockSpec((None, d, d), lambda i: (layer, 0, 0),
                               pipeline_mode=pl.Buffered(1)),
                  pl.BlockSpec((1, d), lambda i: (0, 0))],
        out_specs=[pl.BlockSpec((tm, d), row), pl.BlockSpec((tm, d), row)],
        out_shape=[jax.ShapeDtypeStruct((m, d), F32), jax.ShapeDtypeStruct((m, d), BF16)],
        compiler_params=_params("arbitrary"),
        name="outproj",
    )(h, osb, oc, op, w_out, g.reshape(1, d))


def _mlp_kernel(xn_ref, h_ref, wu_ref, wd_ref, g_ref, *rest, final):
    if final:
        y_ref, acc_ref = rest
    else:
        hn_ref, y_ref, acc_ref = rest
    f = pl.program_id(1)

    @pl.when(f == 0)
    def _():
        acc_ref[...] = jnp.zeros_like(acc_ref)

    a = jnp.maximum(_dot(xn_ref[...], wu_ref[...]), 0.0)
    acc_ref[...] += _dot((a * a).astype(BF16), wd_ref[...])

    @pl.when(f == pl.num_programs(1) - 1)
    def _():
        h = h_ref[...] + acc_ref[...]
        if not final:
            hn_ref[...] = h
        y_ref[...] = _rmsnorm_rows(h, g_ref[...]).astype(y_ref.dtype)


def mlp_call(xn, h, w_up, w_down, g, final, layer):
    m, d = h.shape
    ff = w_up.shape[2]
    tm = min(ROW_TILE, m)
    tf = FF_TILE
    row = lambda i, f: (i, 0)
    y_dtype = F32 if final else BF16
    out_specs = [pl.BlockSpec((tm, d), row)]
    out_shape = [jax.ShapeDtypeStruct((m, d), y_dtype)]
    if not final:
        out_specs = [pl.BlockSpec((tm, d), row)] + out_specs
        out_shape = [jax.ShapeDtypeStruct((m, d), F32)] + out_shape
    return pl.pallas_call(
        functools.partial(_mlp_kernel, final=final),
        grid=(m // tm, ff // tf),
        in_specs=[pl.BlockSpec((tm, d), row),
                  pl.BlockSpec((tm, d), row),
                  pl.BlockSpec((None, d, tf), lambda i, f: (layer, 0, f)),
                  pl.BlockSpec((None, tf, d), lambda i, f: (layer, f, 0)),
                  pl.BlockSpec((1, d), lambda i, f: (0, 0))],
        out_specs=out_specs,
        out_shape=out_shape,
        scratch_shapes=[pltpu.VMEM((tm, d), F32)],
        compiler_params=_params("arbitrary", "arbitrary"),
        name="mlp_final" if final else "mlp",
    )(xn, h, w_up, w_down, g.reshape(1, d))


def _pad_hist(hist, rows):
    return jnp.pad(hist, ((0, 0), (rows - hist.shape[1], 0), (0, 0)))


def _block_diag(w):
    g, a, b = w.shape
    eye = jnp.eye(g, dtype=w.dtype)
    return (eye[:, None, :, None] * w[:, :, None, :]).reshape(g * a, g * b)


def _group_trunk(x, layers, big, final_g, sample, caches):
    depth = len(layers)
    h = x
    xn = rmsnorm_call(x, layers[0]["norm_mix_g"])
    cs, ps = [], []
    y = None
    shape = kv_stack_shape(x.shape[0], sample, depth)
    kv = (jnp.zeros(shape, F32), jnp.zeros(shape, F32))
    for l, lw in enumerate(layers):
        if sample:
            cache_k, cache_v, cache_conv, state_pool = caches
            (q, uc, up), kv = inproj_call(xn, big["w_in"], True, l, kv)
            osb = attn_sample_call(q, kv[0], kv[1], cache_k, cache_v, l)
            hist_c = _pad_hist(cache_conv[l], HIST_ROWS)
            hist_p = _pad_hist(state_pool[l], HIST_ROWS)
            start = cache_k.shape[3]
        else:
            (qt, kb, vt, uc, up), kv = inproj_call(xn, big["w_in"], False, l, kv)
            osb = attn_prompt_call(qt, kb, vt)
            hist_c = jnp.zeros((1, HIST_ROWS, CONV_WIDTH), F32)
            hist_p = jnp.zeros((1, HIST_ROWS, POOL_WIDTH), F32)
            start = 0
        oc, op, c_state, p_state = mixers_call(uc, up, hist_c, hist_p, lw, not sample, start)
        h, xn2 = outproj_call(h, osb, oc, op, big["w_out"], lw["norm_mlp_g"], l)
        if l + 1 < depth:
            h, xn = mlp_call(xn2, h, big["w_up"], big["w_down"], layers[l + 1]["norm_mix_g"],
                             False, l)
        else:
            (y,) = mlp_call(xn2, h, big["w_up"], big["w_down"], final_g, True, l)
        cs.append(c_state)
        ps.append(p_state)
    return y, kv[0], kv[1], jnp.stack(cs), jnp.stack(ps)


def kernel(x_prompt, x_sample, cache_k, cache_v, cache_conv, state_pool, norm_mix_g, w_in,
           conv_dw_w, conv_dw_b, conv_norm_g, conv_norm_b, conv_pw_w, conv_pw_b, pool_w,
           pool_scale, w_out, norm_mlp_g, w_up, w_down, final_norm_g):
    depth = w_in.shape[0]
    bp, seq, d = x_prompt.shape
    bs, dseq, _ = x_sample.shape
    assert bp == 1 and dseq == DEC_SEQ and d == D_MODEL
    layers = []
    for l in range(depth):
        layers.append(dict(
            norm_mix_g=norm_mix_g[l],
            dw_w=jnp.pad(conv_dw_w[l], ((0, HIST_ROWS - CONV_KERNEL), (0, 0))),
            dw_b=conv_dw_b[l].reshape(1, -1),
            n_g=conv_norm_g[l].reshape(1, -1),
            n_b=conv_norm_b[l].reshape(1, -1),
            pw_w=conv_pw_w[l].astype(BF16),
            pw_b=conv_pw_b[l].reshape(1, -1),
            pool_w=_block_diag(pool_w[l]).astype(BF16),
            pool_s=pool_scale[l].reshape(1, -1),
            norm_mlp_g=norm_mlp_g[l],
        ))
    big = dict(w_in=w_in.astype(BF16), w_out=w_out.astype(BF16),
               w_up=w_up.astype(BF16), w_down=w_down.astype(BF16))
    yp, kp, vp, cp, pp = _group_trunk(x_prompt.reshape(seq, d), layers, big, final_norm_g,
                                      False, None)
    ys, ks, vs, cs, ps = _group_trunk(x_sample.reshape(bs * dseq, d), layers, big, final_norm_g, True,
                                      (cache_k, cache_v, cache_conv, state_pool))
    return (yp.reshape(bp, seq, d), ys.reshape(bs, dseq, d),
            kp[:, None], vp[:, None], cp, pp, ks, vs, cs, ps)
```

```python
import functools
import math

import jax
import jax.numpy as jnp
from jax import lax
from jax.experimental import pallas as pl
from jax.experimental.pallas import tpu as pltpu

F32 = jnp.float32
BF16 = jnp.bfloat16

D_MODEL = 2048
SB_WIDTH = 1024
HEAD_DIM = 128
HEADS = SB_WIDTH // HEAD_DIM
CONV_WIDTH = 512
CONV_KERNEL = 31
CONV_GROUPS = 4
POOL_WIDTH = 512
POOL_WINDOWS = (2, 4, 8, 16)
POOL_GROUP = POOL_WIDTH // len(POOL_WINDOWS)
POOL_HIST = max(POOL_WINDOWS) - 1
D_FF = 4 * D_MODEL
IN_COLS = 3 * SB_WIDTH + 2 * CONV_WIDTH + POOL_WIDTH
EPS = 1e-6
DEC_SEQ = 64

LOG2E = 1.4426950408889634
Q_SCALE = HEAD_DIM ** -0.5 * LOG2E

LANE = 128
HIST_ROWS = 32
ROW_TILE = 512
FF_TILE = 1024
ATTN_TK = 128
ATTN_VT = 2 * ATTN_TK
SAMPLE_TK = 256
CONV_CHUNK = 64
VMEM_LIMIT = 56 * 1024 * 1024


def _params(*sem):
    return pltpu.CompilerParams(dimension_semantics=sem, vmem_limit_bytes=VMEM_LIMIT)


def _dot(a, b):
    return jnp.dot(a, b, preferred_element_type=F32)


def _softplus2(z):
    e = jnp.exp2(-jnp.abs(z))
    return jnp.maximum(z, 0.0) + jnp.log(1.0 + e) * LOG2E


def _rmsnorm_rows(x, g):
    ms = jnp.mean(x * x, axis=-1, keepdims=True)
    return x * lax.rsqrt(ms + EPS) * g


def _rmsnorm_kernel(x_ref, g_ref, o_ref):
    o_ref[...] = _rmsnorm_rows(x_ref[...], g_ref[...]).astype(o_ref.dtype)


def rmsnorm_call(x, g):
    m, d = x.shape
    tm = min(ROW_TILE, m)
    return pl.pallas_call(
        _rmsnorm_kernel,
        grid=(m // tm,),
        in_specs=[pl.BlockSpec((tm, d), lambda i: (i, 0)),
                  pl.BlockSpec((1, d), lambda i: (0, 0))],
        out_specs=pl.BlockSpec((tm, d), lambda i: (i, 0)),
        out_shape=jax.ShapeDtypeStruct((m, d), BF16),
        compiler_params=_params("arbitrary"),
        name="rmsnorm",
    )(x, g.reshape(1, d))


def _inproj_kernel(xn_ref, w_ref, *rest, tm, sample, n_alias):
    outs = rest[n_alias:]
    x = xn_ref[...]
    if sample:
        q_ref, kf_ref, vf_ref, uc_ref, up_ref = outs
        nb = tm // DEC_SEQ
    else:
        qt_ref, kb_ref, vt_ref, kf_ref, vf_ref, uc_ref, up_ref = outs
    pair = 2 * HEAD_DIM
    for hp in range(HEADS // 2):
        c0 = hp * pair
        q2 = _dot(x, w_ref[:, c0:c0 + pair]) * Q_SCALE
        k2 = _dot(x, w_ref[:, SB_WIDTH + c0:SB_WIDTH + c0 + pair])
        v2 = _dot(x, w_ref[:, 2 * SB_WIDTH + c0:2 * SB_WIDTH + c0 + pair])
        for u in range(2):
            h = 2 * hp + u
            sl = slice(u * HEAD_DIM, (u + 1) * HEAD_DIM)
            if sample:
                q_ref[:, h] = q2[:, sl].astype(BF16).reshape(nb, DEC_SEQ, HEAD_DIM)
                kf_ref[:, h] = k2[:, sl].reshape(nb, DEC_SEQ, HEAD_DIM)
                vf_ref[:, h] = v2[:, sl].reshape(nb, DEC_SEQ, HEAD_DIM)
            else:
                qt_ref[h] = q2[:, sl].T.astype(BF16)
                kb_ref[h] = k2[:, sl].astype(BF16)
                kf_ref[h] = k2[:, sl]
                vf_ref[h] = v2[:, sl]
                vt = v2[:, sl].T.astype(BF16)
                for c in range(tm // ATTN_VT):
                    vt_ref[h, c] = vt[:, c * ATTN_VT:(c + 1) * ATTN_VT]
    c0 = 3 * SB_WIDTH
    uc_ref[...] = _dot(x, w_ref[:, c0:c0 + 2 * CONV_WIDTH])
    up_ref[...] = _dot(x, w_ref[:, c0 + 2 * CONV_WIDTH:])


def kv_stack_shape(m, sample, depth):
    if sample:
        return (depth, m // DEC_SEQ, HEADS, DEC_SEQ, HEAD_DIM)
    return (depth, HEADS, m, HEAD_DIM)


def inproj_call(xn, w_in, sample, layer, kv_stack):
    depth = kv_stack[0].shape[0]
    m, d = xn.shape
    tm = min(ROW_TILE, m)
    if sample:
        nbt = m // DEC_SEQ
        nb = tm // DEC_SEQ
        hshape = (nbt, HEADS, DEC_SEQ, HEAD_DIM)
        hblock = (nb, HEADS, DEC_SEQ, HEAD_DIM)
        kv_shape = jax.ShapeDtypeStruct((depth,) + hshape, F32)
        kv_spec = pl.BlockSpec((None,) + hblock, lambda i: (layer, i, 0, 0, 0))
        out_shape = [jax.ShapeDtypeStruct(hshape, BF16), kv_shape, kv_shape]
        out_specs = [pl.BlockSpec(hblock, lambda i: (i, 0, 0, 0)), kv_spec, kv_spec]
        kv_index = (1, 2)
    else:
        nkb = m // ATTN_VT
        kshape = (HEADS, m, HEAD_DIM)
        kblock = (HEADS, tm, HEAD_DIM)
        kv_shape = jax.ShapeDtypeStruct((depth,) + kshape, F32)
        kv_spec = pl.BlockSpec((None,) + kblock, lambda i: (layer, 0, i, 0))
        out_shape = [jax.ShapeDtypeStruct((HEADS, HEAD_DIM, m), BF16),
                     jax.ShapeDtypeStruct(kshape, BF16),
                     jax.ShapeDtypeStruct((HEADS, nkb, HEAD_DIM, ATTN_VT), BF16),
                     kv_shape, kv_shape]
        out_specs = [pl.BlockSpec((HEADS, HEAD_DIM, tm), lambda i: (0, 0, i)),
                     pl.BlockSpec(kblock, lambda i: (0, i, 0)),
                     pl.BlockSpec((HEADS, tm // ATTN_VT, HEAD_DIM, ATTN_VT),
                                  lambda i: (0, i, 0, 0)),
                     kv_spec, kv_spec]
        kv_index = (3, 4)
    out_shape += [jax.ShapeDtypeStruct((m, 2 * CONV_WIDTH), F32),
                  jax.ShapeDtypeStruct((m, POOL_WIDTH), F32)]
    out_specs += [pl.BlockSpec((tm, 2 * CONV_WIDTH), lambda i: (i, 0)),
                  pl.BlockSpec((tm, POOL_WIDTH), lambda i: (i, 0))]
    in_specs = [pl.BlockSpec((tm, d), lambda i: (i, 0)),
                pl.BlockSpec((None, d, IN_COLS), lambda i: (layer, 0, 0),
                             pipeline_mode=pl.Buffered(1))]
    args = [xn, w_in]
    aliases = {}
    for n, stack in enumerate(kv_stack):
        assert stack.shape == kv_shape.shape and stack.dtype == kv_shape.dtype
        aliases[len(args)] = kv_index[n]
        in_specs.append(pl.BlockSpec(memory_space=pl.ANY))
        args.append(stack)
    outs = pl.pallas_call(
        functools.partial(_inproj_kernel, tm=tm, sample=sample, n_alias=len(aliases)),
        grid=(m // tm,),
        in_specs=in_specs,
        out_specs=out_specs,
        out_shape=out_shape,
        input_output_aliases=aliases,
        compiler_params=_params("arbitrary"),
        name="inproj_sample" if sample else "inproj_prompt",
    )(*args)
    kv = tuple(outs[n] for n in kv_index)
    return [o for n, o in enumerate(outs) if n not in kv_index], kv


MASKED_LOGIT = -1e30
STICK_EXHAUSTED = 160.0


def _attn_stage1(k_ref, qt, p, i, masked, a_ref, sp_ref, *, tq, tk):
    for half in (1, 0):
        row0 = pl.multiple_of(p * (2 * tk) + half * tk, tk)
        z = _dot(k_ref[pl.ds(row0, tk), :], qt)
        sp = _softplus2(z)
        a = z - sp
        if masked:
            s_idx = row0 + lax.broadcasted_iota(jnp.int32, (tk, tq), 0)
            t_idx = i * tq + lax.broadcasted_iota(jnp.int32, (tk, tq), 1)
            valid = s_idx < t_idx
            sp = jnp.where(valid, sp, 0.0)
            a = jnp.where(valid, a, MASKED_LOGIT)
        a_ref[half * tk:(half + 1) * tk, :] = a
        sp_ref[half] = sp.astype(BF16)


def _attn_stage2(tri_ref, a_ref, sp_ref, half, carry, *, tk):
    sp = sp_ref[half]
    later = _dot(tri_ref[...], sp)
    w = jnp.exp2(a_ref[half * tk:(half + 1) * tk, :] - later - carry)
    return w.astype(BF16), carry + later[0:1, :] + sp[0:1, :].astype(F32)


ATTN_HEADS_PER_STEP = 4
ATTN_STAGE_BUFS = 2
PV_LAG = 2


def _attn_prompt_kernel(qt_ref, k_ref, vt_ref, tri_ref, o_ref, *scratch, tq, tk):
    i = pl.program_id(1)
    nh = ATTN_HEADS_PER_STEP
    heads = range(nh)
    acc = scratch[0:nh]
    zb = [scratch[nh + b * nh:nh + (b + 1) * nh] for b in range(ATTN_STAGE_BUFS)]
    sp0 = nh * (1 + ATTN_STAGE_BUFS)
    spb = [scratch[sp0 + b * nh:sp0 + (b + 1) * nh] for b in range(ATTN_STAGE_BUFS)]
    s1 = functools.partial(_attn_stage1, tq=tq, tk=tk)
    s2 = functools.partial(_attn_stage2, tk=tk)
    qts = [qt_ref[h] for h in heads]
    for h in heads:
        acc[h][...] = jnp.zeros_like(acc[h])

    def second(buf, p, carry):
        carry = list(carry)
        units = [(h, half) for h in heads for half in (1, 0)]
        ws = {}
        for n in range(len(units) + PV_LAG):
            if n < len(units):
                h, half = units[n]
                ws[units[n]], carry[h] = s2(tri_ref, zb[buf][h], spb[buf][h], half, carry[h])
            if n >= PV_LAG:
                h, half = units[n - PV_LAG]
                vt = vt_ref[h, p, :, half * tk:(half + 1) * tk]
                acc[h][...] += _dot(vt, ws.pop(units[n - PV_LAG]))
        return tuple(carry)

    def step(buf, p, carry):
        for h in heads:
            s1(k_ref.at[h], qts[h], jnp.maximum(p - 1, 0), i, False,
               zb[1 - buf][h], spb[1 - buf][h])
        return second(buf, p, carry)

    for h in heads:
        s1(k_ref.at[h], qts[h], i, i, True, zb[0][h], spb[0][h])

    def live(state):
        m, _, least = state
        return jnp.logical_and(m < (i + 1) // 2, least <= STICK_EXHAUSTED)

    def body(state):
        m, carry, _ = state
        p = i - 2 * m
        carry = step(1, p - 1, step(0, p, carry))
        least = functools.reduce(jnp.minimum, [jnp.min(c) for c in carry])
        return m + 1, carry, least

    zero = tuple(jnp.zeros((1, tq), F32) for _ in heads)
    _, carry, least = lax.while_loop(live, body, (jnp.int32(0), zero, jnp.float32(0.0)))

    @pl.when(jnp.logical_and(i % 2 == 0, least <= STICK_EXHAUSTED))
    def _():
        second(0, 0, carry)

    d = acc[0].shape[0]
    for h in heads:
        o_ref[:, h * d:(h + 1) * d] = acc[h][...].T.astype(o_ref.dtype)


def _tri(tk):
    s = lax.broadcasted_iota(jnp.int32, (tk, tk), 0)
    j = lax.broadcasted_iota(jnp.int32, (tk, tk), 1)
    return (j > s).astype(BF16)


def attn_prompt_call(qt, kb, vt):
    h, d, t = qt.shape
    tq = ATTN_VT
    tk = ATTN_TK
    tri = _tri(tk)
    nh = ATTN_HEADS_PER_STEP
    nstage = nh * ATTN_STAGE_BUFS
    return pl.pallas_call(
        functools.partial(_attn_prompt_kernel, tq=tq, tk=tk),
        grid=(h // nh, t // tq),
        in_specs=[pl.BlockSpec((nh, d, tq), lambda hh, i: (hh, 0, i)),
                  pl.BlockSpec((nh, t, d), lambda hh, i: (hh, 0, 0)),
                  pl.BlockSpec((nh, t // ATTN_VT, d, ATTN_VT), lambda hh, i: (hh, 0, 0, 0)),
                  pl.BlockSpec((tk, tk), lambda hh, i: (0, 0))],
        out_specs=pl.BlockSpec((tq, nh * d), lambda hh, i: (i, hh)),
        out_shape=jax.ShapeDtypeStruct((t, h * d), BF16),
        scratch_shapes=([pltpu.VMEM((d, tq), F32)] * nh
                        + [pltpu.VMEM((2 * tk, tq), F32)] * nstage
                        + [pltpu.VMEM((2, tk, tq), BF16)] * nstage),
        compiler_params=_params("arbitrary", "arbitrary"),
        name="attn_prompt",
    )(qt, kb, vt, tri)


SAMPLE_HEADS_PER_STEP = 4


def _sample_sweep(q_ref, kb_ref, vb_ref, nblk, mask_last, carry, tri_ref, acc_ref,
                  z_ref, sp_ref, cum_ref, w_ref, *, tk):
    nt = (((1,), (1,)), ((), ()))
    nh, ts, d = q_ref.shape
    rows = nh * ts
    tot = nblk * tk
    for h in range(nh):
        z_ref[h * ts:(h + 1) * ts, 0:tot] = lax.dot_general(
            q_ref[h], kb_ref[h, 0:tot, :], nt, preferred_element_type=F32)
    for b in range(nblk):
        cols = slice(b * tk, (b + 1) * tk)
        z = z_ref[:, cols]
        sp = _softplus2(z)
        a = z - sp
        if mask_last and b == nblk - 1:
            t_idx = lax.broadcasted_iota(jnp.int32, (rows, tk), 0) % ts
            s_idx = lax.broadcasted_iota(jnp.int32, (rows, tk), 1)
            valid = s_idx < t_idx
            sp = jnp.where(valid, sp, 0.0)
            a = jnp.where(valid, a, MASKED_LOGIT)
        z_ref[:, cols] = a
        sp_ref[b * rows:(b + 1) * rows, :] = sp.astype(BF16)
    cum_ref[0:nblk * rows, :] = _dot(sp_ref[0:nblk * rows, :], tri_ref[...])
    for b in range(nblk - 1, -1, -1):
        cols = slice(b * tk, (b + 1) * tk)
        later = cum_ref[b * rows:(b + 1) * rows, :]
        w_ref[:, cols] = jnp.exp2(z_ref[:, cols] - later - carry).astype(BF16)
        first = sp_ref[b * rows:(b + 1) * rows, 0:1].astype(F32)
        carry = carry + later[:, 0:1] + first
    for h in range(nh):
        acc_ref[h * ts:(h + 1) * ts, :] += _dot(w_ref[h * ts:(h + 1) * ts, 0:tot],
                                                 vb_ref[h, 0:tot, :])
    return carry


def _attn_sample_kernel(q_ref, kn_ref, vn_ref, kt_ref, vt_ref, ck_hbm, cv_hbm, tri_ref, o_ref,
                        kb_ref, vb_ref, ckf_ref, cvf_ref, z_ref, sp_ref, cum_ref, w_ref,
                        acc_ref, sem, *, past, tk, layer):
    nh, ts, d = q_ref.shape
    rows = nh * ts
    early = past - tk
    sweep = functools.partial(_sample_sweep, q_ref, kb_ref, vb_ref, tri_ref=tri_ref,
                              acc_ref=acc_ref, z_ref=z_ref, sp_ref=sp_ref, cum_ref=cum_ref,
                              w_ref=w_ref, tk=tk)
    acc_ref[...] = jnp.zeros_like(acc_ref)
    for tail, new, dst in ((kt_ref, kn_ref, kb_ref), (vt_ref, vn_ref, vb_ref)):
        for h in range(nh):
            dst[h, 0:tk, :] = tail[h].astype(BF16)
            dst[h, tk:tk + ts, :] = new[h].astype(BF16)
            dst[h, tk + ts:2 * tk, :] = jnp.zeros((tk - ts, d), BF16)
    carry = sweep(2, True, jnp.zeros((rows, 1), F32))

    @pl.when(jnp.min(carry) <= STICK_EXHAUSTED)
    def _():
        b = pl.program_id(0)
        h0 = pl.multiple_of(pl.program_id(1) * nh, nh)
        copies = [pltpu.make_async_copy(
            src.at[layer, b, pl.ds(h0, nh), pl.ds(0, early), :], dst, sem.at[n])
            for n, (src, dst) in enumerate(((ck_hbm, ckf_ref), (cv_hbm, cvf_ref)))]
        for c in copies:
            c.start()
        for c in copies:
            c.wait()
        for h in range(nh):
            kb_ref[h, 0:early, :] = ckf_ref[h].astype(BF16)
            vb_ref[h, 0:early, :] = cvf_ref[h].astype(BF16)
        sweep(early // tk, False, carry)

    for h in range(nh):
        o_ref[:, h * d:(h + 1) * d] = acc_ref[h * ts:(h + 1) * ts, :].astype(o_ref.dtype)


def attn_sample_call(q, k_stack, v_stack, cache_k, cache_v, layer):
    b, h, ts, d = q.shape
    past = cache_k.shape[3]
    tk = SAMPLE_TK
    nh = SAMPLE_HEADS_PER_STEP
    assert past % tk == 0 and past >= 2 * tk and ts <= tk
    rows = nh * ts
    early = past - tk
    span = max(early, 2 * tk)
    tri = _tri(tk).T
    q_spec = pl.BlockSpec((None, nh, ts, d), lambda bb, hh: (bb, hh, 0, 0))
    new_spec = pl.BlockSpec((None, None, nh, ts, d), lambda bb, hh: (layer, bb, hh, 0, 0))
    tail_spec = pl.BlockSpec((None, None, nh, tk, d),
                             lambda bb, hh: (layer, bb, hh, past // tk - 1, 0))
    any_spec = pl.BlockSpec(memory_space=pl.ANY)
    return pl.pallas_call(
        functools.partial(_attn_sample_kernel, past=past, tk=tk, layer=layer),
        grid=(b, h // nh),
        in_specs=[q_spec, new_spec, new_spec, tail_spec, tail_spec, any_spec, any_spec,
                  pl.BlockSpec((tk, tk), lambda bb, hh: (0, 0))],
        out_specs=pl.BlockSpec((ts, nh * d), lambda bb, hh: (bb, hh)),
        out_shape=jax.ShapeDtypeStruct((b * ts, h * d), BF16),
        scratch_shapes=[pltpu.VMEM((nh, span, d), BF16), pltpu.VMEM((nh, span, d), BF16),
                        pltpu.VMEM((nh, early, d), F32), pltpu.VMEM((nh, early, d), F32),
                        pltpu.VMEM((rows, span), F32),
                        pltpu.VMEM((span // tk * rows, tk), BF16),
                        pltpu.VMEM((span // tk * rows, tk), F32),
                        pltpu.VMEM((rows, span), BF16),
                        pltpu.VMEM((rows, d), F32),
                        pltpu.SemaphoreType.DMA((2,))],
        compiler_params=_params("arbitrary", "arbitrary"),
        name="attn_sample",
    )(q, k_stack, v_stack, cache_k, cache_v, cache_k, cache_v, tri)


def _mixers_kernel(uc_ref, up_ref, hc_ref, hp_ref, dww_ref, dwb_ref, ng_ref, nb_ref,
                   pww_ref, pwb_ref, plw_ref, pls_ref,
                   oc_ref, op_ref, cs_ref, ps_ref,
                   ext_ref, sh_ref, yn_ref, pext_ref, pooled_ref,
                   *, tm, carry, start_pos):
    i = pl.program_id(0)
    hdr = HIST_ROWS

    def load_hist():
        ext_ref[0:hdr, :] = hc_ref[...]
        pext_ref[0:hdr, :] = hp_ref[...]

    if carry:
        pl.when(i == 0)(load_hist)
    else:
        load_hist()

    ext_ref[hdr:hdr + tm, :] = uc_ref[:, 0:CONV_WIDTH] * jax.nn.sigmoid(uc_ref[:, CONV_WIDTH:])
    pext_ref[hdr:hdr + tm, :] = up_ref[...]
    cs_ref[...] = ext_ref[pl.ds(hdr + tm - (CONV_KERNEL - 1), CONV_KERNEL - 1), :]
    ps_ref[...] = pext_ref[pl.ds(hdr + tm - POOL_HIST, POOL_HIST), :]

    sh_rows = sh_ref.shape[1]
    for r in range(1, 8):
        sh_ref[r - 1] = ext_ref[pl.ds(r, sh_rows), :]

    rc = min(CONV_CHUNK, tm)
    first = hdr - (CONV_KERNEL - 1)

    def conv_chunk(c, _):
        base = pl.multiple_of(c * rc, rc)
        acc = jnp.zeros((rc, CONV_WIDTH), F32)
        for r in range(8):
            taps = [(j, ((first + j) // 8) * 8) for j in range(CONV_KERNEL)
                    if (first + j) % 8 == r]
            lo = min(off for _, off in taps)
            hi = max(off for _, off in taps)
            src = ext_ref if r == 0 else sh_ref.at[r - 1]
            slab = src[pl.ds(base + lo, rc + hi - lo), :]
            for j, off in taps:
                acc = acc + slab[off - lo:off - lo + rc] * dww_ref[j:j + 1, :]
        y = acc + dwb_ref[...]
        gw = CONV_WIDTH // CONV_GROUPS
        for g in range(CONV_GROUPS):
            sl = slice(g * gw, (g + 1) * gw)
            yg = y[:, sl]
            mu = jnp.mean(yg, axis=-1, keepdims=True)
            dv = yg - mu
            var = jnp.mean(dv * dv, axis=-1, keepdims=True)
            yn = dv * lax.rsqrt(var + EPS) * ng_ref[:, sl] + nb_ref[:, sl]
            yn = yn * jax.nn.sigmoid(yn)
            yn_ref[pl.ds(base, rc), sl] = yn.astype(BF16)
        return 0

    lax.fori_loop(0, tm // rc, conv_chunk, 0)
    oc_ref[...] = (_dot(yn_ref[...], pww_ref[...]) + pwb_ref[...]).astype(oc_ref.dtype)

    pc = min(LANE, tm)
    for c in range(tm // pc):
        b0 = hdr + c * pc
        row = lax.broadcasted_iota(jnp.int32, (pc, POOL_GROUP), 0) + c * pc
        pos = row + (start_pos + (i * tm if carry else 0))
        for g, w in enumerate(POOL_WINDOWS):
            sl = slice(g * POOL_GROUP, (g + 1) * POOL_GROUP)
            u = pext_ref[b0:b0 + pc, sl]
            s = u
            for dlt in range(1, w):
                s = s + pext_ref[pl.ds(b0 - dlt, pc), sl]
            cnt = jnp.minimum(w, pos + 1).astype(F32)
            pooled_ref[c * pc:(c + 1) * pc, sl] = (s / cnt - u).astype(BF16)
    op_ref[...] = (_dot(pooled_ref[...], plw_ref[...]) * pls_ref[...]).astype(op_ref.dtype)

    if carry:
        ext_ref[0:hdr, :] = ext_ref[tm:tm + hdr, :]
        pext_ref[0:hdr, :] = pext_ref[tm:tm + hdr, :]


def mixers_call(uc, up, hist_c, hist_p, lw, carry, start_pos):
    m = uc.shape[0]
    nseq = hist_c.shape[0]
    tm = min(ROW_TILE, m) if carry else m // nseq
    row = lambda i: (i, 0)
    const = lambda i: (0, 0)
    seq = (lambda i: (0, 0, 0)) if carry else (lambda i: (i, 0, 0))
    vec = pl.BlockSpec((1, CONV_WIDTH), const)
    sq = pl.BlockSpec((CONV_WIDTH, CONV_WIDTH), const)
    return pl.pallas_call(
        functools.partial(_mixers_kernel, tm=tm, carry=carry, start_pos=start_pos),
        grid=(m // tm,),
        in_specs=[pl.BlockSpec((tm, 2 * CONV_WIDTH), row),
                  pl.BlockSpec((tm, POOL_WIDTH), row),
                  pl.BlockSpec((None, HIST_ROWS, CONV_WIDTH), seq),
                  pl.BlockSpec((None, HIST_ROWS, POOL_WIDTH), seq),
                  pl.BlockSpec((HIST_ROWS, CONV_WIDTH), const),
                  vec, vec, vec, sq, vec, sq, vec],
        out_specs=[pl.BlockSpec((tm, CONV_WIDTH), row),
                   pl.BlockSpec((tm, POOL_WIDTH), row),
                   pl.BlockSpec((None, CONV_KERNEL - 1, CONV_WIDTH), seq),
                   pl.BlockSpec((None, POOL_HIST, POOL_WIDTH), seq)],
        out_shape=[jax.ShapeDtypeStruct((m, CONV_WIDTH), BF16),
                   jax.ShapeDtypeStruct((m, POOL_WIDTH), BF16),
                   jax.ShapeDtypeStruct((nseq, CONV_KERNEL - 1, CONV_WIDTH), F32),
                   jax.ShapeDtypeStruct((nseq, POOL_HIST, POOL_WIDTH), F32)],
        scratch_shapes=[pltpu.VMEM((HIST_ROWS + tm + 8, CONV_WIDTH), F32),
                        pltpu.VMEM((7, tm + HIST_ROWS - 8, CONV_WIDTH), F32),
                        pltpu.VMEM((tm, CONV_WIDTH), BF16),
                        pltpu.VMEM((HIST_ROWS + tm, POOL_WIDTH), F32),
                        pltpu.VMEM((tm, POOL_WIDTH), BF16)],
        compiler_params=_params("arbitrary"),
        name="mixers_prompt" if carry else "mixers_sample",
    )(uc, up, hist_c, hist_p, lw["dw_w"], lw["dw_b"], lw["n_g"], lw["n_b"],
      lw["pw_w"], lw["pw_b"], lw["pool_w"], lw["pool_s"])


def _outproj_kernel(h_ref, osb_ref, oc_ref, op_ref, w_ref, g_ref, hn_ref, xn_ref):
    c1 = SB_WIDTH
    c2 = SB_WIDTH + CONV_WIDTH
    mix = (_dot(osb_ref[...], w_ref[0:c1, :]) + _dot(oc_ref[...], w_ref[c1:c2, :])
           + _dot(op_ref[...], w_ref[c2:, :]))
    h = h_ref[...] + mix
    hn_ref[...] = h
    xn_ref[...] = _rmsnorm_rows(h, g_ref[...]).astype(xn_ref.dtype)


def outproj_call(h, osb, oc, op, w_out, g, layer):
    m, d = h.shape
    tm = min(ROW_TILE, m)
    row = lambda i: (i, 0)
    return pl.pallas_call(
        _outproj_kernel,
        grid=(m // tm,),
        in_specs=[pl.BlockSpec((tm, d), row),
                  pl.BlockSpec((tm, SB_WIDTH), row),
                  pl.BlockSpec((tm, CONV_WIDTH), row),
                  pl.BlockSpec((tm, POOL_WIDTH), row),
                  pl.BlockSpec((None, d, d), lambda i: (layer, 0, 0),
                               pipeline_mode=pl.Buffered(1)),
                  pl.BlockSpec((1, d), lambda i: (0, 0))],
        out_specs=[pl.BlockSpec((tm, d), row), pl.BlockSpec((tm, d), row)],
        out_shape=[jax.ShapeDtypeStruct((m, d), F32), jax.ShapeDtypeStruct((m, d), BF16)],
        compiler_params=_params("arbitrary"),
        name="outproj",
    )(h, osb, oc, op, w_out, g.reshape(1, d))


def _mlp_kernel(xn_ref, h_ref, wu_ref, wd_ref, g_ref, *rest, final):
    if final:
        y_ref, acc_ref = rest
    else:
        hn_ref, y_ref, acc_ref = rest
    f = pl.program_id(1)

    @pl.when(f == 0)
    def _():
        acc_ref[...] = jnp.zeros_like(acc_ref)

    a = jnp.maximum(_dot(xn_ref[...], wu_ref[...]), 0.0)
    acc_ref[...] += _dot((a * a).astype(BF16), wd_ref[...])

    @pl.when(f == pl.num_programs(1) - 1)
    def _():
        h = h_ref[...] + acc_ref[...]
        if not final:
            hn_ref[...] = h
        y_ref[...] = _rmsnorm_rows(h, g_ref[...]).astype(y_ref.dtype)


def mlp_call(xn, h, w_up, w_down, g, final, layer):
    m, d = h.shape
    ff = w_up.shape[2]
    tm = min(ROW_TILE, m)
    tf = FF_TILE
    row = lambda i, f: (i, 0)
    y_dtype = F32 if final else BF16
    out_specs = [pl.BlockSpec((tm, d), row)]
    out_shape = [jax.ShapeDtypeStruct((m, d), y_dtype)]
    if not final:
        out_specs = [pl.BlockSpec((tm, d), row)] + out_specs
        out_shape = [jax.ShapeDtypeStruct((m, d), F32)] + out_shape
    return pl.pallas_call(
        functools.partial(_mlp_kernel, final=final),
        grid=(m // tm, ff // tf),
        in_specs=[pl.BlockSpec((tm, d), row),
                  pl.BlockSpec((tm, d), row),
                  pl.BlockSpec((None, d, tf), lambda i, f: (layer, 0, f)),
                  pl.BlockSpec((None, tf, d), lambda i, f: (layer, f, 0)),
                  pl.BlockSpec((1, d), lambda i, f: (0, 0))],
        out_specs=out_specs,
        out_shape=out_shape,
        scratch_shapes=[pltpu.VMEM((tm, d), F32)],
        compiler_params=_params("arbitrary", "arbitrary"),
        name="mlp_final" if final else "mlp",
    )(xn, h, w_up, w_down, g.reshape(1, d))


def _pad_hist(hist, rows):
    return jnp.pad(hist, ((0, 0), (rows - hist.shape[1], 0), (0, 0)))


def _block_diag(w):
    g, a, b = w.shape
    eye = jnp.eye(g, dtype=w.dtype)
    return (eye[:, None, :, None] * w[:, :, None, :]).reshape(g * a, g * b)


def _group_trunk(x, layers, big, final_g, sample, caches):
    depth = len(layers)
    h = x
    xn = rmsnorm_call(x, layers[0]["norm_mix_g"])
    cs, ps = [], []
    y = None
    shape = kv_stack_shape(x.shape[0], sample, depth)
    kv = (jnp.zeros(shape, F32), jnp.zeros(shape, F32))
    for l, lw in enumerate(layers):
        if sample:
            cache_k, cache_v, cache_conv, state_pool = caches
            (q, uc, up), kv = inproj_call(xn, big["w_in"], True, l, kv)
            osb = attn_sample_call(q, kv[0], kv[1], cache_k, cache_v, l)
            hist_c = _pad_hist(cache_conv[l], HIST_ROWS)
            hist_p = _pad_hist(state_pool[l], HIST_ROWS)
            start = cache_k.shape[3]
        else:
            (qt, kb, vt, uc, up), kv = inproj_call(xn, big["w_in"], False, l, kv)
            osb = attn_prompt_call(qt, kb, vt)
            hist_c = jnp.zeros((1, HIST_ROWS, CONV_WIDTH), F32)
            hist_p = jnp.zeros((1, HIST_ROWS, POOL_WIDTH), F32)
            start = 0
        oc, op, c_state, p_state = mixers_call(uc, up, hist_c, hist_p, lw, not sample, start)
        h, xn2 = outproj_call(h, osb, oc, op, big["w_out"], lw["norm_mlp_g"], l)
        if l + 1 < depth:
            h, xn = mlp_call(xn2, h, big["w_up"], big["w_down"], layers[l + 1]["norm_mix_g"],
                             False, l)
        else:
            (y,) = mlp_call(xn2, h, big["w_up"], big["w_down"], final_g, True, l)
        cs.append(c_state)
        ps.append(p_state)
    return y, kv[0], kv[1], jnp.stack(cs), jnp.stack(ps)


def kernel(x_prompt, x_sample, cache_k, cache_v, cache_conv, state_pool, norm_mix_g, w_in,
           conv_dw_w, conv_dw_b, conv_norm_g, conv_norm_b, conv_pw_w, conv_pw_b, pool_w,
           pool_scale, w_out, norm_mlp_g, w_up, w_down, final_norm_g):
    depth = w_in.shape[0]
    bp, seq, d = x_prompt.shape
    bs, dseq, _ = x_sample.shape
    assert bp == 1 and dseq == DEC_SEQ and d == D_MODEL
    layers = []
    for l in range(depth):
        layers.append(dict(
            norm_mix_g=norm_mix_g[l],
            dw_w=jnp.pad(conv_dw_w[l], ((0, HIST_ROWS - CONV_KERNEL), (0, 0))),
            dw_b=conv_dw_b[l].reshape(1, -1),
            n_g=conv_norm_g[l].reshape(1, -1),
            n_b=conv_norm_b[l].reshape(1, -1),
            pw_w=conv_pw_w[l].astype(BF16),
            pw_b=conv_pw_b[l].reshape(1, -1),
            pool_w=_block_diag(pool_w[l]).astype(BF16),
            pool_s=pool_scale[l].reshape(1, -1),
            norm_mlp_g=norm_mlp_g[l],
        ))
    big = dict(w_in=w_in.astype(BF16), w_out=w_out.astype(BF16),
               w_up=w_up.astype(BF16), w_down=w_down.astype(BF16))
    yp, kp, vp, cp, pp = _group_trunk(x_prompt.reshape(seq, d), layers, big, final_norm_g,
                                      False, None)
    ys, ks, vs, cs, ps = _group_trunk(x_sample.reshape(bs * dseq, d), layers, big, final_norm_g, True,
                                      (cache_k, cache_v, cache_conv, state_pool))
    return (yp.reshape(bp, seq, d), ys.reshape(bs, dseq, d),
            kp[:, None], vp[:, None], cp, pp, ks, vs, cs, ps)
```

```python
import functools
import math

import jax
import jax.numpy as jnp
from jax import lax
from jax.experimental import pallas as pl
from jax.experimental.pallas import tpu as pltpu

F32 = jnp.float32
BF16 = jnp.bfloat16

D_MODEL = 2048
SB_WIDTH = 1024
HEAD_DIM = 128
HEADS = SB_WIDTH // HEAD_DIM
CONV_WIDTH = 512
CONV_KERNEL = 31
CONV_GROUPS = 4
POOL_WIDTH = 512
POOL_WINDOWS = (2, 4, 8, 16)
POOL_GROUP = POOL_WIDTH // len(POOL_WINDOWS)
POOL_HIST = max(POOL_WINDOWS) - 1
D_FF = 4 * D_MODEL
IN_COLS = 3 * SB_WIDTH + 2 * CONV_WIDTH + POOL_WIDTH
EPS = 1e-6
DEC_SEQ = 64

LOG2E = 1.4426950408889634
Q_SCALE = HEAD_DIM ** -0.5 * LOG2E

LANE = 128
HIST_ROWS = 32
ROW_TILE = 512
FF_TILE = 1024
ATTN_TK = 128
ATTN_VT = 2 * ATTN_TK
SAMPLE_TK = 256
CONV_CHUNK = 64
VMEM_LIMIT = 56 * 1024 * 1024


def _params(*sem):
    return pltpu.CompilerParams(dimension_semantics=sem, vmem_limit_bytes=VMEM_LIMIT)


def _dot(a, b):
    return jnp.dot(a, b, preferred_element_type=F32)


def _softplus2(z):
    e = jnp.exp2(-jnp.abs(z))
    return jnp.maximum(z, 0.0) + jnp.log(1.0 + e) * LOG2E


def _rmsnorm_rows(x, g):
    ms = jnp.mean(x * x, axis=-1, keepdims=True)
    return x * lax.rsqrt(ms + EPS) * g


def _inproj_kernel(h_ref, g_ref, w_ref, *rest, tm, sample, n_alias):
    outs = rest[n_alias:]
    x = _rmsnorm_rows(h_ref[...], g_ref[...]).astype(BF16)
    if sample:
        q_ref, kf_ref, vf_ref, glu_ref, up_ref = outs
        nb = tm // DEC_SEQ
    else:
        qt_ref, kb_ref, vt_ref, kf_ref, vf_ref, glu_ref, up_ref = outs
    pair = 2 * HEAD_DIM
    for hp in range(HEADS // 2):
        c0 = hp * pair
        q2 = _dot(x, w_ref[:, c0:c0 + pair]) * Q_SCALE
        k2 = _dot(x, w_ref[:, SB_WIDTH + c0:SB_WIDTH + c0 + pair])
        v2 = _dot(x, w_ref[:, 2 * SB_WIDTH + c0:2 * SB_WIDTH + c0 + pair])
        for u in range(2):
            h = 2 * hp + u
            sl = slice(u * HEAD_DIM, (u + 1) * HEAD_DIM)
            if sample:
                q_ref[:, h] = q2[:, sl].astype(BF16).reshape(nb, DEC_SEQ, HEAD_DIM)
                kf_ref[:, h] = k2[:, sl].reshape(nb, DEC_SEQ, HEAD_DIM)
                vf_ref[:, h] = v2[:, sl].reshape(nb, DEC_SEQ, HEAD_DIM)
            else:
                qt_ref[h] = q2[:, sl].T.astype(BF16)
                kb_ref[h] = k2[:, sl].astype(BF16)
                kf_ref[h] = k2[:, sl]
                vf_ref[h] = v2[:, sl]
                vt = v2[:, sl].T.astype(BF16)
                for c in range(tm // ATTN_VT):
                    vt_ref[h, c] = vt[:, c * ATTN_VT:(c + 1) * ATTN_VT]
    c0 = 3 * SB_WIDTH
    uc = _dot(x, w_ref[:, c0:c0 + 2 * CONV_WIDTH])
    glu_ref[...] = uc[:, 0:CONV_WIDTH] * jax.nn.sigmoid(uc[:, CONV_WIDTH:])
    up_ref[...] = _dot(x, w_ref[:, c0 + 2 * CONV_WIDTH:])


def kv_stack_shape(m, sample, depth):
    if sample:
        return (depth, m // DEC_SEQ, HEADS, DEC_SEQ, HEAD_DIM)
    return (depth, HEADS, m, HEAD_DIM)


def inproj_call(h, g, w_in, sample, layer, kv_stack):
    depth = kv_stack[0].shape[0]
    m, d = h.shape
    tm = min(ROW_TILE, m)
    if sample:
        nbt = m // DEC_SEQ
        nb = tm // DEC_SEQ
        hshape = (nbt, HEADS, DEC_SEQ, HEAD_DIM)
        hblock = (nb, HEADS, DEC_SEQ, HEAD_DIM)
        kv_shape = jax.ShapeDtypeStruct((depth,) + hshape, F32)
        kv_spec = pl.BlockSpec((None,) + hblock, lambda i: (layer, i, 0, 0, 0))
        out_shape = [jax.ShapeDtypeStruct(hshape, BF16), kv_shape, kv_shape]
        out_specs = [pl.BlockSpec(hblock, lambda i: (i, 0, 0, 0)), kv_spec, kv_spec]
        kv_index = (1, 2)
    else:
        nkb = m // ATTN_VT
        kshape = (HEADS, m, HEAD_DIM)
        kblock = (HEADS, tm, HEAD_DIM)
        kv_shape = jax.ShapeDtypeStruct((depth,) + kshape, F32)
        kv_spec = pl.BlockSpec((None,) + kblock, lambda i: (layer, 0, i, 0))
        out_shape = [jax.ShapeDtypeStruct((HEADS, HEAD_DIM, m), BF16),
                     jax.ShapeDtypeStruct(kshape, BF16),
                     jax.ShapeDtypeStruct((HEADS, nkb, HEAD_DIM, ATTN_VT), BF16),
                     kv_shape, kv_shape]
        out_specs = [pl.BlockSpec((HEADS, HEAD_DIM, tm), lambda i: (0, 0, i)),
                     pl.BlockSpec(kblock, lambda i: (0, i, 0)),
                     pl.BlockSpec((HEADS, tm // ATTN_VT, HEAD_DIM, ATTN_VT),
                                  lambda i: (0, i, 0, 0)),
                     kv_spec, kv_spec]
        kv_index = (3, 4)
    out_shape += [jax.ShapeDtypeStruct((m, CONV_WIDTH), F32),
                  jax.ShapeDtypeStruct((m, POOL_WIDTH), F32)]
    out_specs += [pl.BlockSpec((tm, CONV_WIDTH), lambda i: (i, 0)),
                  pl.BlockSpec((tm, POOL_WIDTH), lambda i: (i, 0))]
    in_specs = [pl.BlockSpec((tm, d), lambda i: (i, 0)),
                pl.BlockSpec((1, d), lambda i: (0, 0)),
                pl.BlockSpec((None, d, IN_COLS), lambda i: (layer, 0, 0),
                             pipeline_mode=pl.Buffered(1))]
    args = [h, g.reshape(1, d), w_in]
    aliases = {}
    for n, stack in enumerate(kv_stack):
        assert stack.shape == kv_shape.shape and stack.dtype == kv_shape.dtype
        aliases[len(args)] = kv_index[n]
        in_specs.append(pl.BlockSpec(memory_space=pl.ANY))
        args.append(stack)
    outs = pl.pallas_call(
        functools.partial(_inproj_kernel, tm=tm, sample=sample, n_alias=len(aliases)),
        grid=(m // tm,),
        in_specs=in_specs,
        out_specs=out_specs,
        out_shape=out_shape,
        input_output_aliases=aliases,
        compiler_params=_params("arbitrary"),
        name="inproj_sample" if sample else "inproj_prompt",
    )(*args)
    kv = tuple(outs[n] for n in kv_index)
    return [o for n, o in enumerate(outs) if n not in kv_index], kv


MASKED_LOGIT = -1e30
STICK_EXHAUSTED = 160.0


def _attn_stage1(k_ref, qt, p, i, masked, a_ref, sp_ref, *, tq, tk):
    for half in (1, 0):
        row0 = pl.multiple_of(p * (2 * tk) + half * tk, tk)
        z = _dot(k_ref[pl.ds(row0, tk), :], qt)
        sp = _softplus2(z)
        a = z - sp
        if masked:
            s_idx = row0 + lax.broadcasted_iota(jnp.int32, (tk, tq), 0)
            t_idx = i * tq + lax.broadcasted_iota(jnp.int32, (tk, tq), 1)
            valid = s_idx < t_idx
            sp = jnp.where(valid, sp, 0.0)
            a = jnp.where(valid, a, MASKED_LOGIT)
        a_ref[half * tk:(half + 1) * tk, :] = a
        sp_ref[half] = sp.astype(BF16)


def _attn_stage2(tri_ref, a_ref, sp_ref, half, carry, *, tk):
    sp = sp_ref[half]
    later = _dot(tri_ref[...], sp)
    w = jnp.exp2(a_ref[half * tk:(half + 1) * tk, :] - later - carry)
    return w.astype(BF16), carry + later[0:1, :] + sp[0:1, :].astype(F32)


ATTN_HEADS_PER_STEP = 4
ATTN_STAGE_BUFS = 2
PV_LAG = 2


def _attn_prompt_kernel(qt_ref, k_ref, vt_ref, tri_ref, o_ref, *scratch, tq, tk):
    i = pl.program_id(1)
    nh = ATTN_HEADS_PER_STEP
    heads = range(nh)
    acc = scratch[0:nh]
    zb = [scratch[nh + b * nh:nh + (b + 1) * nh] for b in range(ATTN_STAGE_BUFS)]
    sp0 = nh * (1 + ATTN_STAGE_BUFS)
    spb = [scratch[sp0 + b * nh:sp0 + (b + 1) * nh] for b in range(ATTN_STAGE_BUFS)]
    s1 = functools.partial(_attn_stage1, tq=tq, tk=tk)
    s2 = functools.partial(_attn_stage2, tk=tk)
    qts = [qt_ref[h] for h in heads]
    for h in heads:
        acc[h][...] = jnp.zeros_like(acc[h])

    def second(buf, p, carry):
        carry = list(carry)
        units = [(h, half) for h in heads for half in (1, 0)]
        ws = {}
        for n in range(len(units) + PV_LAG):
            if n < len(units):
                h, half = units[n]
                ws[units[n]], carry[h] = s2(tri_ref, zb[buf][h], spb[buf][h], half, carry[h])
            if n >= PV_LAG:
                h, half = units[n - PV_LAG]
                vt = vt_ref[h, p, :, half * tk:(half + 1) * tk]
                acc[h][...] += _dot(vt, ws.pop(units[n - PV_LAG]))
        return tuple(carry)

    def step(buf, p, carry):
        for h in heads:
            s1(k_ref.at[h], qts[h], jnp.maximum(p - 1, 0), i, False,
               zb[1 - buf][h], spb[1 - buf][h])
        return second(buf, p, carry)

    for h in heads:
        s1(k_ref.at[h], qts[h], i, i, True, zb[0][h], spb[0][h])

    def live(state):
        m, _, least = state
        return jnp.logical_and(m < (i + 1) // 2, least <= STICK_EXHAUSTED)

    def body(state):
        m, carry, _ = state
        p = i - 2 * m
        carry = step(1, p - 1, step(0, p, carry))
        least = functools.reduce(jnp.minimum, [jnp.min(c) for c in carry])
        return m + 1, carry, least

    zero = tuple(jnp.zeros((1, tq), F32) for _ in heads)
    _, carry, least = lax.while_loop(live, body, (jnp.int32(0), zero, jnp.float32(0.0)))

    @pl.when(jnp.logical_and(i % 2 == 0, least <= STICK_EXHAUSTED))
    def _():
        second(0, 0, carry)

    d = acc[0].shape[0]
    for h in heads:
        o_ref[:, h * d:(h + 1) * d] = acc[h][...].T.astype(o_ref.dtype)


def _tri(tk):
    s = lax.broadcasted_iota(jnp.int32, (tk, tk), 0)
    j = lax.broadcasted_iota(jnp.int32, (tk, tk), 1)
    return (j > s).astype(BF16)


def attn_prompt_call(qt, kb, vt):
    h, d, t = qt.shape
    tq = ATTN_VT
    tk = ATTN_TK
    tri = _tri(tk)
    nh = ATTN_HEADS_PER_STEP
    nstage = nh * ATTN_STAGE_BUFS
    return pl.pallas_call(
        functools.partial(_attn_prompt_kernel, tq=tq, tk=tk),
        grid=(h // nh, t // tq),
        in_specs=[pl.BlockSpec((nh, d, tq), lambda hh, i: (hh, 0, i)),
                  pl.BlockSpec((nh, t, d), lambda hh, i: (hh, 0, 0)),
                  pl.BlockSpec((nh, t // ATTN_VT, d, ATTN_VT), lambda hh, i: (hh, 0, 0, 0)),
                  pl.BlockSpec((tk, tk), lambda hh, i: (0, 0))],
        out_specs=pl.BlockSpec((tq, nh * d), lambda hh, i: (i, hh)),
        out_shape=jax.ShapeDtypeStruct((t, h * d), BF16),
        scratch_shapes=([pltpu.VMEM((d, tq), F32)] * nh
                        + [pltpu.VMEM((2 * tk, tq), F32)] * nstage
                        + [pltpu.VMEM((2, tk, tq), BF16)] * nstage),
        compiler_params=_params("arbitrary", "arbitrary"),
        name="attn_prompt",
    )(qt, kb, vt, tri)


SAMPLE_HEADS_PER_STEP = 4


def _sample_sweep(q_ref, kb_ref, vb_ref, nblk, mask_last, carry, tri_ref, acc_ref,
                  z_ref, sp_ref, cum_ref, w_ref, *, tk):
    nt = (((1,), (1,)), ((), ()))
    nh, ts, d = q_ref.shape
    rows = nh * ts
    tot = nblk * tk
    for h in range(nh):
        z_ref[h * ts:(h + 1) * ts, 0:tot] = lax.dot_general(
            q_ref[h], kb_ref[h, 0:tot, :], nt, preferred_element_type=F32)
    for b in range(nblk):
        cols = slice(b * tk, (b + 1) * tk)
        z = z_ref[:, cols]
        sp = _softplus2(z)
        a = z - sp
        if mask_last and b == nblk - 1:
            t_idx = lax.broadcasted_iota(jnp.int32, (rows, tk), 0) % ts
            s_idx = lax.broadcasted_iota(jnp.int32, (rows, tk), 1)
            valid = s_idx < t_idx
            sp = jnp.where(valid, sp, 0.0)
            a = jnp.where(valid, a, MASKED_LOGIT)
        z_ref[:, cols] = a
        sp_ref[b * rows:(b + 1) * rows, :] = sp.astype(BF16)
    cum_ref[0:nblk * rows, :] = _dot(sp_ref[0:nblk * rows, :], tri_ref[...])
    for b in range(nblk - 1, -1, -1):
        cols = slice(b * tk, (b + 1) * tk)
        later = cum_ref[b * rows:(b + 1) * rows, :]
        w_ref[:, cols] = jnp.exp2(z_ref[:, cols] - later - carry).astype(BF16)
        first = sp_ref[b * rows:(b + 1) * rows, 0:1].astype(F32)
        carry = carry + later[:, 0:1] + first
    for h in range(nh):
        acc_ref[h * ts:(h + 1) * ts, :] += _dot(w_ref[h * ts:(h + 1) * ts, 0:tot],
                                                 vb_ref[h, 0:tot, :])
    return carry


def _attn_sample_kernel(q_ref, kn_ref, vn_ref, kt_ref, vt_ref, ck_hbm, cv_hbm, tri_ref, o_ref,
                        kb_ref, vb_ref, ckf_ref, cvf_ref, z_ref, sp_ref, cum_ref, w_ref,
                        acc_ref, sem, *, past, tk, layer):
    nh, ts, d = q_ref.shape
    rows = nh * ts
    early = past - tk
    sweep = functools.partial(_sample_sweep, q_ref, kb_ref, vb_ref, tri_ref=tri_ref,
                              acc_ref=acc_ref, z_ref=z_ref, sp_ref=sp_ref, cum_ref=cum_ref,
                              w_ref=w_ref, tk=tk)
    acc_ref[...] = jnp.zeros_like(acc_ref)
    for tail, new, dst in ((kt_ref, kn_ref, kb_ref), (vt_ref, vn_ref, vb_ref)):
        for h in range(nh):
            dst[h, 0:tk, :] = tail[h].astype(BF16)
            dst[h, tk:tk + ts, :] = new[h].astype(BF16)
            dst[h, tk + ts:2 * tk, :] = jnp.zeros((tk - ts, d), BF16)
    carry = sweep(2, True, jnp.zeros((rows, 1), F32))

    @pl.when(jnp.min(carry) <= STICK_EXHAUSTED)
    def _():
        b = pl.program_id(0)
        h0 = pl.multiple_of(pl.program_id(1) * nh, nh)
        copies = [pltpu.make_async_copy(
            src.at[layer, b, pl.ds(h0, nh), pl.ds(0, early), :], dst, sem.at[n])
            for n, (src, dst) in enumerate(((ck_hbm, ckf_ref), (cv_hbm, cvf_ref)))]
        for c in copies:
            c.start()
        for c in copies:
            c.wait()
        for h in range(nh):
            kb_ref[h, 0:early, :] = ckf_ref[h].astype(BF16)
            vb_ref[h, 0:early, :] = cvf_ref[h].astype(BF16)
        sweep(early // tk, False, carry)

    for h in range(nh):
        o_ref[:, h * d:(h + 1) * d] = acc_ref[h * ts:(h + 1) * ts, :].astype(o_ref.dtype)


def attn_sample_call(q, k_stack, v_stack, cache_k, cache_v, layer):
    b, h, ts, d = q.shape
    past = cache_k.shape[3]
    tk = SAMPLE_TK
    nh = SAMPLE_HEADS_PER_STEP
    assert past % tk == 0 and past >= 2 * tk and ts <= tk
    rows = nh * ts
    early = past - tk
    span = max(early, 2 * tk)
    tri = _tri(tk).T
    q_spec = pl.BlockSpec((None, nh, ts, d), lambda bb, hh: (bb, hh, 0, 0))
    new_spec = pl.BlockSpec((None, None, nh, ts, d), lambda bb, hh: (layer, bb, hh, 0, 0))
    tail_spec = pl.BlockSpec((None, None, nh, tk, d),
                             lambda bb, hh: (layer, bb, hh, past // tk - 1, 0))
    any_spec = pl.BlockSpec(memory_space=pl.ANY)
    return pl.pallas_call(
        functools.partial(_attn_sample_kernel, past=past, tk=tk, layer=layer),
        grid=(b, h // nh),
        in_specs=[q_spec, new_spec, new_spec, tail_spec, tail_spec, any_spec, any_spec,
                  pl.BlockSpec((tk, tk), lambda bb, hh: (0, 0))],
        out_specs=pl.BlockSpec((ts, nh * d), lambda bb, hh: (bb, hh)),
        out_shape=jax.ShapeDtypeStruct((b * ts, h * d), BF16),
        scratch_shapes=[pltpu.VMEM((nh, span, d), BF16), pltpu.VMEM((nh, span, d), BF16),
                        pltpu.VMEM((nh, early, d), F32), pltpu.VMEM((nh, early, d), F32),
                        pltpu.VMEM((rows, span), F32),
                        pltpu.VMEM((span // tk * rows, tk), BF16),
                        pltpu.VMEM((span // tk * rows, tk), F32),
                        pltpu.VMEM((rows, span), BF16),
                        pltpu.VMEM((rows, d), F32),
                        pltpu.SemaphoreType.DMA((2,))],
        compiler_params=_params("arbitrary", "arbitrary"),
        name="attn_sample",
    )(q, k_stack, v_stack, cache_k, cache_v, cache_k, cache_v, tri)


def _mixers_kernel(glu_ref, up_ref, hc_ref, hp_ref, dww_ref, dwb_ref, ng_ref, nb_ref,
                   pww_ref, pwb_ref, plw_ref, pls_ref,
                   oc_ref, op_ref, cs_ref, ps_ref,
                   ext_ref, sh_ref, yn_ref, pext_ref, pooled_ref,
                   *, tm, carry, start_pos):
    i = pl.program_id(0)
    hdr = HIST_ROWS

    def load_hist():
        ext_ref[0:hdr, :] = hc_ref[...]
        pext_ref[0:hdr, :] = hp_ref[...]

    if carry:
        pl.when(i == 0)(load_hist)
    else:
        load_hist()

    ext_ref[hdr:hdr + tm, :] = glu_ref[...]
    pext_ref[hdr:hdr + tm, :] = up_ref[...]
    cs_ref[...] = ext_ref[pl.ds(hdr + tm - (CONV_KERNEL - 1), CONV_KERNEL - 1), :]
    ps_ref[...] = pext_ref[pl.ds(hdr + tm - POOL_HIST, POOL_HIST), :]

    sh_rows = sh_ref.shape[1]
    for r in range(1, 8):
        sh_ref[r - 1] = ext_ref[pl.ds(r, sh_rows), :]

    rc = min(CONV_CHUNK, tm)
    first = hdr - (CONV_KERNEL - 1)

    def conv_chunk(c, _):
        base = pl.multiple_of(c * rc, rc)
        acc = jnp.zeros((rc, CONV_WIDTH), F32)
        for r in range(8):
            taps = [(j, ((first + j) // 8) * 8) for j in range(CONV_KERNEL)
                    if (first + j) % 8 == r]
            lo = min(off for _, off in taps)
            hi = max(off for _, off in taps)
            src = ext_ref if r == 0 else sh_ref.at[r - 1]
            slab = src[pl.ds(base + lo, rc + hi - lo), :]
            for j, off in taps:
                acc = acc + slab[off - lo:off - lo + rc] * dww_ref[j:j + 1, :]
        y = acc + dwb_ref[...]
        gw = CONV_WIDTH // CONV_GROUPS
        for g in range(CONV_GROUPS):
            sl = slice(g * gw, (g + 1) * gw)
            yg = y[:, sl]
            mu = jnp.mean(yg, axis=-1, keepdims=True)
            dv = yg - mu
            var = jnp.mean(dv * dv, axis=-1, keepdims=True)
            yn = dv * lax.rsqrt(var + EPS) * ng_ref[:, sl] + nb_ref[:, sl]
            yn = yn * jax.nn.sigmoid(yn)
            yn_ref[pl.ds(base, rc), sl] = yn.astype(BF16)
        return 0

    lax.fori_loop(0, tm // rc, conv_chunk, 0)
    oc_ref[...] = (_dot(yn_ref[...], pww_ref[...]) + pwb_ref[...]).astype(oc_ref.dtype)

    pc = min(LANE, tm)
    for c in range(tm // pc):
        b0 = hdr + c * pc
        row = lax.broadcasted_iota(jnp.int32, (pc, POOL_GROUP), 0) + c * pc
        pos = row + (start_pos + (i * tm if carry else 0))
        for g, w in enumerate(POOL_WINDOWS):
            sl = slice(g * POOL_GROUP, (g + 1) * POOL_GROUP)
            u = pext_ref[b0:b0 + pc, sl]
            s = u
            for dlt in range(1, w):
                s = s + pext_ref[pl.ds(b0 - dlt, pc), sl]
            cnt = jnp.minimum(w, pos + 1).astype(F32)
            pooled_ref[c * pc:(c + 1) * pc, sl] = (s / cnt - u).astype(BF16)
    op_ref[...] = (_dot(pooled_ref[...], plw_ref[...]) * pls_ref[...]).astype(op_ref.dtype)

    if carry:
        ext_ref[0:hdr, :] = ext_ref[tm:tm + hdr, :]
        pext_ref[0:hdr, :] = pext_ref[tm:tm + hdr, :]


def mixers_call(glu, up, hist_c, hist_p, lw, carry, start_pos):
    m = glu.shape[0]
    nseq = hist_c.shape[0]
    tm = min(ROW_TILE, m) if carry else m // nseq
    row = lambda i: (i, 0)
    const = lambda i: (0, 0)
    seq = (lambda i: (0, 0, 0)) if carry else (lambda i: (i, 0, 0))
    vec = pl.BlockSpec((1, CONV_WIDTH), const)
    sq = pl.BlockSpec((CONV_WIDTH, CONV_WIDTH), const)
    return pl.pallas_call(
        functools.partial(_mixers_kernel, tm=tm, carry=carry, start_pos=start_pos),
        grid=(m // tm,),
        in_specs=[pl.BlockSpec((tm, CONV_WIDTH), row),
                  pl.BlockSpec((tm, POOL_WIDTH), row),
                  pl.BlockSpec((None, HIST_ROWS, CONV_WIDTH), seq),
                  pl.BlockSpec((None, HIST_ROWS, POOL_WIDTH), seq),
                  pl.BlockSpec((HIST_ROWS, CONV_WIDTH), const),
                  vec, vec, vec, sq, vec, sq, vec],
        out_specs=[pl.BlockSpec((tm, CONV_WIDTH), row),
                   pl.BlockSpec((tm, POOL_WIDTH), row),
                   pl.BlockSpec((None, CONV_KERNEL - 1, CONV_WIDTH), seq),
                   pl.BlockSpec((None, POOL_HIST, POOL_WIDTH), seq)],
        out_shape=[jax.ShapeDtypeStruct((m, CONV_WIDTH), BF16),
                   jax.ShapeDtypeStruct((m, POOL_WIDTH), BF16),
                   jax.ShapeDtypeStruct((nseq, CONV_KERNEL - 1, CONV_WIDTH), F32),
                   jax.ShapeDtypeStruct((nseq, POOL_HIST, POOL_WIDTH), F32)],
        scratch_shapes=[pltpu.VMEM((HIST_ROWS + tm + 8, CONV_WIDTH), F32),
                        pltpu.VMEM((7, tm + HIST_ROWS - 8, CONV_WIDTH), F32),
                        pltpu.VMEM((tm, CONV_WIDTH), BF16),
                        pltpu.VMEM((HIST_ROWS + tm, POOL_WIDTH), F32),
                        pltpu.VMEM((tm, POOL_WIDTH), BF16)],
        compiler_params=_params("arbitrary"),
        name="mixers_prompt" if carry else "mixers_sample",
    )(glu, up, hist_c, hist_p, lw["dw_w"], lw["dw_b"], lw["n_g"], lw["n_b"],
      lw["pw_w"], lw["pw_b"], lw["pool_w"], lw["pool_s"])


def _outproj_kernel(h_ref, osb_ref, oc_ref, op_ref, w_ref, g_ref, hn_ref, xn_ref):
    c1 = SB_WIDTH
    c2 = SB_WIDTH + CONV_WIDTH
    mix = (_dot(osb_ref[...], w_ref[0:c1, :]) + _dot(oc_ref[...], w_ref[c1:c2, :])
           + _dot(op_ref[...], w_ref[c2:, :]))
    h = h_ref[...] + mix
    hn_ref[...] = h
    xn_ref[...] = _rmsnorm_rows(h, g_ref[...]).astype(xn_ref.dtype)


def outproj_call(h, osb, oc, op, w_out, g, layer):
    m, d = h.shape
    tm = min(ROW_TILE, m)
    row = lambda i: (i, 0)
    return pl.pallas_call(
        _outproj_kernel,
        grid=(m // tm,),
        in_specs=[pl.BlockSpec((tm, d), row),
                  pl.BlockSpec((tm, SB_WIDTH), row),
                  pl.BlockSpec((tm, CONV_WIDTH), row),
                  pl.BlockSpec((tm, POOL_WIDTH), row),
                  pl.BlockSpec((None, d, d), lambda i: (layer, 0, 0),
                               pipeline_mode=pl.Buffered(1)),
                  pl.BlockSpec((1, d), lambda i: (0, 0))],
        out_specs=[pl.BlockSpec((tm, d), row), pl.BlockSpec((tm, d), row)],
        out_shape=[jax.ShapeDtypeStruct((m, d), F32), jax.ShapeDtypeStruct((m, d), BF16)],
        compiler_params=_params("arbitrary"),
        name="outproj",
    )(h, osb, oc, op, w_out, g.reshape(1, d))


def _mlp_kernel(xn_ref, h_ref, wu_ref, wd_ref, *rest, final):
    if final:
        g_ref, o_ref, acc_ref = rest
    else:
        o_ref, acc_ref = rest
    f = pl.program_id(1)

    @pl.when(f == 0)
    def _():
        acc_ref[...] = jnp.zeros_like(acc_ref)

    a = jnp.maximum(_dot(xn_ref[...], wu_ref[...]), 0.0)
    acc_ref[...] += _dot((a * a).astype(BF16), wd_ref[...])

    @pl.when(f == pl.num_programs(1) - 1)
    def _():
        h = h_ref[...] + acc_ref[...]
        o_ref[...] = _rmsnorm_rows(h, g_ref[...]) if final else h


def mlp_call(xn, h, w_up, w_down, layer, final_g=None):
    m, d = h.shape
    ff = w_up.shape[2]
    tm = min(ROW_TILE, m)
    tf = FF_TILE
    row = lambda i, f: (i, 0)
    final = final_g is not None
    in_specs = [pl.BlockSpec((tm, d), row),
                pl.BlockSpec((tm, d), row),
                pl.BlockSpec((None, d, tf), lambda i, f: (layer, 0, f)),
                pl.BlockSpec((None, tf, d), lambda i, f: (layer, f, 0))]
    args = [xn, h, w_up, w_down]
    if final:
        in_specs.append(pl.BlockSpec((1, d), lambda i, f: (0, 0)))
        args.append(final_g.reshape(1, d))
    return pl.pallas_call(
        functools.partial(_mlp_kernel, final=final),
        grid=(m // tm, ff // tf),
        in_specs=in_specs,
        out_specs=pl.BlockSpec((tm, d), row),
        out_shape=jax.ShapeDtypeStruct((m, d), F32),
        scratch_shapes=[pltpu.VMEM((tm, d), F32)],
        compiler_params=_params("arbitrary", "arbitrary"),
        name="mlp_final" if final else "mlp",
    )(*args)


def _pad_hist(hist, rows):
    return jnp.pad(hist, ((0, 0), (rows - hist.shape[1], 0), (0, 0)))


def _block_diag(w):
    g, a, b = w.shape
    eye = jnp.eye(g, dtype=w.dtype)
    return (eye[:, None, :, None] * w[:, :, None, :]).reshape(g * a, g * b)


def _group_trunk(x, layers, big, final_g, sample, caches):
    depth = len(layers)
    h = x
    cs, ps = [], []
    shape = kv_stack_shape(x.shape[0], sample, depth)
    kv = (jnp.zeros(shape, F32), jnp.zeros(shape, F32))
    for l, lw in enumerate(layers):
        if sample:
            cache_k, cache_v, cache_conv, state_pool = caches
            (q, glu, up), kv = inproj_call(h, lw["norm_mix_g"], big["w_in"], True, l, kv)
            osb = attn_sample_call(q, kv[0], kv[1], cache_k, cache_v, l)
            hist_c = _pad_hist(cache_conv[l], HIST_ROWS)
            hist_p = _pad_hist(state_pool[l], HIST_ROWS)
            start = cache_k.shape[3]
        else:
            (qt, kb, vt, glu, up), kv = inproj_call(h, lw["norm_mix_g"], big["w_in"], False,
                                                    l, kv)
            osb = attn_prompt_call(qt, kb, vt)
            hist_c = jnp.zeros((1, HIST_ROWS, CONV_WIDTH), F32)
            hist_p = jnp.zeros((1, HIST_ROWS, POOL_WIDTH), F32)
            start = 0
        oc, op, c_state, p_state = mixers_call(glu, up, hist_c, hist_p, lw, not sample, start)
        h, xn2 = outproj_call(h, osb, oc, op, big["w_out"], lw["norm_mlp_g"], l)
        h = mlp_call(xn2, h, big["w_up"], big["w_down"], l, final_g if l + 1 == depth else None)
        cs.append(c_state)
        ps.append(p_state)
    return h, kv[0], kv[1], jnp.stack(cs), jnp.stack(ps)


def kernel(x_prompt, x_sample, cache_k, cache_v, cache_conv, state_pool, norm_mix_g, w_in,
           conv_dw_w, conv_dw_b, conv_norm_g, conv_norm_b, conv_pw_w, conv_pw_b, pool_w,
           pool_scale, w_out, norm_mlp_g, w_up, w_down, final_norm_g):
    depth = w_in.shape[0]
    bp, seq, d = x_prompt.shape
    bs, dseq, _ = x_sample.shape
    assert bp == 1 and dseq == DEC_SEQ and d == D_MODEL
    layers = []
    for l in range(depth):
        layers.append(dict(
            norm_mix_g=norm_mix_g[l],
            dw_w=jnp.pad(conv_dw_w[l], ((0, HIST_ROWS - CONV_KERNEL), (0, 0))),
            dw_b=conv_dw_b[l].reshape(1, -1),
            n_g=conv_norm_g[l].reshape(1, -1),
            n_b=conv_norm_b[l].reshape(1, -1),
            pw_w=conv_pw_w[l].astype(BF16),
            pw_b=conv_pw_b[l].reshape(1, -1),
            pool_w=_block_diag(pool_w[l]).astype(BF16),
            pool_s=pool_scale[l].reshape(1, -1),
            norm_mlp_g=norm_mlp_g[l],
        ))
    big = dict(w_in=w_in.astype(BF16), w_out=w_out.astype(BF16),
               w_up=w_up.astype(BF16), w_down=w_down.astype(BF16))
    yp, kp, vp, cp, pp = _group_trunk(x_prompt.reshape(seq, d), layers, big, final_norm_g,
                                      False, None)
    ys, ks, vs, cs, ps = _group_trunk(x_sample.reshape(bs * dseq, d), layers, big, final_norm_g, True,
                                      (cache_k, cache_v, cache_conv, state_pool))
    return (yp.reshape(bp, seq, d), ys.reshape(bs, dseq, d),
            kp[:, None], vp[:, None], cp, pp, ks, vs, cs, ps)
```

```python
import functools
import math

import jax
import jax.numpy as jnp
from jax import lax
from jax.experimental import pallas as pl
from jax.experimental.pallas import tpu as pltpu

F32 = jnp.float32
BF16 = jnp.bfloat16

D_MODEL = 2048
SB_WIDTH = 1024
HEAD_DIM = 128
HEADS = SB_WIDTH // HEAD_DIM
CONV_WIDTH = 512
CONV_KERNEL = 31
CONV_GROUPS = 4
POOL_WIDTH = 512
POOL_WINDOWS = (2, 4, 8, 16)
POOL_GROUP = POOL_WIDTH // len(POOL_WINDOWS)
POOL_HIST = max(POOL_WINDOWS) - 1
D_FF = 4 * D_MODEL
IN_COLS = 3 * SB_WIDTH + 2 * CONV_WIDTH + POOL_WIDTH
EPS = 1e-6
DEC_SEQ = 64

LOG2E = 1.4426950408889634
Q_SCALE = HEAD_DIM ** -0.5 * LOG2E

LANE = 128
HIST_ROWS = 32
ROW_TILE = 512
MLP_ROW_TILE = 1024
FF_TILE = 512
ATTN_TK = 128
ATTN_VT = 2 * ATTN_TK
SAMPLE_TK = 256
CONV_CHUNK = 64
VMEM_LIMIT = 56 * 1024 * 1024


def _params(*sem):
    return pltpu.CompilerParams(dimension_semantics=sem, vmem_limit_bytes=VMEM_LIMIT)


def _dot(a, b):
    return jnp.dot(a, b, preferred_element_type=F32)


def _softplus2(z):
    e = jnp.exp2(-jnp.abs(z))
    return jnp.maximum(z, 0.0) + jnp.log(1.0 + e) * LOG2E


def _rmsnorm_rows(x, g):
    ms = jnp.mean(x * x, axis=-1, keepdims=True)
    return x * lax.rsqrt(ms + EPS) * g


def _inproj_kernel(h_ref, g_ref, w_ref, *rest, tm, sample, n_alias):
    outs = rest[n_alias:]
    x = _rmsnorm_rows(h_ref[...], g_ref[...]).astype(BF16)
    if sample:
        q_ref, kf_ref, vf_ref, glu_ref, up_ref = outs
        nb = tm // DEC_SEQ
    else:
        qt_ref, kb_ref, vt_ref, kf_ref, vf_ref, glu_ref, up_ref = outs
    pair = 2 * HEAD_DIM
    for hp in range(HEADS // 2):
        c0 = hp * pair
        q2 = _dot(x, w_ref[:, c0:c0 + pair]) * Q_SCALE
        k2 = _dot(x, w_ref[:, SB_WIDTH + c0:SB_WIDTH + c0 + pair])
        v2 = _dot(x, w_ref[:, 2 * SB_WIDTH + c0:2 * SB_WIDTH + c0 + pair])
        for u in range(2):
            h = 2 * hp + u
            sl = slice(u * HEAD_DIM, (u + 1) * HEAD_DIM)
            if sample:
                q_ref[:, h] = q2[:, sl].astype(BF16).reshape(nb, DEC_SEQ, HEAD_DIM)
                kf_ref[:, h] = k2[:, sl].reshape(nb, DEC_SEQ, HEAD_DIM)
                vf_ref[:, h] = v2[:, sl].reshape(nb, DEC_SEQ, HEAD_DIM)
            else:
                qt_ref[h] = q2[:, sl].T.astype(BF16)
                kb_ref[h] = k2[:, sl].astype(BF16)
                kf_ref[h] = k2[:, sl]
                vf_ref[h] = v2[:, sl]
                vt = v2[:, sl].T.astype(BF16)
                for c in range(tm // ATTN_VT):
                    vt_ref[h, c] = vt[:, c * ATTN_VT:(c + 1) * ATTN_VT]
    c0 = 3 * SB_WIDTH
    uc = _dot(x, w_ref[:, c0:c0 + 2 * CONV_WIDTH])
    glu_ref[...] = uc[:, 0:CONV_WIDTH] * jax.nn.sigmoid(uc[:, CONV_WIDTH:])
    up_ref[...] = _dot(x, w_ref[:, c0 + 2 * CONV_WIDTH:])


def kv_stack_shape(m, sample, depth):
    if sample:
        return (depth, m // DEC_SEQ, HEADS, DEC_SEQ, HEAD_DIM)
    return (depth, HEADS, m, HEAD_DIM)


def inproj_call(h, g, w_in, sample, layer, kv_stack):
    depth = kv_stack[0].shape[0]
    m, d = h.shape
    tm = min(ROW_TILE, m)
    if sample:
        nbt = m // DEC_SEQ
        nb = tm // DEC_SEQ
        hshape = (nbt, HEADS, DEC_SEQ, HEAD_DIM)
        hblock = (nb, HEADS, DEC_SEQ, HEAD_DIM)
        kv_shape = jax.ShapeDtypeStruct((depth,) + hshape, F32)
        kv_spec = pl.BlockSpec((None,) + hblock, lambda i: (layer, i, 0, 0, 0))
        out_shape = [jax.ShapeDtypeStruct(hshape, BF16), kv_shape, kv_shape]
        out_specs = [pl.BlockSpec(hblock, lambda i: (i, 0, 0, 0)), kv_spec, kv_spec]
        kv_index = (1, 2)
    else:
        nkb = m // ATTN_VT
        kshape = (HEADS, m, HEAD_DIM)
        kblock = (HEADS, tm, HEAD_DIM)
        kv_shape = jax.ShapeDtypeStruct((depth,) + kshape, F32)
        kv_spec = pl.BlockSpec((None,) + kblock, lambda i: (layer, 0, i, 0))
        out_shape = [jax.ShapeDtypeStruct((HEADS, HEAD_DIM, m), BF16),
                     jax.ShapeDtypeStruct(kshape, BF16),
                     jax.ShapeDtypeStruct((HEADS, nkb, HEAD_DIM, ATTN_VT), BF16),
                     kv_shape, kv_shape]
        out_specs = [pl.BlockSpec((HEADS, HEAD_DIM, tm), lambda i: (0, 0, i)),
                     pl.BlockSpec(kblock, lambda i: (0, i, 0)),
                     pl.BlockSpec((HEADS, tm // ATTN_VT, HEAD_DIM, ATTN_VT),
                                  lambda i: (0, i, 0, 0)),
                     kv_spec, kv_spec]
        kv_index = (3, 4)
    out_shape += [jax.ShapeDtypeStruct((m, CONV_WIDTH), F32),
                  jax.ShapeDtypeStruct((m, POOL_WIDTH), F32)]
    out_specs += [pl.BlockSpec((tm, CONV_WIDTH), lambda i: (i, 0)),
                  pl.BlockSpec((tm, POOL_WIDTH), lambda i: (i, 0))]
    in_specs = [pl.BlockSpec((tm, d), lambda i: (i, 0)),
                pl.BlockSpec((1, d), lambda i: (0, 0)),
                pl.BlockSpec((None, d, IN_COLS), lambda i: (layer, 0, 0),
                             pipeline_mode=pl.Buffered(1))]
    args = [h, g.reshape(1, d), w_in]
    aliases = {}
    for n, stack in enumerate(kv_stack):
        assert stack.shape == kv_shape.shape and stack.dtype == kv_shape.dtype
        aliases[len(args)] = kv_index[n]
        in_specs.append(pl.BlockSpec(memory_space=pl.ANY))
        args.append(stack)
    outs = pl.pallas_call(
        functools.partial(_inproj_kernel, tm=tm, sample=sample, n_alias=len(aliases)),
        grid=(m // tm,),
        in_specs=in_specs,
        out_specs=out_specs,
        out_shape=out_shape,
        input_output_aliases=aliases,
        compiler_params=_params("arbitrary"),
        name="inproj_sample" if sample else "inproj_prompt",
    )(*args)
    kv = tuple(outs[n] for n in kv_index)
    return [o for n, o in enumerate(outs) if n not in kv_index], kv


MASKED_LOGIT = -1e30
STICK_EXHAUSTED = 160.0


def _attn_stage1(k_ref, qt, p, i, masked, a_ref, sp_ref, *, tq, tk):
    for half in (1, 0):
        row0 = pl.multiple_of(p * (2 * tk) + half * tk, tk)
        z = _dot(k_ref[pl.ds(row0, tk), :], qt)
        sp = _softplus2(z)
        a = z - sp
        if masked:
            s_idx = row0 + lax.broadcasted_iota(jnp.int32, (tk, tq), 0)
            t_idx = i * tq + lax.broadcasted_iota(jnp.int32, (tk, tq), 1)
            valid = s_idx < t_idx
            sp = jnp.where(valid, sp, 0.0)
            a = jnp.where(valid, a, MASKED_LOGIT)
        a_ref[half * tk:(half + 1) * tk, :] = a
        sp_ref[half] = sp.astype(BF16)


def _attn_stage2(tri_ref, a_ref, sp_ref, half, carry, *, tk):
    sp = sp_ref[half]
    later = _dot(tri_ref[...], sp)
    w = jnp.exp2(a_ref[half * tk:(half + 1) * tk, :] - later - carry)
    return w.astype(BF16), carry + later[0:1, :] + sp[0:1, :].astype(F32)


ATTN_HEADS_PER_STEP = 4
ATTN_STAGE_BUFS = 2
PV_LAG = 2


def _attn_prompt_kernel(qt_ref, k_ref, vt_ref, tri_ref, o_ref, *scratch, tq, tk):
    i = pl.program_id(1)
    nh = ATTN_HEADS_PER_STEP
    heads = range(nh)
    acc = scratch[0:nh]
    zb = [scratch[nh + b * nh:nh + (b + 1) * nh] for b in range(ATTN_STAGE_BUFS)]
    sp0 = nh * (1 + ATTN_STAGE_BUFS)
    spb = [scratch[sp0 + b * nh:sp0 + (b + 1) * nh] for b in range(ATTN_STAGE_BUFS)]
    s1 = functools.partial(_attn_stage1, tq=tq, tk=tk)
    s2 = functools.partial(_attn_stage2, tk=tk)
    qts = [qt_ref[h] for h in heads]
    for h in heads:
        acc[h][...] = jnp.zeros_like(acc[h])

    def second(buf, p, carry):
        carry = list(carry)
        units = [(h, half) for h in heads for half in (1, 0)]
        ws = {}
        for n in range(len(units) + PV_LAG):
            if n < len(units):
                h, half = units[n]
                ws[units[n]], carry[h] = s2(tri_ref, zb[buf][h], spb[buf][h], half, carry[h])
            if n >= PV_LAG:
                h, half = units[n - PV_LAG]
                vt = vt_ref[h, p, :, half * tk:(half + 1) * tk]
                acc[h][...] += _dot(vt, ws.pop(units[n - PV_LAG]))
        return tuple(carry)

    def step(buf, p, carry):
        for h in heads:
            s1(k_ref.at[h], qts[h], jnp.maximum(p - 1, 0), i, False,
               zb[1 - buf][h], spb[1 - buf][h])
        return second(buf, p, carry)

    for h in heads:
        s1(k_ref.at[h], qts[h], i, i, True, zb[0][h], spb[0][h])

    def live(state):
        m, _, least = state
        return jnp.logical_and(m < (i + 1) // 2, least <= STICK_EXHAUSTED)

    def body(state):
        m, carry, _ = state
        p = i - 2 * m
        carry = step(1, p - 1, step(0, p, carry))
        least = functools.reduce(jnp.minimum, [jnp.min(c) for c in carry])
        return m + 1, carry, least

    zero = tuple(jnp.zeros((1, tq), F32) for _ in heads)
    _, carry, least = lax.while_loop(live, body, (jnp.int32(0), zero, jnp.float32(0.0)))

    @pl.when(jnp.logical_and(i % 2 == 0, least <= STICK_EXHAUSTED))
    def _():
        second(0, 0, carry)

    d = acc[0].shape[0]
    for h in heads:
        o_ref[:, h * d:(h + 1) * d] = acc[h][...].T.astype(o_ref.dtype)


def _tri(tk):
    s = lax.broadcasted_iota(jnp.int32, (tk, tk), 0)
    j = lax.broadcasted_iota(jnp.int32, (tk, tk), 1)
    return (j > s).astype(BF16)


def attn_prompt_call(qt, kb, vt):
    h, d, t = qt.shape
    tq = ATTN_VT
    tk = ATTN_TK
    tri = _tri(tk)
    nh = ATTN_HEADS_PER_STEP
    nstage = nh * ATTN_STAGE_BUFS
    return pl.pallas_call(
        functools.partial(_attn_prompt_kernel, tq=tq, tk=tk),
        grid=(h // nh, t // tq),
        in_specs=[pl.BlockSpec((nh, d, tq), lambda hh, i: (hh, 0, i)),
                  pl.BlockSpec((nh, t, d), lambda hh, i: (hh, 0, 0)),
                  pl.BlockSpec((nh, t // ATTN_VT, d, ATTN_VT), lambda hh, i: (hh, 0, 0, 0)),
                  pl.BlockSpec((tk, tk), lambda hh, i: (0, 0))],
        out_specs=pl.BlockSpec((tq, nh * d), lambda hh, i: (i, hh)),
        out_shape=jax.ShapeDtypeStruct((t, h * d), BF16),
        scratch_shapes=([pltpu.VMEM((d, tq), F32)] * nh
                        + [pltpu.VMEM((2 * tk, tq), F32)] * nstage
                        + [pltpu.VMEM((2, tk, tq), BF16)] * nstage),
        compiler_params=_params("arbitrary", "arbitrary"),
        name="attn_prompt",
    )(qt, kb, vt, tri)


SAMPLE_HEADS_PER_STEP = 4


def _sample_sweep(q_ref, kb_ref, vb_ref, nblk, mask_last, carry, tri_ref, acc_ref,
                  z_ref, sp_ref, cum_ref, w_ref, *, tk):
    nt = (((1,), (1,)), ((), ()))
    nh, ts, d = q_ref.shape
    rows = nh * ts
    tot = nblk * tk
    for h in range(nh):
        z_ref[h * ts:(h + 1) * ts, 0:tot] = lax.dot_general(
            q_ref[h], kb_ref[h, 0:tot, :], nt, preferred_element_type=F32)
    for b in range(nblk):
        cols = slice(b * tk, (b + 1) * tk)
        z = z_ref[:, cols]
        sp = _softplus2(z)
        a = z - sp
        if mask_last and b == nblk - 1:
            t_idx = lax.broadcasted_iota(jnp.int32, (rows, tk), 0) % ts
            s_idx = lax.broadcasted_iota(jnp.int32, (rows, tk), 1)
            valid = s_idx < t_idx
            sp = jnp.where(valid, sp, 0.0)
            a = jnp.where(valid, a, MASKED_LOGIT)
        z_ref[:, cols] = a
        sp_ref[b * rows:(b + 1) * rows, :] = sp.astype(BF16)
    cum_ref[0:nblk * rows, :] = _dot(sp_ref[0:nblk * rows, :], tri_ref[...])
    for b in range(nblk - 1, -1, -1):
        cols = slice(b * tk, (b + 1) * tk)
        later = cum_ref[b * rows:(b + 1) * rows, :]
        w_ref[:, cols] = jnp.exp2(z_ref[:, cols] - later - carry).astype(BF16)
        first = sp_ref[b * rows:(b + 1) * rows, 0:1].astype(F32)
        carry = carry + later[:, 0:1] + first
    for h in range(nh):
        acc_ref[h * ts:(h + 1) * ts, :] += _dot(w_ref[h * ts:(h + 1) * ts, 0:tot],
                                                 vb_ref[h, 0:tot, :])
    return carry


def _attn_sample_kernel(q_ref, kn_ref, vn_ref, kt_ref, vt_ref, ck_hbm, cv_hbm, tri_ref, o_ref,
                        kb_ref, vb_ref, ckf_ref, cvf_ref, z_ref, sp_ref, cum_ref, w_ref,
                        acc_ref, sem, *, past, tk, layer):
    nh, ts, d = q_ref.shape
    rows = nh * ts
    early = past - tk
    sweep = functools.partial(_sample_sweep, q_ref, kb_ref, vb_ref, tri_ref=tri_ref,
                              acc_ref=acc_ref, z_ref=z_ref, sp_ref=sp_ref, cum_ref=cum_ref,
                              w_ref=w_ref, tk=tk)
    acc_ref[...] = jnp.zeros_like(acc_ref)
    for tail, new, dst in ((kt_ref, kn_ref, kb_ref), (vt_ref, vn_ref, vb_ref)):
        for h in range(nh):
            dst[h, 0:tk, :] = tail[h].astype(BF16)
            dst[h, tk:tk + ts, :] = new[h].astype(BF16)
            dst[h, tk + ts:2 * tk, :] = jnp.zeros((tk - ts, d), BF16)
    carry = sweep(2, True, jnp.zeros((rows, 1), F32))

    @pl.when(jnp.min(carry) <= STICK_EXHAUSTED)
    def _():
        b = pl.program_id(0)
        h0 = pl.multiple_of(pl.program_id(1) * nh, nh)
        copies = [pltpu.make_async_copy(
            src.at[layer, b, pl.ds(h0, nh), pl.ds(0, early), :], dst, sem.at[n])
            for n, (src, dst) in enumerate(((ck_hbm, ckf_ref), (cv_hbm, cvf_ref)))]
        for c in copies:
            c.start()
        for c in copies:
            c.wait()
        for h in range(nh):
            kb_ref[h, 0:early, :] = ckf_ref[h].astype(BF16)
            vb_ref[h, 0:early, :] = cvf_ref[h].astype(BF16)
        sweep(early // tk, False, carry)

    for h in range(nh):
        o_ref[:, h * d:(h + 1) * d] = acc_ref[h * ts:(h + 1) * ts, :].astype(o_ref.dtype)


def attn_sample_call(q, k_stack, v_stack, cache_k, cache_v, layer):
    b, h, ts, d = q.shape
    past = cache_k.shape[3]
    tk = SAMPLE_TK
    nh = SAMPLE_HEADS_PER_STEP
    assert past % tk == 0 and past >= 2 * tk and ts <= tk
    rows = nh * ts
    early = past - tk
    span = max(early, 2 * tk)
    tri = _tri(tk).T
    q_spec = pl.BlockSpec((None, nh, ts, d), lambda bb, hh: (bb, hh, 0, 0))
    new_spec = pl.BlockSpec((None, None, nh, ts, d), lambda bb, hh: (layer, bb, hh, 0, 0))
    tail_spec = pl.BlockSpec((None, None, nh, tk, d),
                             lambda bb, hh: (layer, bb, hh, past // tk - 1, 0))
    any_spec = pl.BlockSpec(memory_space=pl.ANY)
    return pl.pallas_call(
        functools.partial(_attn_sample_kernel, past=past, tk=tk, layer=layer),
        grid=(b, h // nh),
        in_specs=[q_spec, new_spec, new_spec, tail_spec, tail_spec, any_spec, any_spec,
                  pl.BlockSpec((tk, tk), lambda bb, hh: (0, 0))],
        out_specs=pl.BlockSpec((ts, nh * d), lambda bb, hh: (bb, hh)),
        out_shape=jax.ShapeDtypeStruct((b * ts, h * d), BF16),
        scratch_shapes=[pltpu.VMEM((nh, span, d), BF16), pltpu.VMEM((nh, span, d), BF16),
                        pltpu.VMEM((nh, early, d), F32), pltpu.VMEM((nh, early, d), F32),
                        pltpu.VMEM((rows, span), F32),
                        pltpu.VMEM((span // tk * rows, tk), BF16),
                        pltpu.VMEM((span // tk * rows, tk), F32),
                        pltpu.VMEM((rows, span), BF16),
                        pltpu.VMEM((rows, d), F32),
                        pltpu.SemaphoreType.DMA((2,))],
        compiler_params=_params("arbitrary", "arbitrary"),
        name="attn_sample",
    )(q, k_stack, v_stack, cache_k, cache_v, cache_k, cache_v, tri)


def _mixers_kernel(glu_ref, up_ref, hc_ref, hp_ref, dww_ref, dwb_ref, ng_ref, nb_ref,
                   pww_ref, pwb_ref, plw_ref, pls_ref,
                   oc_ref, op_ref, cs_ref, ps_ref,
                   ext_ref, sh_ref, yn_ref, pext_ref, pooled_ref,
                   *, tm, carry, start_pos):
    i = pl.program_id(0)
    hdr = HIST_ROWS

    def load_hist():
        ext_ref[0:hdr, :] = hc_ref[...]
        pext_ref[0:hdr, :] = hp_ref[...]

    if carry:
        pl.when(i == 0)(load_hist)
    else:
        load_hist()

    ext_ref[hdr:hdr + tm, :] = glu_ref[...]
    pext_ref[hdr:hdr + tm, :] = up_ref[...]
    cs_ref[...] = ext_ref[pl.ds(hdr + tm - (CONV_KERNEL - 1), CONV_KERNEL - 1), :]
    ps_ref[...] = pext_ref[pl.ds(hdr + tm - POOL_HIST, POOL_HIST), :]

    sh_rows = sh_ref.shape[1]
    for r in range(1, 8):
        sh_ref[r - 1] = ext_ref[pl.ds(r, sh_rows), :]

    rc = min(CONV_CHUNK, tm)
    first = hdr - (CONV_KERNEL - 1)

    def conv_chunk(c, _):
        base = pl.multiple_of(c * rc, rc)
        acc = jnp.zeros((rc, CONV_WIDTH), F32)
        for r in range(8):
            taps = [(j, ((first + j) // 8) * 8) for j in range(CONV_KERNEL)
                    if (first + j) % 8 == r]
            lo = min(off for _, off in taps)
            hi = max(off for _, off in taps)
            src = ext_ref if r == 0 else sh_ref.at[r - 1]
            slab = src[pl.ds(base + lo, rc + hi - lo), :]
            for j, off in taps:
                acc = acc + slab[off - lo:off - lo + rc] * dww_ref[j:j + 1, :]
        y = acc + dwb_ref[...]
        gw = CONV_WIDTH // CONV_GROUPS
        for g in range(CONV_GROUPS):
            sl = slice(g * gw, (g + 1) * gw)
            yg = y[:, sl]
            mu = jnp.mean(yg, axis=-1, keepdims=True)
            dv = yg - mu
            var = jnp.mean(dv * dv, axis=-1, keepdims=True)
            yn = dv * lax.rsqrt(var + EPS) * ng_ref[:, sl] + nb_ref[:, sl]
            yn = yn * jax.nn.sigmoid(yn)
            yn_ref[pl.ds(base, rc), sl] = yn.astype(BF16)
        return 0

    lax.fori_loop(0, tm // rc, conv_chunk, 0)
    oc_ref[...] = (_dot(yn_ref[...], pww_ref[...]) + pwb_ref[...]).astype(oc_ref.dtype)

    pc = min(LANE, tm)
    for c in range(tm // pc):
        b0 = hdr + c * pc
        row = lax.broadcasted_iota(jnp.int32, (pc, POOL_GROUP), 0) + c * pc
        pos = row + (start_pos + (i * tm if carry else 0))
        for g, w in enumerate(POOL_WINDOWS):
            sl = slice(g * POOL_GROUP, (g + 1) * POOL_GROUP)
            u = pext_ref[b0:b0 + pc, sl]
            s = u
            for dlt in range(1, w):
                s = s + pext_ref[pl.ds(b0 - dlt, pc), sl]
            cnt = jnp.minimum(w, pos + 1).astype(F32)
            pooled_ref[c * pc:(c + 1) * pc, sl] = (s / cnt - u).astype(BF16)
    op_ref[...] = (_dot(pooled_ref[...], plw_ref[...]) * pls_ref[...]).astype(op_ref.dtype)

    if carry:
        ext_ref[0:hdr, :] = ext_ref[tm:tm + hdr, :]
        pext_ref[0:hdr, :] = pext_ref[tm:tm + hdr, :]


def mixers_call(glu, up, hist_c, hist_p, lw, carry, start_pos):
    m = glu.shape[0]
    nseq = hist_c.shape[0]
    tm = min(ROW_TILE, m) if carry else m // nseq
    row = lambda i: (i, 0)
    const = lambda i: (0, 0)
    seq = (lambda i: (0, 0, 0)) if carry else (lambda i: (i, 0, 0))
    vec = pl.BlockSpec((1, CONV_WIDTH), const)
    sq = pl.BlockSpec((CONV_WIDTH, CONV_WIDTH), const)
    return pl.pallas_call(
        functools.partial(_mixers_kernel, tm=tm, carry=carry, start_pos=start_pos),
        grid=(m // tm,),
        in_specs=[pl.BlockSpec((tm, CONV_WIDTH), row),
                  pl.BlockSpec((tm, POOL_WIDTH), row),
                  pl.BlockSpec((None, HIST_ROWS, CONV_WIDTH), seq),
                  pl.BlockSpec((None, HIST_ROWS, POOL_WIDTH), seq),
                  pl.BlockSpec((HIST_ROWS, CONV_WIDTH), const),
                  vec, vec, vec, sq, vec, sq, vec],
        out_specs=[pl.BlockSpec((tm, CONV_WIDTH), row),
                   pl.BlockSpec((tm, POOL_WIDTH), row),
                   pl.BlockSpec((None, CONV_KERNEL - 1, CONV_WIDTH), seq),
                   pl.BlockSpec((None, POOL_HIST, POOL_WIDTH), seq)],
        out_shape=[jax.ShapeDtypeStruct((m, CONV_WIDTH), BF16),
                   jax.ShapeDtypeStruct((m, POOL_WIDTH), BF16),
                   jax.ShapeDtypeStruct((nseq, CONV_KERNEL - 1, CONV_WIDTH), F32),
                   jax.ShapeDtypeStruct((nseq, POOL_HIST, POOL_WIDTH), F32)],
        scratch_shapes=[pltpu.VMEM((HIST_ROWS + tm + 8, CONV_WIDTH), F32),
                        pltpu.VMEM((7, tm + HIST_ROWS - 8, CONV_WIDTH), F32),
                        pltpu.VMEM((tm, CONV_WIDTH), BF16),
                        pltpu.VMEM((HIST_ROWS + tm, POOL_WIDTH), F32),
                        pltpu.VMEM((tm, POOL_WIDTH), BF16)],
        compiler_params=_params("arbitrary"),
        name="mixers_prompt" if carry else "mixers_sample",
    )(glu, up, hist_c, hist_p, lw["dw_w"], lw["dw_b"], lw["n_g"], lw["n_b"],
      lw["pw_w"], lw["pw_b"], lw["pool_w"], lw["pool_s"])


def _outproj_kernel(h_ref, osb_ref, oc_ref, op_ref, w_ref, g_ref, hn_ref, xn_ref):
    c1 = SB_WIDTH
    c2 = SB_WIDTH + CONV_WIDTH
    mix = (_dot(osb_ref[...], w_ref[0:c1, :]) + _dot(oc_ref[...], w_ref[c1:c2, :])
           + _dot(op_ref[...], w_ref[c2:, :]))
    h = h_ref[...] + mix
    hn_ref[...] = h
    xn_ref[...] = _rmsnorm_rows(h, g_ref[...]).astype(xn_ref.dtype)


def outproj_call(h, osb, oc, op, w_out, g, layer):
    m, d = h.shape
    tm = min(ROW_TILE, m)
    row = lambda i: (i, 0)
    return pl.pallas_call(
        _outproj_kernel,
        grid=(m // tm,),
        in_specs=[pl.BlockSpec((tm, d), row),
                  pl.BlockSpec((tm, SB_WIDTH), row),
                  pl.BlockSpec((tm, CONV_WIDTH), row),
                  pl.BlockSpec((tm, POOL_WIDTH), row),
                  pl.BlockSpec((None, d, d), lambda i: (layer, 0, 0),
                               pipeline_mode=pl.Buffered(1)),
                  pl.BlockSpec((1, d), lambda i: (0, 0))],
        out_specs=[pl.BlockSpec((tm, d), row), pl.BlockSpec((tm, d), row)],
        out_shape=[jax.ShapeDtypeStruct((m, d), F32), jax.ShapeDtypeStruct((m, d), BF16)],
        compiler_params=_params("arbitrary"),
        name="outproj",
    )(h, osb, oc, op, w_out, g.reshape(1, d))


def _mlp_kernel(xn_ref, h_ref, wu_ref, wd_ref, *rest, final):
    if final:
        g_ref, o_ref = rest
    else:
        (o_ref,) = rest
    f = pl.program_id(1)

    @pl.when(f == 0)
    def _():
        o_ref[...] = h_ref[...]

    a = jnp.maximum(_dot(xn_ref[...], wu_ref[...]), 0.0)
    o_ref[...] += _dot((a * a).astype(BF16), wd_ref[...])

    if final:
        @pl.when(f == pl.num_programs(1) - 1)
        def _():
            o_ref[...] = _rmsnorm_rows(o_ref[...], g_ref[...])


def mlp_call(xn, h, w_up, w_down, layer, final_g=None):
    m, d = h.shape
    ff = w_up.shape[2]
    tm = min(MLP_ROW_TILE, m)
    tf = FF_TILE
    row = lambda i, f: (i, 0)
    final = final_g is not None
    in_specs = [pl.BlockSpec((tm, d), row),
                pl.BlockSpec((tm, d), row),
                pl.BlockSpec((None, d, tf), lambda i, f: (layer, 0, f)),
                pl.BlockSpec((None, tf, d), lambda i, f: (layer, f, 0))]
    args = [xn, h, w_up, w_down]
    if final:
        in_specs.append(pl.BlockSpec((1, d), lambda i, f: (0, 0)))
        args.append(final_g.reshape(1, d))
    return pl.pallas_call(
        functools.partial(_mlp_kernel, final=final),
        grid=(m // tm, ff // tf),
        in_specs=in_specs,
        out_specs=pl.BlockSpec((tm, d), row),
        out_shape=jax.ShapeDtypeStruct((m, d), F32),
        compiler_params=_params("arbitrary", "arbitrary"),
        name="mlp_final" if final else "mlp",
    )(*args)


def _pad_hist(hist, rows):
    return jnp.pad(hist, ((0, 0), (rows - hist.shape[1], 0), (0, 0)))


def _block_diag(w):
    g, a, b = w.shape
    eye = jnp.eye(g, dtype=w.dtype)
    return (eye[:, None, :, None] * w[:, :, None, :]).reshape(g * a, g * b)


def _group_trunk(x, layers, big, final_g, sample, caches):
    depth = len(layers)
    h = x
    cs, ps = [], []
    shape = kv_stack_shape(x.shape[0], sample, depth)
    kv = (jnp.zeros(shape, F32), jnp.zeros(shape, F32))
    for l, lw in enumerate(layers):
        if sample:
            cache_k, cache_v, cache_conv, state_pool = caches
            (q, glu, up), kv = inproj_call(h, lw["norm_mix_g"], big["w_in"], True, l, kv)
            osb = attn_sample_call(q, kv[0], kv[1], cache_k, cache_v, l)
            hist_c = _pad_hist(cache_conv[l], HIST_ROWS)
            hist_p = _pad_hist(state_pool[l], HIST_ROWS)
            start = cache_k.shape[3]
        else:
            (qt, kb, vt, glu, up), kv = inproj_call(h, lw["norm_mix_g"], big["w_in"], False,
                                                    l, kv)
            osb = attn_prompt_call(qt, kb, vt)
            hist_c = jnp.zeros((1, HIST_ROWS, CONV_WIDTH), F32)
            hist_p = jnp.zeros((1, HIST_ROWS, POOL_WIDTH), F32)
            start = 0
        oc, op, c_state, p_state = mixers_call(glu, up, hist_c, hist_p, lw, not sample, start)
        h, xn2 = outproj_call(h, osb, oc, op, big["w_out"], lw["norm_mlp_g"], l)
        h = mlp_call(xn2, h, big["w_up"], big["w_down"], l, final_g if l + 1 == depth else None)
        cs.append(c_state)
        ps.append(p_state)
    return h, kv[0], kv[1], jnp.stack(cs), jnp.stack(ps)


def kernel(x_prompt, x_sample, cache_k, cache_v, cache_conv, state_pool, norm_mix_g, w_in,
           conv_dw_w, conv_dw_b, conv_norm_g, conv_norm_b, conv_pw_w, conv_pw_b, pool_w,
           pool_scale, w_out, norm_mlp_g, w_up, w_down, final_norm_g):
    depth = w_in.shape[0]
    bp, seq, d = x_prompt.shape
    bs, dseq, _ = x_sample.shape
    assert bp == 1 and dseq == DEC_SEQ and d == D_MODEL
    layers = []
    for l in range(depth):
        layers.append(dict(
            norm_mix_g=norm_mix_g[l],
            dw_w=jnp.pad(conv_dw_w[l], ((0, HIST_ROWS - CONV_KERNEL), (0, 0))),
            dw_b=conv_dw_b[l].reshape(1, -1),
            n_g=conv_norm_g[l].reshape(1, -1),
            n_b=conv_norm_b[l].reshape(1, -1),
            pw_w=conv_pw_w[l].astype(BF16),
            pw_b=conv_pw_b[l].reshape(1, -1),
            pool_w=_block_diag(pool_w[l]).astype(BF16),
            pool_s=pool_scale[l].reshape(1, -1),
            norm_mlp_g=norm_mlp_g[l],
        ))
    big = dict(w_in=w_in.astype(BF16), w_out=w_out.astype(BF16),
               w_up=w_up.astype(BF16), w_down=w_down.astype(BF16))
    yp, kp, vp, cp, pp = _group_trunk(x_prompt.reshape(seq, d), layers, big, final_norm_g,
                                      False, None)
    ys, ks, vs, cs, ps = _group_trunk(x_sample.reshape(bs * dseq, d), layers, big, final_norm_g, True,
                                      (cache_k, cache_v, cache_conv, state_pool))
    return (yp.reshape(bp, seq, d), ys.reshape(bs, dseq, d),
            kp[:, None], vp[:, None], cp, pp, ks, vs, cs, ps)
```

```python
import functools

import jax
import jax.numpy as jnp
from jax import lax
from jax.experimental import pallas as pl
from jax.experimental.pallas import tpu as pltpu

F32 = jnp.float32
BF16 = jnp.bfloat16

D_MODEL = 2048
SB_WIDTH = 1024
HEAD_DIM = 128
HEADS = SB_WIDTH // HEAD_DIM
CONV_WIDTH = 512
CONV_KERNEL = 31
CONV_GROUPS = 4
POOL_WIDTH = 512
POOL_WINDOWS = (2, 4, 8, 16)
POOL_GROUP = POOL_WIDTH // len(POOL_WINDOWS)
POOL_HIST = max(POOL_WINDOWS) - 1
IN_COLS = 3 * SB_WIDTH + 2 * CONV_WIDTH + POOL_WIDTH
EPS = 1e-6
DEC_SEQ = 64

LOG2E = 1.4426950408889634
Q_SCALE = HEAD_DIM ** -0.5 * LOG2E

LANE = 128
HIST_ROWS = 32
ROW_TILE = 512
FF_TILE = 1024
ATTN_TK = 128
ATTN_VT = 2 * ATTN_TK
SAMPLE_TK = 256
CONV_CHUNK = 64
VMEM_LIMIT = 56 * 1024 * 1024


def _params(*sem):
    return pltpu.CompilerParams(dimension_semantics=sem, vmem_limit_bytes=VMEM_LIMIT)


def _dot(a, b):
    return jnp.dot(a, b, preferred_element_type=F32)


def _softplus2(z):
    e = jnp.exp2(-jnp.abs(z))
    return jnp.maximum(z, 0.0) + jnp.log(1.0 + e) * LOG2E


def _rmsnorm_rows(x, g):
    ms = jnp.mean(x * x, axis=-1, keepdims=True)
    return x * lax.rsqrt(ms + EPS) * g


def _inproj_kernel(h_ref, g_ref, w_ref, *rest, tm, sample, n_alias):
    outs = rest[n_alias:]
    x = _rmsnorm_rows(h_ref[...], g_ref[...]).astype(BF16)
    if sample:
        q_ref, kf_ref, vf_ref, glu_ref, up_ref = outs
        nb = tm // DEC_SEQ
    else:
        qt_ref, kb_ref, vt_ref, kf_ref, vf_ref, glu_ref, up_ref = outs
    pair = 2 * HEAD_DIM
    for hp in range(HEADS // 2):
        c0 = hp * pair
        q2 = _dot(x, w_ref[:, c0:c0 + pair]) * Q_SCALE
        k2 = _dot(x, w_ref[:, SB_WIDTH + c0:SB_WIDTH + c0 + pair])
        v2 = _dot(x, w_ref[:, 2 * SB_WIDTH + c0:2 * SB_WIDTH + c0 + pair])
        for u in range(2):
            h = 2 * hp + u
            sl = slice(u * HEAD_DIM, (u + 1) * HEAD_DIM)
            if sample:
                q_ref[:, h] = q2[:, sl].astype(BF16).reshape(nb, DEC_SEQ, HEAD_DIM)
                kf_ref[:, h] = k2[:, sl].reshape(nb, DEC_SEQ, HEAD_DIM)
                vf_ref[:, h] = v2[:, sl].reshape(nb, DEC_SEQ, HEAD_DIM)
            else:
                qt_ref[h] = q2[:, sl].T.astype(BF16)
                kb_ref[h] = k2[:, sl].astype(BF16)
                kf_ref[h] = k2[:, sl]
                vf_ref[h] = v2[:, sl]
                vt = v2[:, sl].T.astype(BF16)
                for c in range(tm // ATTN_VT):
                    vt_ref[h, c] = vt[:, c * ATTN_VT:(c + 1) * ATTN_VT]
    c0 = 3 * SB_WIDTH
    uc = _dot(x, w_ref[:, c0:c0 + 2 * CONV_WIDTH])
    glu_ref[...] = uc[:, 0:CONV_WIDTH] * jax.nn.sigmoid(uc[:, CONV_WIDTH:])
    up_ref[...] = _dot(x, w_ref[:, c0 + 2 * CONV_WIDTH:])


def kv_stack_shape(m, sample, depth):
    if sample:
        return (depth, m // DEC_SEQ, HEADS, DEC_SEQ, HEAD_DIM)
    return (depth, HEADS, m, HEAD_DIM)


def inproj_call(h, g, w_in, sample, layer, kv_stack):
    depth = kv_stack[0].shape[0]
    m, d = h.shape
    tm = min(ROW_TILE, m)
    if sample:
        nbt = m // DEC_SEQ
        nb = tm // DEC_SEQ
        hshape = (nbt, HEADS, DEC_SEQ, HEAD_DIM)
        hblock = (nb, HEADS, DEC_SEQ, HEAD_DIM)
        kv_shape = jax.ShapeDtypeStruct((depth,) + hshape, F32)
        kv_spec = pl.BlockSpec((None,) + hblock, lambda i: (layer, i, 0, 0, 0))
        out_shape = [jax.ShapeDtypeStruct(hshape, BF16), kv_shape, kv_shape]
        out_specs = [pl.BlockSpec(hblock, lambda i: (i, 0, 0, 0)), kv_spec, kv_spec]
        kv_index = (1, 2)
    else:
        nkb = m // ATTN_VT
        kshape = (HEADS, m, HEAD_DIM)
        kblock = (HEADS, tm, HEAD_DIM)
        kv_shape = jax.ShapeDtypeStruct((depth,) + kshape, F32)
        kv_spec = pl.BlockSpec((None,) + kblock, lambda i: (layer, 0, i, 0))
        out_shape = [jax.ShapeDtypeStruct((HEADS, HEAD_DIM, m), BF16),
                     jax.ShapeDtypeStruct(kshape, BF16),
                     jax.ShapeDtypeStruct((HEADS, nkb, HEAD_DIM, ATTN_VT), BF16),
                     kv_shape, kv_shape]
        out_specs = [pl.BlockSpec((HEADS, HEAD_DIM, tm), lambda i: (0, 0, i)),
                     pl.BlockSpec(kblock, lambda i: (0, i, 0)),
                     pl.BlockSpec((HEADS, tm // ATTN_VT, HEAD_DIM, ATTN_VT),
                                  lambda i: (0, i, 0, 0)),
                     kv_spec, kv_spec]
        kv_index = (3, 4)
    out_shape += [jax.ShapeDtypeStruct((m, CONV_WIDTH), F32),
                  jax.ShapeDtypeStruct((m, POOL_WIDTH), F32)]
    out_specs += [pl.BlockSpec((tm, CONV_WIDTH), lambda i: (i, 0)),
                  pl.BlockSpec((tm, POOL_WIDTH), lambda i: (i, 0))]
    in_specs = [pl.BlockSpec((tm, d), lambda i: (i, 0)),
                pl.BlockSpec((1, d), lambda i: (0, 0)),
                pl.BlockSpec((None, d, IN_COLS), lambda i: (layer, 0, 0),
                             pipeline_mode=pl.Buffered(1))]
    args = [h, g.reshape(1, d), w_in]
    aliases = {}
    for n, stack in enumerate(kv_stack):
        assert stack.shape == kv_shape.shape and stack.dtype == kv_shape.dtype
        aliases[len(args)] = kv_index[n]
        in_specs.append(pl.BlockSpec(memory_space=pl.ANY))
        args.append(stack)
    outs = pl.pallas_call(
        functools.partial(_inproj_kernel, tm=tm, sample=sample, n_alias=len(aliases)),
        grid=(m // tm,),
        in_specs=in_specs,
        out_specs=out_specs,
        out_shape=out_shape,
        input_output_aliases=aliases,
        compiler_params=_params("arbitrary"),
        name="inproj_sample" if sample else "inproj_prompt",
    )(*args)
    kv = tuple(outs[n] for n in kv_index)
    return [o for n, o in enumerate(outs) if n not in kv_index], kv


MASKED_LOGIT = -1e30
STICK_EXHAUSTED = 160.0


def _attn_stage1(k_ref, qt, p, i, masked, a_ref, sp_ref, *, tq, tk):
    for half in (1, 0):
        row0 = pl.multiple_of(p * (2 * tk) + half * tk, tk)
        z = _dot(k_ref[pl.ds(row0, tk), :], qt)
        sp = _softplus2(z)
        a = z - sp
        if masked:
            s_idx = row0 + lax.broadcasted_iota(jnp.int32, (tk, tq), 0)
            t_idx = i * tq + lax.broadcasted_iota(jnp.int32, (tk, tq), 1)
            valid = s_idx < t_idx
            sp = jnp.where(valid, sp, 0.0)
            a = jnp.where(valid, a, MASKED_LOGIT)
        a_ref[half * tk:(half + 1) * tk, :] = a
        sp_ref[half] = sp.astype(BF16)


def _attn_stage2(tri_ref, a_ref, sp_ref, half, carry, *, tk):
    sp = sp_ref[half]
    later = _dot(tri_ref[...], sp)
    w = jnp.exp2(a_ref[half * tk:(half + 1) * tk, :] - later - carry)
    return w.astype(BF16), carry + later[0:1, :] + sp[0:1, :].astype(F32)


ATTN_HEADS_PER_STEP = 4
ATTN_STAGE_BUFS = 2
PV_LAG = 2


def _attn_prompt_kernel(qt_ref, k_ref, vt_ref, tri_ref, o_ref, *scratch, tq, tk):
    i = pl.program_id(1)
    nh = ATTN_HEADS_PER_STEP
    heads = range(nh)
    acc = scratch[0:nh]
    ab = [scratch[nh + b * nh:nh + (b + 1) * nh] for b in range(ATTN_STAGE_BUFS)]
    sp0 = nh * (1 + ATTN_STAGE_BUFS)
    spb = [scratch[sp0 + b * nh:sp0 + (b + 1) * nh] for b in range(ATTN_STAGE_BUFS)]
    s1 = functools.partial(_attn_stage1, tq=tq, tk=tk)
    s2 = functools.partial(_attn_stage2, tk=tk)
    qts = [qt_ref[h] for h in heads]
    for h in heads:
        acc[h][...] = jnp.zeros_like(acc[h])

    def second(buf, p, carry):
        carry = list(carry)
        units = [(h, half) for h in heads for half in (1, 0)]
        ws = {}
        for n in range(len(units) + PV_LAG):
            if n < len(units):
                h, half = units[n]
                ws[units[n]], carry[h] = s2(tri_ref, ab[buf][h], spb[buf][h], half, carry[h])
            if n >= PV_LAG:
                h, half = units[n - PV_LAG]
                vt = vt_ref[h, p, :, half * tk:(half + 1) * tk]
                acc[h][...] += _dot(vt, ws.pop(units[n - PV_LAG]))
        return tuple(carry)

    def step(buf, p, carry):
        for h in heads:
            s1(k_ref.at[h], qts[h], jnp.maximum(p - 1, 0), i, False,
               ab[1 - buf][h], spb[1 - buf][h])
        return second(buf, p, carry)

    for h in heads:
        s1(k_ref.at[h], qts[h], i, i, True, ab[0][h], spb[0][h])

    def live(state):
        m, _, least = state
        return jnp.logical_and(m < (i + 1) // 2, least <= STICK_EXHAUSTED)

    def body(state):
        m, carry, _ = state
        p = i - 2 * m
        carry = step(1, p - 1, step(0, p, carry))
        least = functools.reduce(jnp.minimum, [jnp.min(c) for c in carry])
        return m + 1, carry, least

    zero = tuple(jnp.zeros((1, tq), F32) for _ in heads)
    _, carry, least = lax.while_loop(live, body, (jnp.int32(0), zero, jnp.float32(0.0)))

    @pl.when(jnp.logical_and(i % 2 == 0, least <= STICK_EXHAUSTED))
    def _():
        second(0, 0, carry)

    d = acc[0].shape[0]
    for h in heads:
        o_ref[:, h * d:(h + 1) * d] = acc[h][...].T.astype(o_ref.dtype)


def _tri(tk):
    s = lax.broadcasted_iota(jnp.int32, (tk, tk), 0)
    j = lax.broadcasted_iota(jnp.int32, (tk, tk), 1)
    return (j > s).astype(BF16)


def attn_prompt_call(qt, kb, vt):
    h, d, t = qt.shape
    tq = ATTN_VT
    tk = ATTN_TK
    tri = _tri(tk)
    nh = ATTN_HEADS_PER_STEP
    nstage = nh * ATTN_STAGE_BUFS
    return pl.pallas_call(
        functools.partial(_attn_prompt_kernel, tq=tq, tk=tk),
        grid=(h // nh, t // tq),
        in_specs=[pl.BlockSpec((nh, d, tq), lambda hh, i: (hh, 0, i)),
                  pl.BlockSpec((nh, t, d), lambda hh, i: (hh, 0, 0)),
                  pl.BlockSpec((nh, t // ATTN_VT, d, ATTN_VT), lambda hh, i: (hh, 0, 0, 0)),
                  pl.BlockSpec((tk, tk), lambda hh, i: (0, 0))],
        out_specs=pl.BlockSpec((tq, nh * d), lambda hh, i: (i, hh)),
        out_shape=jax.ShapeDtypeStruct((t, h * d), BF16),
        scratch_shapes=([pltpu.VMEM((d, tq), F32)] * nh
                        + [pltpu.VMEM((2 * tk, tq), F32)] * nstage
                        + [pltpu.VMEM((2, tk, tq), BF16)] * nstage),
        compiler_params=_params("arbitrary", "arbitrary"),
        name="attn_prompt",
    )(qt, kb, vt, tri)


SAMPLE_HEADS_PER_STEP = 4


def _sample_sweep(q_ref, kb_ref, vb_ref, nblk, mask_last, carry, tri_ref, acc_ref,
                  z_ref, sp_ref, cum_ref, w_ref, *, tk):
    nt = (((1,), (1,)), ((), ()))
    nh, ts, d = q_ref.shape
    rows = nh * ts
    tot = nblk * tk
    for h in range(nh):
        z_ref[h * ts:(h + 1) * ts, 0:tot] = lax.dot_general(
            q_ref[h], kb_ref[h, 0:tot, :], nt, preferred_element_type=F32)
    for b in range(nblk):
        cols = slice(b * tk, (b + 1) * tk)
        z = z_ref[:, cols]
        sp = _softplus2(z)
        a = z - sp
        if mask_last and b == nblk - 1:
            t_idx = lax.broadcasted_iota(jnp.int32, (rows, tk), 0) % ts
            s_idx = lax.broadcasted_iota(jnp.int32, (rows, tk), 1)
            valid = s_idx < t_idx
            sp = jnp.where(valid, sp, 0.0)
            a = jnp.where(valid, a, MASKED_LOGIT)
        z_ref[:, cols] = a
        sp_ref[b * rows:(b + 1) * rows, :] = sp.astype(BF16)
    cum_ref[0:nblk * rows, :] = _dot(sp_ref[0:nblk * rows, :], tri_ref[...])
    for b in range(nblk - 1, -1, -1):
        cols = slice(b * tk, (b + 1) * tk)
        later = cum_ref[b * rows:(b + 1) * rows, :]
        w_ref[:, cols] = jnp.exp2(z_ref[:, cols] - later - carry).astype(BF16)
        first = sp_ref[b * rows:(b + 1) * rows, 0:1].astype(F32)
        carry = carry + later[:, 0:1] + first
    for h in range(nh):
        acc_ref[h * ts:(h + 1) * ts, :] += _dot(w_ref[h * ts:(h + 1) * ts, 0:tot],
                                                 vb_ref[h, 0:tot, :])
    return carry


def _attn_sample_kernel(q_ref, kn_ref, vn_ref, kt_ref, vt_ref, ck_hbm, cv_hbm, tri_ref, o_ref,
                        kb_ref, vb_ref, ckf_ref, cvf_ref, z_ref, sp_ref, cum_ref, w_ref,
                        acc_ref, sem, *, past, tk, layer):
    nh, ts, d = q_ref.shape
    rows = nh * ts
    early = past - tk
    sweep = functools.partial(_sample_sweep, q_ref, kb_ref, vb_ref, tri_ref=tri_ref,
                              acc_ref=acc_ref, z_ref=z_ref, sp_ref=sp_ref, cum_ref=cum_ref,
                              w_ref=w_ref, tk=tk)
    acc_ref[...] = jnp.zeros_like(acc_ref)
    for tail, new, dst in ((kt_ref, kn_ref, kb_ref), (vt_ref, vn_ref, vb_ref)):
        for h in range(nh):
            dst[h, 0:tk, :] = tail[h].astype(BF16)
            dst[h, tk:tk + ts, :] = new[h].astype(BF16)
            dst[h, tk + ts:2 * tk, :] = jnp.zeros((tk - ts, d), BF16)
    carry = sweep(2, True, jnp.zeros((rows, 1), F32))

    @pl.when(jnp.min(carry) <= STICK_EXHAUSTED)
    def _():
        b = pl.program_id(0)
        h0 = pl.multiple_of(pl.program_id(1) * nh, nh)
        copies = [pltpu.make_async_copy(
            src.at[layer, b, pl.ds(h0, nh), pl.ds(0, early), :], dst, sem.at[n])
            for n, (src, dst) in enumerate(((ck_hbm, ckf_ref), (cv_hbm, cvf_ref)))]
        for c in copies:
            c.start()
        for c in copies:
            c.wait()
        for h in range(nh):
            kb_ref[h, 0:early, :] = ckf_ref[h].astype(BF16)
            vb_ref[h, 0:early, :] = cvf_ref[h].astype(BF16)
        sweep(early // tk, False, carry)

    for h in range(nh):
        o_ref[:, h * d:(h + 1) * d] = acc_ref[h * ts:(h + 1) * ts, :].astype(o_ref.dtype)


def attn_sample_call(q, k_stack, v_stack, cache_k, cache_v, layer):
    b, h, ts, d = q.shape
    past = cache_k.shape[3]
    tk = SAMPLE_TK
    nh = SAMPLE_HEADS_PER_STEP
    assert past % tk == 0 and past >= 2 * tk and ts <= tk
    rows = nh * ts
    early = past - tk
    span = max(early, 2 * tk)
    tri = _tri(tk).T
    q_spec = pl.BlockSpec((None, nh, ts, d), lambda bb, hh: (bb, hh, 0, 0))
    new_spec = pl.BlockSpec((None, None, nh, ts, d), lambda bb, hh: (layer, bb, hh, 0, 0))
    tail_spec = pl.BlockSpec((None, None, nh, tk, d),
                             lambda bb, hh: (layer, bb, hh, past // tk - 1, 0))
    any_spec = pl.BlockSpec(memory_space=pl.ANY)
    return pl.pallas_call(
        functools.partial(_attn_sample_kernel, past=past, tk=tk, layer=layer),
        grid=(b, h // nh),
        in_specs=[q_spec, new_spec, new_spec, tail_spec, tail_spec, any_spec, any_spec,
                  pl.BlockSpec((tk, tk), lambda bb, hh: (0, 0))],
        out_specs=pl.BlockSpec((ts, nh * d), lambda bb, hh: (bb, hh)),
        out_shape=jax.ShapeDtypeStruct((b * ts, h * d), BF16),
        scratch_shapes=[pltpu.VMEM((nh, span, d), BF16), pltpu.VMEM((nh, span, d), BF16),
                        pltpu.VMEM((nh, early, d), F32), pltpu.VMEM((nh, early, d), F32),
                        pltpu.VMEM((rows, span), F32),
                        pltpu.VMEM((span // tk * rows, tk), BF16),
                        pltpu.VMEM((span // tk * rows, tk), F32),
                        pltpu.VMEM((rows, span), BF16),
                        pltpu.VMEM((rows, d), F32),
                        pltpu.SemaphoreType.DMA((2,))],
        compiler_params=_params("arbitrary", "arbitrary"),
        name="attn_sample",
    )(q, k_stack, v_stack, cache_k, cache_v, cache_k, cache_v, tri)


def _mixers_kernel(glu_ref, up_ref, hc_ref, hp_ref, dww_ref, dwb_ref, ng_ref, nb_ref,
                   pww_ref, pwb_ref, plw_ref, pls_ref,
                   oc_ref, op_ref, cs_ref, ps_ref,
                   ext_ref, sh_ref, yn_ref, pext_ref, pooled_ref,
                   *, tm, carry, start_pos):
    i = pl.program_id(0)
    hdr = HIST_ROWS

    def load_hist():
        ext_ref[0:hdr, :] = hc_ref[...]
        pext_ref[0:hdr, :] = hp_ref[...]

    if carry:
        pl.when(i == 0)(load_hist)
    else:
        load_hist()

    ext_ref[hdr:hdr + tm, :] = glu_ref[...]
    pext_ref[hdr:hdr + tm, :] = up_ref[...]
    cs_ref[...] = ext_ref[pl.ds(hdr + tm - (CONV_KERNEL - 1), CONV_KERNEL - 1), :]
    ps_ref[...] = pext_ref[pl.ds(hdr + tm - POOL_HIST, POOL_HIST), :]

    sh_rows = sh_ref.shape[1]
    for r in range(1, 8):
        sh_ref[r - 1] = ext_ref[pl.ds(r, sh_rows), :]

    rc = min(CONV_CHUNK, tm)
    first = hdr - (CONV_KERNEL - 1)

    def conv_chunk(c, _):
        base = pl.multiple_of(c * rc, rc)
        acc = jnp.zeros((rc, CONV_WIDTH), F32)
        for r in range(8):
            taps = [(j, ((first + j) // 8) * 8) for j in range(CONV_KERNEL)
                    if (first + j) % 8 == r]
            lo = min(off for _, off in taps)
            hi = max(off for _, off in taps)
            src = ext_ref if r == 0 else sh_ref.at[r - 1]
            slab = src[pl.ds(base + lo, rc + hi - lo), :]
            for j, off in taps:
                acc = acc + slab[off - lo:off - lo + rc] * dww_ref[j:j + 1, :]
        y = acc + dwb_ref[...]
        gw = CONV_WIDTH // CONV_GROUPS
        for g in range(CONV_GROUPS):
            sl = slice(g * gw, (g + 1) * gw)
            yg = y[:, sl]
            mu = jnp.mean(yg, axis=-1, keepdims=True)
            dv = yg - mu
            var = jnp.mean(dv * dv, axis=-1, keepdims=True)
            yn = dv * lax.rsqrt(var + EPS) * ng_ref[:, sl] + nb_ref[:, sl]
            yn = yn * jax.nn.sigmoid(yn)
            yn_ref[pl.ds(base, rc), sl] = yn.astype(BF16)
        return 0

    lax.fori_loop(0, tm // rc, conv_chunk, 0)
    oc_ref[...] = (_dot(yn_ref[...], pww_ref[...]) + pwb_ref[...]).astype(oc_ref.dtype)

    pc = min(LANE, tm)
    for c in range(tm // pc):
        b0 = hdr + c * pc
        row = lax.broadcasted_iota(jnp.int32, (pc, POOL_GROUP), 0) + c * pc
        pos = row + (start_pos + (i * tm if carry else 0))
        for g, w in enumerate(POOL_WINDOWS):
            sl = slice(g * POOL_GROUP, (g + 1) * POOL_GROUP)
            u = pext_ref[b0:b0 + pc, sl]
            s = u
            for dlt in range(1, w):
                s = s + pext_ref[pl.ds(b0 - dlt, pc), sl]
            cnt = jnp.minimum(w, pos + 1).astype(F32)
            pooled_ref[c * pc:(c + 1) * pc, sl] = (s / cnt - u).astype(BF16)
    op_ref[...] = (_dot(pooled_ref[...], plw_ref[...]) * pls_ref[...]).astype(op_ref.dtype)

    if carry:
        ext_ref[0:hdr, :] = ext_ref[tm:tm + hdr, :]
        pext_ref[0:hdr, :] = pext_ref[tm:tm + hdr, :]


def mixers_call(glu, up, hist_c, hist_p, lw, carry, start_pos):
    m = glu.shape[0]
    nseq = hist_c.shape[0]
    tm = min(ROW_TILE, m) if carry else m // nseq
    row = lambda i: (i, 0)
    const = lambda i: (0, 0)
    seq = (lambda i: (0, 0, 0)) if carry else (lambda i: (i, 0, 0))
    vec = pl.BlockSpec((1, CONV_WIDTH), const)
    sq = pl.BlockSpec((CONV_WIDTH, CONV_WIDTH), const)
    return pl.pallas_call(
        functools.partial(_mixers_kernel, tm=tm, carry=carry, start_pos=start_pos),
        grid=(m // tm,),
        in_specs=[pl.BlockSpec((tm, CONV_WIDTH), row),
                  pl.BlockSpec((tm, POOL_WIDTH), row),
                  pl.BlockSpec((None, HIST_ROWS, CONV_WIDTH), seq),
                  pl.BlockSpec((None, HIST_ROWS, POOL_WIDTH), seq),
                  pl.BlockSpec((HIST_ROWS, CONV_WIDTH), const),
                  vec, vec, vec, sq, vec, sq, vec],
        out_specs=[pl.BlockSpec((tm, CONV_WIDTH), row),
                   pl.BlockSpec((tm, POOL_WIDTH), row),
                   pl.BlockSpec((None, CONV_KERNEL - 1, CONV_WIDTH), seq),
                   pl.BlockSpec((None, POOL_HIST, POOL_WIDTH), seq)],
        out_shape=[jax.ShapeDtypeStruct((m, CONV_WIDTH), BF16),
                   jax.ShapeDtypeStruct((m, POOL_WIDTH), BF16),
                   jax.ShapeDtypeStruct((nseq, CONV_KERNEL - 1, CONV_WIDTH), F32),
                   jax.ShapeDtypeStruct((nseq, POOL_HIST, POOL_WIDTH), F32)],
        scratch_shapes=[pltpu.VMEM((HIST_ROWS + tm + 8, CONV_WIDTH), F32),
                        pltpu.VMEM((7, tm + HIST_ROWS - 8, CONV_WIDTH), F32),
                        pltpu.VMEM((tm, CONV_WIDTH), BF16),
                        pltpu.VMEM((HIST_ROWS + tm, POOL_WIDTH), F32),
                        pltpu.VMEM((tm, POOL_WIDTH), BF16)],
        compiler_params=_params("arbitrary"),
        name="mixers_prompt" if carry else "mixers_sample",
    )(glu, up, hist_c, hist_p, lw["dw_w"], lw["dw_b"], lw["n_g"], lw["n_b"],
      lw["pw_w"], lw["pw_b"], lw["pool_w"], lw["pool_s"])


def _outproj_kernel(h_ref, osb_ref, oc_ref, op_ref, w_ref, g_ref, hn_ref, xn_ref):
    c1 = SB_WIDTH
    c2 = SB_WIDTH + CONV_WIDTH
    mix = (_dot(osb_ref[...], w_ref[0:c1, :]) + _dot(oc_ref[...], w_ref[c1:c2, :])
           + _dot(op_ref[...], w_ref[c2:, :]))
    h = h_ref[...] + mix
    hn_ref[...] = h
    xn_ref[...] = _rmsnorm_rows(h, g_ref[...]).astype(xn_ref.dtype)


def outproj_call(h, osb, oc, op, w_out, g, layer):
    m, d = h.shape
    tm = min(ROW_TILE, m)
    row = lambda i: (i, 0)
    return pl.pallas_call(
        _outproj_kernel,
        grid=(m // tm,),
        in_specs=[pl.BlockSpec((tm, d), row),
                  pl.BlockSpec((tm, SB_WIDTH), row),
                  pl.BlockSpec((tm, CONV_WIDTH), row),
                  pl.BlockSpec((tm, POOL_WIDTH), row),
                  pl.BlockSpec((None, d, d), lambda i: (layer, 0, 0),
                               pipeline_mode=pl.Buffered(1)),
                  pl.BlockSpec((1, d), lambda i: (0, 0))],
        out_specs=[pl.BlockSpec((tm, d), row), pl.BlockSpec((tm, d), row)],
        out_shape=[jax.ShapeDtypeStruct((m, d), F32), jax.ShapeDtypeStruct((m, d), BF16)],
        compiler_params=_params("arbitrary"),
        name="outproj",
    )(h, osb, oc, op, w_out, g.reshape(1, d))


def _mlp_kernel(xn_ref, h_ref, wu_ref, wd_ref, *rest, final):
    if final:
        g_ref, o_ref, acc_ref = rest
    else:
        o_ref, acc_ref = rest
    f = pl.program_id(1)

    @pl.when(f == 0)
    def _():
        acc_ref[...] = jnp.zeros_like(acc_ref)

    a = jnp.maximum(_dot(xn_ref[...], wu_ref[...]), 0.0)
    acc_ref[...] += _dot((a * a).astype(BF16), wd_ref[...])

    @pl.when(f == pl.num_programs(1) - 1)
    def _():
        h = h_ref[...] + acc_ref[...]
        o_ref[...] = _rmsnorm_rows(h, g_ref[...]) if final else h


def mlp_call(xn, h, w_up, w_down, layer, final_g=None):
    m, d = h.shape
    ff = w_up.shape[2]
    tm = min(ROW_TILE, m)
    tf = FF_TILE
    row = lambda i, f: (i, 0)
    final = final_g is not None
    in_specs = [pl.BlockSpec((tm, d), row),
                pl.BlockSpec((tm, d), row),
                pl.BlockSpec((None, d, tf), lambda i, f: (layer, 0, f)),
                pl.BlockSpec((None, tf, d), lambda i, f: (layer, f, 0))]
    args = [xn, h, w_up, w_down]
    if final:
        in_specs.append(pl.BlockSpec((1, d), lambda i, f: (0, 0)))
        args.append(final_g.reshape(1, d))
    return pl.pallas_call(
        functools.partial(_mlp_kernel, final=final),
        grid=(m // tm, ff // tf),
        in_specs=in_specs,
        out_specs=pl.BlockSpec((tm, d), row),
        out_shape=jax.ShapeDtypeStruct((m, d), F32),
        scratch_shapes=[pltpu.VMEM((tm, d), F32)],
        compiler_params=_params("arbitrary", "arbitrary"),
        name="mlp_final" if final else "mlp",
    )(*args)


def _pad_hist(hist, rows):
    return jnp.pad(hist, ((0, 0), (rows - hist.shape[1], 0), (0, 0)))


def _block_diag(w):
    g, a, b = w.shape
    eye = jnp.eye(g, dtype=w.dtype)
    return (eye[:, None, :, None] * w[:, :, None, :]).reshape(g * a, g * b)


def _group_trunk(x, layers, big, final_g, sample, caches):
    depth = len(layers)
    h = x
    cs, ps = [], []
    shape = kv_stack_shape(x.shape[0], sample, depth)
    kv = (jnp.zeros(shape, F32), jnp.zeros(shape, F32))
    for l, lw in enumerate(layers):
        if sample:
            cache_k, cache_v, cache_conv, state_pool = caches
            (q, glu, up), kv = inproj_call(h, lw["norm_mix_g"], big["w_in"], True, l, kv)
            osb = attn_sample_call(q, kv[0], kv[1], cache_k, cache_v, l)
            hist_c = _pad_hist(cache_conv[l], HIST_ROWS)
            hist_p = _pad_hist(state_pool[l], HIST_ROWS)
            start = cache_k.shape[3]
        else:
            (qt, kb, vt, glu, up), kv = inproj_call(h, lw["norm_mix_g"], big["w_in"], False,
                                                    l, kv)
            osb = attn_prompt_call(qt, kb, vt)
            hist_c = jnp.zeros((1, HIST_ROWS, CONV_WIDTH), F32)
            hist_p = jnp.zeros((1, HIST_ROWS, POOL_WIDTH), F32)
            start = 0
        oc, op, c_state, p_state = mixers_call(glu, up, hist_c, hist_p, lw, not sample, start)
        h, xn2 = outproj_call(h, osb, oc, op, big["w_out"], lw["norm_mlp_g"], l)
        h = mlp_call(xn2, h, big["w_up"], big["w_down"], l, final_g if l + 1 == depth else None)
        cs.append(c_state)
        ps.append(p_state)
    return h, kv[0], kv[1], jnp.stack(cs), jnp.stack(ps)


def kernel(x_prompt, x_sample, cache_k, cache_v, cache_conv, state_pool, norm_mix_g, w_in,
           conv_dw_w, conv_dw_b, conv_norm_g, conv_norm_b, conv_pw_w, conv_pw_b, pool_w,
           pool_scale, w_out, norm_mlp_g, w_up, w_down, final_norm_g):
    depth = w_in.shape[0]
    bp, seq, d = x_prompt.shape
    bs, dseq, _ = x_sample.shape
    assert bp == 1 and dseq == DEC_SEQ and d == D_MODEL
    layers = []
    for l in range(depth):
        layers.append(dict(
            norm_mix_g=norm_mix_g[l],
            dw_w=jnp.pad(conv_dw_w[l], ((0, HIST_ROWS - CONV_KERNEL), (0, 0))),
            dw_b=conv_dw_b[l].reshape(1, -1),
            n_g=conv_norm_g[l].reshape(1, -1),
            n_b=conv_norm_b[l].reshape(1, -1),
            pw_w=conv_pw_w[l].astype(BF16),
            pw_b=conv_pw_b[l].reshape(1, -1),
            pool_w=_block_diag(pool_w[l]).astype(BF16),
            pool_s=pool_scale[l].reshape(1, -1),
            norm_mlp_g=norm_mlp_g[l],
        ))
    big = dict(w_in=w_in.astype(BF16), w_out=w_out.astype(BF16),
               w_up=w_up.astype(BF16), w_down=w_down.astype(BF16))
    yp, kp, vp, cp, pp = _group_trunk(x_prompt.reshape(seq, d), layers, big, final_norm_g,
                                      False, None)
    ys, ks, vs, cs, ps = _group_trunk(x_sample.reshape(bs * dseq, d), layers, big, final_norm_g, True,
                                      (cache_k, cache_v, cache_conv, state_pool))
    return (yp.reshape(bp, seq, d), ys.reshape(bs, dseq, d),
            kp[:, None], vp[:, None], cp, pp, ks, vs, cs, ps)
```

```python
import functools

import jax
import jax.numpy as jnp
from jax import lax
from jax.experimental import pallas as pl
from jax.experimental.pallas import tpu as pltpu

F32 = jnp.float32
BF16 = jnp.bfloat16

D_MODEL = 2048
SB_WIDTH = 1024
HEAD_DIM = 128
HEADS = SB_WIDTH // HEAD_DIM
CONV_WIDTH = 512
CONV_KERNEL = 31
CONV_GROUPS = 4
POOL_WIDTH = 512
POOL_WINDOWS = (2, 4, 8, 16)
POOL_GROUP = POOL_WIDTH // len(POOL_WINDOWS)
POOL_HIST = max(POOL_WINDOWS) - 1
IN_COLS = 3 * SB_WIDTH + 2 * CONV_WIDTH + POOL_WIDTH
EPS = 1e-6
DEC_SEQ = 64

LOG2E = 1.4426950408889634
Q_SCALE = HEAD_DIM ** -0.5 * LOG2E

LANE = 128
HIST_ROWS = 32
ROW_TILE = 512
FF_TILE = 1024
ATTN_TK = 128
ATTN_VT = 2 * ATTN_TK
SAMPLE_TK = 256
CONV_CHUNK = 64
VMEM_LIMIT = 56 * 1024 * 1024


def _params(*sem):
    return pltpu.CompilerParams(dimension_semantics=sem, vmem_limit_bytes=VMEM_LIMIT)


def _dot(a, b):
    return jnp.dot(a, b, preferred_element_type=F32)


def _softplus2(z):
    e = jnp.exp2(-jnp.abs(z))
    return jnp.maximum(z, 0.0) + jnp.log(1.0 + e) * LOG2E


def _rmsnorm_rows(x, g):
    ms = jnp.mean(x * x, axis=-1, keepdims=True)
    return x * lax.rsqrt(ms + EPS) * g


def _inproj_kernel(h_ref, g_ref, w_ref, *rest, tm, sample, n_alias):
    outs = rest[n_alias:]
    x = _rmsnorm_rows(h_ref[...], g_ref[...]).astype(BF16)
    if sample:
        q_ref, kf_ref, vf_ref, glu_ref, up_ref = outs
        nb = tm // DEC_SEQ
    else:
        qt_ref, kb_ref, vt_ref, kf_ref, vf_ref, glu_ref, up_ref = outs
    pair = 2 * HEAD_DIM
    for hp in range(HEADS // 2):
        c0 = hp * pair
        q2 = _dot(x, w_ref[:, c0:c0 + pair]) * Q_SCALE
        k2 = _dot(x, w_ref[:, SB_WIDTH + c0:SB_WIDTH + c0 + pair])
        v2 = _dot(x, w_ref[:, 2 * SB_WIDTH + c0:2 * SB_WIDTH + c0 + pair])
        for u in range(2):
            h = 2 * hp + u
            sl = slice(u * HEAD_DIM, (u + 1) * HEAD_DIM)
            if sample:
                q_ref[:, h] = q2[:, sl].astype(BF16).reshape(nb, DEC_SEQ, HEAD_DIM)
                kf_ref[:, h] = k2[:, sl].reshape(nb, DEC_SEQ, HEAD_DIM)
                vf_ref[:, h] = v2[:, sl].reshape(nb, DEC_SEQ, HEAD_DIM)
            else:
                qt_ref[h] = q2[:, sl].T.astype(BF16)
                kb_ref[h] = k2[:, sl].astype(BF16)
                kf_ref[h] = k2[:, sl]
                vf_ref[h] = v2[:, sl]
                vt = v2[:, sl].T.astype(BF16)
                for c in range(tm // ATTN_VT):
                    vt_ref[h, c] = vt[:, c * ATTN_VT:(c + 1) * ATTN_VT]
    c0 = 3 * SB_WIDTH
    uc = _dot(x, w_ref[:, c0:c0 + 2 * CONV_WIDTH])
    glu_ref[...] = uc[:, 0:CONV_WIDTH] * jax.nn.sigmoid(uc[:, CONV_WIDTH:])
    up_ref[...] = _dot(x, w_ref[:, c0 + 2 * CONV_WIDTH:])


def kv_stack_shape(m, sample, depth):
    if sample:
        return (depth, m // DEC_SEQ, HEADS, DEC_SEQ, HEAD_DIM)
    return (depth, HEADS, m, HEAD_DIM)


def inproj_call(h, g, w_in, sample, layer, kv_stack):
    depth = kv_stack[0].shape[0]
    m, d = h.shape
    tm = min(ROW_TILE, m)
    if sample:
        nbt = m // DEC_SEQ
        nb = tm // DEC_SEQ
        hshape = (nbt, HEADS, DEC_SEQ, HEAD_DIM)
        hblock = (nb, HEADS, DEC_SEQ, HEAD_DIM)
        kv_shape = jax.ShapeDtypeStruct((depth,) + hshape, F32)
        kv_spec = pl.BlockSpec((None,) + hblock, lambda i: (layer, i, 0, 0, 0))
        out_shape = [jax.ShapeDtypeStruct(hshape, BF16), kv_shape, kv_shape]
        out_specs = [pl.BlockSpec(hblock, lambda i: (i, 0, 0, 0)), kv_spec, kv_spec]
        kv_index = (1, 2)
    else:
        nkb = m // ATTN_VT
        kshape = (HEADS, m, HEAD_DIM)
        kblock = (HEADS, tm, HEAD_DIM)
        kv_shape = jax.ShapeDtypeStruct((depth,) + kshape, F32)
        kv_spec = pl.BlockSpec((None,) + kblock, lambda i: (layer, 0, i, 0))
        out_shape = [jax.ShapeDtypeStruct((HEADS, HEAD_DIM, m), BF16),
                     jax.ShapeDtypeStruct(kshape, BF16),
                     jax.ShapeDtypeStruct((HEADS, nkb, HEAD_DIM, ATTN_VT), BF16),
                     kv_shape, kv_shape]
        out_specs = [pl.BlockSpec((HEADS, HEAD_DIM, tm), lambda i: (0, 0, i)),
                     pl.BlockSpec(kblock, lambda i: (0, i, 0)),
                     pl.BlockSpec((HEADS, tm // ATTN_VT, HEAD_DIM, ATTN_VT),
                                  lambda i: (0, i, 0, 0)),
                     kv_spec, kv_spec]
        kv_index = (3, 4)
    out_shape += [jax.ShapeDtypeStruct((m, CONV_WIDTH), F32),
                  jax.ShapeDtypeStruct((m, POOL_WIDTH), F32)]
    out_specs += [pl.BlockSpec((tm, CONV_WIDTH), lambda i: (i, 0)),
                  pl.BlockSpec((tm, POOL_WIDTH), lambda i: (i, 0))]
    in_specs = [pl.BlockSpec((tm, d), lambda i: (i, 0)),
                pl.BlockSpec((1, d), lambda i: (0, 0)),
                pl.BlockSpec((None, d, IN_COLS), lambda i: (layer, 0, 0),
                             pipeline_mode=pl.Buffered(1))]
    args = [h, g.reshape(1, d), w_in]
    aliases = {}
    for n, stack in enumerate(kv_stack):
        assert stack.shape == kv_shape.shape and stack.dtype == kv_shape.dtype
        aliases[len(args)] = kv_index[n]
        in_specs.append(pl.BlockSpec(memory_space=pl.ANY))
        args.append(stack)
    outs = pl.pallas_call(
        functools.partial(_inproj_kernel, tm=tm, sample=sample, n_alias=len(aliases)),
        grid=(m // tm,),
        in_specs=in_specs,
        out_specs=out_specs,
        out_shape=out_shape,
        input_output_aliases=aliases,
        compiler_params=_params("arbitrary"),
        name="inproj_sample" if sample else "inproj_prompt",
    )(*args)
    kv = tuple(outs[n] for n in kv_index)
    return [o for n, o in enumerate(outs) if n not in kv_index], kv


MASKED_LOGIT = -1e30
STICK_EXHAUSTED = 160.0


def _attn_stage1(k_ref, qt, p, i, masked, a_ref, sp_ref, *, tq, tk):
    for half in (1, 0):
        row0 = pl.multiple_of(p * (2 * tk) + half * tk, tk)
        rows = slice(half * tk, (half + 1) * tk)
        lane0 = tq - tk if masked and half == 1 else 0
        if lane0:
            a_ref[rows, 0:lane0] = jnp.full((tk, lane0), MASKED_LOGIT, F32)
            sp_ref[half, :, 0:lane0] = jnp.zeros((tk, lane0), BF16)
        z = _dot(k_ref[pl.ds(row0, tk), :], qt[:, lane0:])
        sp = _softplus2(z)
        a = z - sp
        if masked:
            s_idx = row0 + lax.broadcasted_iota(jnp.int32, z.shape, 0)
            t_idx = i * tq + lane0 + lax.broadcasted_iota(jnp.int32, z.shape, 1)
            valid = s_idx < t_idx
            sp = jnp.where(valid, sp, 0.0)
            a = jnp.where(valid, a, MASKED_LOGIT)
        a_ref[rows, lane0:] = a
        sp_ref[half, :, lane0:] = sp.astype(BF16)


def _attn_stage2(tri_ref, a_ref, sp_ref, half, carry, *, tk):
    sp = sp_ref[half]
    later = _dot(tri_ref[...], sp)
    w = jnp.exp2(a_ref[half * tk:(half + 1) * tk, :] - later - carry)
    return w.astype(BF16), carry + later[0:1, :] + sp[0:1, :].astype(F32)


ATTN_HEADS_PER_STEP = 4
ATTN_STAGE_BUFS = 2
PV_LAG = 4


def _attn_prompt_kernel(qt_ref, k_ref, vt_ref, tri_ref, o_ref, *scratch, tq, tk):
    i = pl.program_id(1)
    nh = ATTN_HEADS_PER_STEP
    heads = range(nh)
    acc = scratch[0:nh]
    ab = [scratch[nh + b * nh:nh + (b + 1) * nh] for b in range(ATTN_STAGE_BUFS)]
    sp0 = nh * (1 + ATTN_STAGE_BUFS)
    spb = [scratch[sp0 + b * nh:sp0 + (b + 1) * nh] for b in range(ATTN_STAGE_BUFS)]
    s1 = functools.partial(_attn_stage1, tq=tq, tk=tk)
    s2 = functools.partial(_attn_stage2, tk=tk)
    qts = [qt_ref[h] for h in heads]
    for h in heads:
        acc[h][...] = jnp.zeros_like(acc[h])

    def second(buf, p, carry):
        carry = list(carry)
        units = [(h, half) for h in heads for half in (1, 0)]
        ws = {}
        for n in range(len(units) + PV_LAG):
            if n < len(units):
                h, half = units[n]
                ws[units[n]], carry[h] = s2(tri_ref, ab[buf][h], spb[buf][h], half, carry[h])
            if n >= PV_LAG:
                h, half = units[n - PV_LAG]
                vt = vt_ref[h, p, :, half * tk:(half + 1) * tk]
                acc[h][...] += _dot(vt, ws.pop(units[n - PV_LAG]))
        return tuple(carry)

    def step(buf, p, carry):
        for h in heads:
            s1(k_ref.at[h], qts[h], jnp.maximum(p - 1, 0), i, False,
               ab[1 - buf][h], spb[1 - buf][h])
        return second(buf, p, carry)

    for h in heads:
        s1(k_ref.at[h], qts[h], i, i, True, ab[0][h], spb[0][h])

    def live(state):
        m, _, least = state
        return jnp.logical_and(m < (i + 1) // 2, least <= STICK_EXHAUSTED)

    def body(state):
        m, carry, _ = state
        p = i - 2 * m
        carry = step(1, p - 1, step(0, p, carry))
        least = functools.reduce(jnp.minimum, [jnp.min(c) for c in carry])
        return m + 1, carry, least

    zero = tuple(jnp.zeros((1, tq), F32) for _ in heads)
    _, carry, least = lax.while_loop(live, body, (jnp.int32(0), zero, jnp.float32(0.0)))

    @pl.when(jnp.logical_and(i % 2 == 0, least <= STICK_EXHAUSTED))
    def _():
        second(0, 0, carry)

    d = acc[0].shape[0]
    for h in heads:
        o_ref[:, h * d:(h + 1) * d] = acc[h][...].T.astype(o_ref.dtype)


def _tri(tk):
    s = lax.broadcasted_iota(jnp.int32, (tk, tk), 0)
    j = lax.broadcasted_iota(jnp.int32, (tk, tk), 1)
    return (j > s).astype(BF16)


def attn_prompt_call(qt, kb, vt):
    h, d, t = qt.shape
    tq = ATTN_VT
    tk = ATTN_TK
    tri = _tri(tk)
    nh = ATTN_HEADS_PER_STEP
    nstage = nh * ATTN_STAGE_BUFS
    return pl.pallas_call(
        functools.partial(_attn_prompt_kernel, tq=tq, tk=tk),
        grid=(h // nh, t // tq),
        in_specs=[pl.BlockSpec((nh, d, tq), lambda hh, i: (hh, 0, i)),
                  pl.BlockSpec((nh, t, d), lambda hh, i: (hh, 0, 0)),
                  pl.BlockSpec((nh, t // ATTN_VT, d, ATTN_VT), lambda hh, i: (hh, 0, 0, 0)),
                  pl.BlockSpec((tk, tk), lambda hh, i: (0, 0))],
        out_specs=pl.BlockSpec((tq, nh * d), lambda hh, i: (i, hh)),
        out_shape=jax.ShapeDtypeStruct((t, h * d), BF16),
        scratch_shapes=([pltpu.VMEM((d, tq), F32)] * nh
                        + [pltpu.VMEM((2 * tk, tq), F32)] * nstage
                        + [pltpu.VMEM((2, tk, tq), BF16)] * nstage),
        compiler_params=_params("arbitrary", "arbitrary"),
        name="attn_prompt",
    )(qt, kb, vt, tri)


SAMPLE_HEADS_PER_STEP = 4


def _sample_sweep(q_ref, kb_ref, vb_ref, nblk, mask_last, carry, tri_ref, acc_ref,
                  z_ref, sp_ref, cum_ref, w_ref, *, tk):
    nt = (((1,), (1,)), ((), ()))
    nh, ts, d = q_ref.shape
    rows = nh * ts
    tot = nblk * tk
    for h in range(nh):
        z_ref[h * ts:(h + 1) * ts, 0:tot] = lax.dot_general(
            q_ref[h], kb_ref[h, 0:tot, :], nt, preferred_element_type=F32)
    for b in range(nblk):
        cols = slice(b * tk, (b + 1) * tk)
        z = z_ref[:, cols]
        sp = _softplus2(z)
        a = z - sp
        if mask_last and b == nblk - 1:
            t_idx = lax.broadcasted_iota(jnp.int32, (rows, tk), 0) % ts
            s_idx = lax.broadcasted_iota(jnp.int32, (rows, tk), 1)
            valid = s_idx < t_idx
            sp = jnp.where(valid, sp, 0.0)
            a = jnp.where(valid, a, MASKED_LOGIT)
        z_ref[:, cols] = a
        sp_ref[b * rows:(b + 1) * rows, :] = sp.astype(BF16)
    cum_ref[0:nblk * rows, :] = _dot(sp_ref[0:nblk * rows, :], tri_ref[...])
    for b in range(nblk - 1, -1, -1):
        cols = slice(b * tk, (b + 1) * tk)
        later = cum_ref[b * rows:(b + 1) * rows, :]
        w_ref[:, cols] = jnp.exp2(z_ref[:, cols] - later - carry).astype(BF16)
        first = sp_ref[b * rows:(b + 1) * rows, 0:1].astype(F32)
        carry = carry + later[:, 0:1] + first
    for h in range(nh):
        acc_ref[h * ts:(h + 1) * ts, :] += _dot(w_ref[h * ts:(h + 1) * ts, 0:tot],
                                                 vb_ref[h, 0:tot, :])
    return carry


def _attn_sample_kernel(q_ref, kn_ref, vn_ref, kt_ref, vt_ref, ck_hbm, cv_hbm, tri_ref, o_ref,
                        kb_ref, vb_ref, ckf_ref, cvf_ref, z_ref, sp_ref, cum_ref, w_ref,
                        acc_ref, sem, *, past, tk, layer):
    nh, ts, d = q_ref.shape
    rows = nh * ts
    early = past - tk
    sweep = functools.partial(_sample_sweep, q_ref, kb_ref, vb_ref, tri_ref=tri_ref,
                              acc_ref=acc_ref, z_ref=z_ref, sp_ref=sp_ref, cum_ref=cum_ref,
                              w_ref=w_ref, tk=tk)
    acc_ref[...] = jnp.zeros_like(acc_ref)
    for tail, new, dst in ((kt_ref, kn_ref, kb_ref), (vt_ref, vn_ref, vb_ref)):
        for h in range(nh):
            dst[h, 0:tk, :] = tail[h].astype(BF16)
            dst[h, tk:tk + ts, :] = new[h].astype(BF16)
            dst[h, tk + ts:2 * tk, :] = jnp.zeros((tk - ts, d), BF16)
    carry = sweep(2, True, jnp.zeros((rows, 1), F32))

    @pl.when(jnp.min(carry) <= STICK_EXHAUSTED)
    def _():
        b = pl.program_id(0)
        h0 = pl.multiple_of(pl.program_id(1) * nh, nh)
        copies = [pltpu.make_async_copy(
            src.at[layer, b, pl.ds(h0, nh), pl.ds(0, early), :], dst, sem.at[n])
            for n, (src, dst) in enumerate(((ck_hbm, ckf_ref), (cv_hbm, cvf_ref)))]
        for c in copies:
            c.start()
        for c in copies:
            c.wait()
        for h in range(nh):
            kb_ref[h, 0:early, :] = ckf_ref[h].astype(BF16)
            vb_ref[h, 0:early, :] = cvf_ref[h].astype(BF16)
        sweep(early // tk, False, carry)

    for h in range(nh):
        o_ref[:, h * d:(h + 1) * d] = acc_ref[h * ts:(h + 1) * ts, :].astype(o_ref.dtype)


def attn_sample_call(q, k_stack, v_stack, cache_k, cache_v, layer):
    b, h, ts, d = q.shape
    past = cache_k.shape[3]
    tk = SAMPLE_TK
    nh = SAMPLE_HEADS_PER_STEP
    assert past % tk == 0 and past >= 2 * tk and ts <= tk
    rows = nh * ts
    early = past - tk
    span = max(early, 2 * tk)
    tri = _tri(tk).T
    q_spec = pl.BlockSpec((None, nh, ts, d), lambda bb, hh: (bb, hh, 0, 0))
    new_spec = pl.BlockSpec((None, None, nh, ts, d), lambda bb, hh: (layer, bb, hh, 0, 0))
    tail_spec = pl.BlockSpec((None, None, nh, tk, d),
                             lambda bb, hh: (layer, bb, hh, past // tk - 1, 0))
    any_spec = pl.BlockSpec(memory_space=pl.ANY)
    return pl.pallas_call(
        functools.partial(_attn_sample_kernel, past=past, tk=tk, layer=layer),
        grid=(b, h // nh),
        in_specs=[q_spec, new_spec, new_spec, tail_spec, tail_spec, any_spec, any_spec,
                  pl.BlockSpec((tk, tk), lambda bb, hh: (0, 0))],
        out_specs=pl.BlockSpec((ts, nh * d), lambda bb, hh: (bb, hh)),
        out_shape=jax.ShapeDtypeStruct((b * ts, h * d), BF16),
        scratch_shapes=[pltpu.VMEM((nh, span, d), BF16), pltpu.VMEM((nh, span, d), BF16),
                        pltpu.VMEM((nh, early, d), F32), pltpu.VMEM((nh, early, d), F32),
                        pltpu.VMEM((rows, span), F32),
                        pltpu.VMEM((span // tk * rows, tk), BF16),
                        pltpu.VMEM((span // tk * rows, tk), F32),
                        pltpu.VMEM((rows, span), BF16),
                        pltpu.VMEM((rows, d), F32),
                        pltpu.SemaphoreType.DMA((2,))],
        compiler_params=_params("arbitrary", "arbitrary"),
        name="attn_sample",
    )(q, k_stack, v_stack, cache_k, cache_v, cache_k, cache_v, tri)


def _mixers_kernel(glu_ref, up_ref, hc_ref, hp_ref, dww_ref, dwb_ref, ng_ref, nb_ref,
                   pww_ref, pwb_ref, plw_ref, pls_ref,
                   oc_ref, op_ref, cs_ref, ps_ref,
                   ext_ref, sh_ref, yn_ref, pext_ref, pooled_ref,
                   *, tm, carry, start_pos):
    i = pl.program_id(0)
    hdr = HIST_ROWS

    def load_hist():
        ext_ref[0:hdr, :] = hc_ref[...]
        pext_ref[0:hdr, :] = hp_ref[...]

    if carry:
        pl.when(i == 0)(load_hist)
    else:
        load_hist()

    ext_ref[hdr:hdr + tm, :] = glu_ref[...]
    pext_ref[hdr:hdr + tm, :] = up_ref[...]
    cs_ref[...] = ext_ref[pl.ds(hdr + tm - (CONV_KERNEL - 1), CONV_KERNEL - 1), :]
    ps_ref[...] = pext_ref[pl.ds(hdr + tm - POOL_HIST, POOL_HIST), :]

    sh_rows = sh_ref.shape[1]
    for r in range(1, 8):
        sh_ref[r - 1] = ext_ref[pl.ds(r, sh_rows), :]

    rc = min(CONV_CHUNK, tm)
    first = hdr - (CONV_KERNEL - 1)

    def conv_chunk(c, _):
        base = pl.multiple_of(c * rc, rc)
        acc = jnp.zeros((rc, CONV_WIDTH), F32)
        for r in range(8):
            taps = [(j, ((first + j) // 8) * 8) for j in range(CONV_KERNEL)
                    if (first + j) % 8 == r]
            lo = min(off for _, off in taps)
            hi = max(off for _, off in taps)
            src = ext_ref if r == 0 else sh_ref.at[r - 1]
            slab = src[pl.ds(base + lo, rc + hi - lo), :]
            for j, off in taps:
                acc = acc + slab[off - lo:off - lo + rc] * dww_ref[j:j + 1, :]
        y = acc + dwb_ref[...]
        gw = CONV_WIDTH // CONV_GROUPS
        for g in range(CONV_GROUPS):
            sl = slice(g * gw, (g + 1) * gw)
            yg = y[:, sl]
            mu = jnp.mean(yg, axis=-1, keepdims=True)
            dv = yg - mu
            var = jnp.mean(dv * dv, axis=-1, keepdims=True)
            yn = dv * lax.rsqrt(var + EPS) * ng_ref[:, sl] + nb_ref[:, sl]
            yn = yn * jax.nn.sigmoid(yn)
            yn_ref[pl.ds(base, rc), sl] = yn.astype(BF16)
        return 0

    lax.fori_loop(0, tm // rc, conv_chunk, 0)
    oc_ref[...] = (_dot(yn_ref[...], pww_ref[...]) + pwb_ref[...]).astype(oc_ref.dtype)

    pc = min(LANE, tm)
    for c in range(tm // pc):
        b0 = hdr + c * pc
        row = lax.broadcasted_iota(jnp.int32, (pc, POOL_GROUP), 0) + c * pc
        pos = row + (start_pos + (i * tm if carry else 0))
        for g, w in enumerate(POOL_WINDOWS):
            sl = slice(g * POOL_GROUP, (g + 1) * POOL_GROUP)
            u = pext_ref[b0:b0 + pc, sl]
            s = u
            for dlt in range(1, w):
                s = s + pext_ref[pl.ds(b0 - dlt, pc), sl]
            cnt = jnp.minimum(w, pos + 1).astype(F32)
            pooled_ref[c * pc:(c + 1) * pc, sl] = (s / cnt - u).astype(BF16)
    op_ref[...] = (_dot(pooled_ref[...], plw_ref[...]) * pls_ref[...]).astype(op_ref.dtype)

    if carry:
        ext_ref[0:hdr, :] = ext_ref[tm:tm + hdr, :]
        pext_ref[0:hdr, :] = pext_ref[tm:tm + hdr, :]


def mixers_call(glu, up, hist_c, hist_p, lw, carry, start_pos):
    m = glu.shape[0]
    nseq = hist_c.shape[0]
    tm = min(ROW_TILE, m) if carry else m // nseq
    row = lambda i: (i, 0)
    const = lambda i: (0, 0)
    seq = (lambda i: (0, 0, 0)) if carry else (lambda i: (i, 0, 0))
    vec = pl.BlockSpec((1, CONV_WIDTH), const)
    sq = pl.BlockSpec((CONV_WIDTH, CONV_WIDTH), const)
    return pl.pallas_call(
        functools.partial(_mixers_kernel, tm=tm, carry=carry, start_pos=start_pos),
        grid=(m // tm,),
        in_specs=[pl.BlockSpec((tm, CONV_WIDTH), row),
                  pl.BlockSpec((tm, POOL_WIDTH), row),
                  pl.BlockSpec((None, HIST_ROWS, CONV_WIDTH), seq),
                  pl.BlockSpec((None, HIST_ROWS, POOL_WIDTH), seq),
                  pl.BlockSpec((HIST_ROWS, CONV_WIDTH), const),
                  vec, vec, vec, sq, vec, sq, vec],
        out_specs=[pl.BlockSpec((tm, CONV_WIDTH), row),
                   pl.BlockSpec((tm, POOL_WIDTH), row),
                   pl.BlockSpec((None, CONV_KERNEL - 1, CONV_WIDTH), seq),
                   pl.BlockSpec((None, POOL_HIST, POOL_WIDTH), seq)],
        out_shape=[jax.ShapeDtypeStruct((m, CONV_WIDTH), BF16),
                   jax.ShapeDtypeStruct((m, POOL_WIDTH), BF16),
                   jax.ShapeDtypeStruct((nseq, CONV_KERNEL - 1, CONV_WIDTH), F32),
                   jax.ShapeDtypeStruct((nseq, POOL_HIST, POOL_WIDTH), F32)],
        scratch_shapes=[pltpu.VMEM((HIST_ROWS + tm + 8, CONV_WIDTH), F32),
                        pltpu.VMEM((7, tm + HIST_ROWS - 8, CONV_WIDTH), F32),
                        pltpu.VMEM((tm, CONV_WIDTH), BF16),
                        pltpu.VMEM((HIST_ROWS + tm, POOL_WIDTH), F32),
                        pltpu.VMEM((tm, POOL_WIDTH), BF16)],
        compiler_params=_params("arbitrary"),
        name="mixers_prompt" if carry else "mixers_sample",
    )(glu, up, hist_c, hist_p, lw["dw_w"], lw["dw_b"], lw["n_g"], lw["n_b"],
      lw["pw_w"], lw["pw_b"], lw["pool_w"], lw["pool_s"])


def _outproj_kernel(h_ref, osb_ref, oc_ref, op_ref, w_ref, g_ref, hn_ref, xn_ref):
    c1 = SB_WIDTH
    c2 = SB_WIDTH + CONV_WIDTH
    mix = (_dot(osb_ref[...], w_ref[0:c1, :]) + _dot(oc_ref[...], w_ref[c1:c2, :])
           + _dot(op_ref[...], w_ref[c2:, :]))
    h = h_ref[...] + mix
    hn_ref[...] = h
    xn_ref[...] = _rmsnorm_rows(h, g_ref[...]).astype(xn_ref.dtype)


def outproj_call(h, osb, oc, op, w_out, g, layer):
    m, d = h.shape
    tm = min(ROW_TILE, m)
    row = lambda i: (i, 0)
    return pl.pallas_call(
        _outproj_kernel,
        grid=(m // tm,),
        in_specs=[pl.BlockSpec((tm, d), row),
                  pl.BlockSpec((tm, SB_WIDTH), row),
                  pl.BlockSpec((tm, CONV_WIDTH), row),
                  pl.BlockSpec((tm, POOL_WIDTH), row),
                  pl.BlockSpec((None, d, d), lambda i: (layer, 0, 0),
                               pipeline_mode=pl.Buffered(1)),
                  pl.BlockSpec((1, d), lambda i: (0, 0))],
        out_specs=[pl.BlockSpec((tm, d), row), pl.BlockSpec((tm, d), row)],
        out_shape=[jax.ShapeDtypeStruct((m, d), F32), jax.ShapeDtypeStruct((m, d), BF16)],
        compiler_params=_params("arbitrary"),
        name="outproj",
    )(h, osb, oc, op, w_out, g.reshape(1, d))


def _mlp_kernel(xn_ref, h_ref, wu_ref, wd_ref, *rest, final):
    if final:
        g_ref, o_ref, acc_ref = rest
    else:
        o_ref, acc_ref = rest
    f = pl.program_id(1)

    @pl.when(f == 0)
    def _():
        acc_ref[...] = jnp.zeros_like(acc_ref)

    a = jnp.maximum(_dot(xn_ref[...], wu_ref[...]), 0.0)
    acc_ref[...] += _dot((a * a).astype(BF16), wd_ref[...])

    @pl.when(f == pl.num_programs(1) - 1)
    def _():
        h = h_ref[...] + acc_ref[...]
        o_ref[...] = _rmsnorm_rows(h, g_ref[...]) if final else h


def mlp_call(xn, h, w_up, w_down, layer, final_g=None):
    m, d = h.shape
    ff = w_up.shape[2]
    tm = min(ROW_TILE, m)
    tf = FF_TILE
    row = lambda i, f: (i, 0)
    final = final_g is not None
    in_specs = [pl.BlockSpec((tm, d), row),
                pl.BlockSpec((tm, d), row),
                pl.BlockSpec((None, d, tf), lambda i, f: (layer, 0, f)),
                pl.BlockSpec((None, tf, d), lambda i, f: (layer, f, 0))]
    args = [xn, h, w_up, w_down]
    if final:
        in_specs.append(pl.BlockSpec((1, d), lambda i, f: (0, 0)))
        args.append(final_g.reshape(1, d))
    return pl.pallas_call(
        functools.partial(_mlp_kernel, final=final),
        grid=(m // tm, ff // tf),
        in_specs=in_specs,
        out_specs=pl.BlockSpec((tm, d), row),
        out_shape=jax.ShapeDtypeStruct((m, d), F32),
        scratch_shapes=[pltpu.VMEM((tm, d), F32)],
        compiler_params=_params("arbitrary", "arbitrary"),
        name="mlp_final" if final else "mlp",
    )(*args)


def _pad_hist(hist, rows):
    return jnp.pad(hist, ((0, 0), (rows - hist.shape[1], 0), (0, 0)))


def _block_diag(w):
    g, a, b = w.shape
    eye = jnp.eye(g, dtype=w.dtype)
    return (eye[:, None, :, None] * w[:, :, None, :]).reshape(g * a, g * b)


def _group_trunk(x, layers, big, final_g, sample, caches):
    depth = len(layers)
    h = x
    cs, ps = [], []
    shape = kv_stack_shape(x.shape[0], sample, depth)
    kv = (jnp.zeros(shape, F32), jnp.zeros(shape, F32))
    for l, lw in enumerate(layers):
        if sample:
            cache_k, cache_v, cache_conv, state_pool = caches
            (q, glu, up), kv = inproj_call(h, lw["norm_mix_g"], big["w_in"], True, l, kv)
            osb = attn_sample_call(q, kv[0], kv[1], cache_k, cache_v, l)
            hist_c = _pad_hist(cache_conv[l], HIST_ROWS)
            hist_p = _pad_hist(state_pool[l], HIST_ROWS)
            start = cache_k.shape[3]
        else:
            (qt, kb, vt, glu, up), kv = inproj_call(h, lw["norm_mix_g"], big["w_in"], False,
                                                    l, kv)
            osb = attn_prompt_call(qt, kb, vt)
            hist_c = jnp.zeros((1, HIST_ROWS, CONV_WIDTH), F32)
            hist_p = jnp.zeros((1, HIST_ROWS, POOL_WIDTH), F32)
            start = 0
        oc, op, c_state, p_state = mixers_call(glu, up, hist_c, hist_p, lw, not sample, start)
        h, xn2 = outproj_call(h, osb, oc, op, big["w_out"], lw["norm_mlp_g"], l)
        h = mlp_call(xn2, h, big["w_up"], big["w_down"], l, final_g if l + 1 == depth else None)
        cs.append(c_state)
        ps.append(p_state)
    return h, kv[0], kv[1], jnp.stack(cs), jnp.stack(ps)


def kernel(x_prompt, x_sample, cache_k, cache_v, cache_conv, state_pool, norm_mix_g, w_in,
           conv_dw_w, conv_dw_b, conv_norm_g, conv_norm_b, conv_pw_w, conv_pw_b, pool_w,
           pool_scale, w_out, norm_mlp_g, w_up, w_down, final_norm_g):
    depth = w_in.shape[0]
    bp, seq, d = x_prompt.shape
    bs, dseq, _ = x_sample.shape
    assert bp == 1 and dseq == DEC_SEQ and d == D_MODEL
    layers = []
    for l in range(depth):
        layers.append(dict(
            norm_mix_g=norm_mix_g[l],
            dw_w=jnp.pad(conv_dw_w[l], ((0, HIST_ROWS - CONV_KERNEL), (0, 0))),
            dw_b=conv_dw_b[l].reshape(1, -1),
            n_g=conv_norm_g[l].reshape(1, -1),
            n_b=conv_norm_b[l].reshape(1, -1),
            pw_w=conv_pw_w[l].astype(BF16),
            pw_b=conv_pw_b[l].reshape(1, -1),
            pool_w=_block_diag(pool_w[l]).astype(BF16),
            pool_s=pool_scale[l].reshape(1, -1),
            norm_mlp_g=norm_mlp_g[l],
        ))
    big = dict(w_in=w_in.astype(BF16), w_out=w_out.astype(BF16),
               w_up=w_up.astype(BF16), w_down=w_down.astype(BF16))
    yp, kp, vp, cp, pp = _group_trunk(x_prompt.reshape(seq, d), layers, big, final_norm_g,
                                      False, None)
    ys, ks, vs, cs, ps = _group_trunk(x_sample.reshape(bs * dseq, d), layers, big, final_norm_g, True,
                                      (cache_k, cache_v, cache_conv, state_pool))
    return (yp.reshape(bp, seq, d), ys.reshape(bs, dseq, d),
            kp[:, None], vp[:, None], cp, pp, ks, vs, cs, ps)
```

```python
import functools

import jax
import jax.numpy as jnp
from jax import lax
from jax.experimental import pallas as pl
from jax.experimental.pallas import tpu as pltpu

F32 = jnp.float32
BF16 = jnp.bfloat16

D_MODEL = 2048
SB_WIDTH = 1024
HEAD_DIM = 128
HEADS = SB_WIDTH // HEAD_DIM
CONV_WIDTH = 512
CONV_KERNEL = 31
CONV_GROUPS = 4
POOL_WIDTH = 512
POOL_WINDOWS = (2, 4, 8, 16)
POOL_GROUP = POOL_WIDTH // len(POOL_WINDOWS)
POOL_HIST = max(POOL_WINDOWS) - 1
IN_COLS = 3 * SB_WIDTH + 2 * CONV_WIDTH + POOL_WIDTH
EPS = 1e-6
DEC_SEQ = 64

LOG2E = 1.4426950408889634
Q_SCALE = HEAD_DIM ** -0.5 * LOG2E

LANE = 128
HIST_ROWS = 32
ROW_TILE = 512
FF_TILE = 1024
ATTN_TK = 128
ATTN_VT = 2 * ATTN_TK
SAMPLE_TK = 256
CONV_CHUNK = 64
VMEM_LIMIT = 56 * 1024 * 1024


def _params(*sem):
    return pltpu.CompilerParams(dimension_semantics=sem, vmem_limit_bytes=VMEM_LIMIT)


def _dot(a, b):
    return jnp.dot(a, b, preferred_element_type=F32)


def _softplus2(z):
    e = jnp.exp2(-jnp.abs(z))
    return jnp.maximum(z, 0.0) + jnp.log(1.0 + e) * LOG2E


def _rmsnorm_rows(x, g):
    ms = jnp.mean(x * x, axis=-1, keepdims=True)
    return x * lax.rsqrt(ms + EPS) * g


def _inproj_kernel(h_ref, g_ref, w_ref, *rest, tm, sample, n_alias):
    outs = rest[n_alias:]
    x = _rmsnorm_rows(h_ref[...], g_ref[...]).astype(BF16)
    if sample:
        q_ref, kf_ref, vf_ref, glu_ref, up_ref = outs
        nb = tm // DEC_SEQ
    else:
        qt_ref, kb_ref, vt_ref, kf_ref, vf_ref, glu_ref, up_ref = outs
    pair = 2 * HEAD_DIM
    for hp in range(HEADS // 2):
        c0 = hp * pair
        q2 = _dot(x, w_ref[:, c0:c0 + pair]) * Q_SCALE
        k2 = _dot(x, w_ref[:, SB_WIDTH + c0:SB_WIDTH + c0 + pair])
        v2 = _dot(x, w_ref[:, 2 * SB_WIDTH + c0:2 * SB_WIDTH + c0 + pair])
        for u in range(2):
            h = 2 * hp + u
            sl = slice(u * HEAD_DIM, (u + 1) * HEAD_DIM)
            if sample:
                q_ref[:, h] = q2[:, sl].astype(BF16).reshape(nb, DEC_SEQ, HEAD_DIM)
                kf_ref[:, h] = k2[:, sl].reshape(nb, DEC_SEQ, HEAD_DIM)
                vf_ref[:, h] = v2[:, sl].reshape(nb, DEC_SEQ, HEAD_DIM)
            else:
                qt_ref[h] = q2[:, sl].T.astype(BF16)
                kb_ref[h] = k2[:, sl].astype(BF16)
                kf_ref[h] = k2[:, sl]
                vf_ref[h] = v2[:, sl]
                vt = v2[:, sl].T.astype(BF16)
                for c in range(tm // ATTN_VT):
                    vt_ref[h, c] = vt[:, c * ATTN_VT:(c + 1) * ATTN_VT]
    c0 = 3 * SB_WIDTH
    uc = _dot(x, w_ref[:, c0:c0 + 2 * CONV_WIDTH])
    glu_ref[...] = uc[:, 0:CONV_WIDTH] * jax.nn.sigmoid(uc[:, CONV_WIDTH:])
    up_ref[...] = _dot(x, w_ref[:, c0 + 2 * CONV_WIDTH:])


def kv_stack_shape(m, sample, depth):
    if sample:
        return (depth, m // DEC_SEQ, HEADS, DEC_SEQ, HEAD_DIM)
    return (depth, HEADS, m, HEAD_DIM)


def inproj_call(h, g, w_in, sample, layer, kv_stack):
    depth = kv_stack[0].shape[0]
    m, d = h.shape
    tm = min(ROW_TILE, m)
    if sample:
        nbt = m // DEC_SEQ
        nb = tm // DEC_SEQ
        hshape = (nbt, HEADS, DEC_SEQ, HEAD_DIM)
        hblock = (nb, HEADS, DEC_SEQ, HEAD_DIM)
        kv_shape = jax.ShapeDtypeStruct((depth,) + hshape, F32)
        kv_spec = pl.BlockSpec((None,) + hblock, lambda i: (layer, i, 0, 0, 0))
        out_shape = [jax.ShapeDtypeStruct(hshape, BF16), kv_shape, kv_shape]
        out_specs = [pl.BlockSpec(hblock, lambda i: (i, 0, 0, 0)), kv_spec, kv_spec]
        kv_index = (1, 2)
    else:
        nkb = m // ATTN_VT
        kshape = (HEADS, m, HEAD_DIM)
        kblock = (HEADS, tm, HEAD_DIM)
        kv_shape = jax.ShapeDtypeStruct((depth,) + kshape, F32)
        kv_spec = pl.BlockSpec((None,) + kblock, lambda i: (layer, 0, i, 0))
        out_shape = [jax.ShapeDtypeStruct((HEADS, HEAD_DIM, m), BF16),
                     jax.ShapeDtypeStruct(kshape, BF16),
                     jax.ShapeDtypeStruct((HEADS, nkb, HEAD_DIM, ATTN_VT), BF16),
                     kv_shape, kv_shape]
        out_specs = [pl.BlockSpec((HEADS, HEAD_DIM, tm), lambda i: (0, 0, i)),
                     pl.BlockSpec(kblock, lambda i: (0, i, 0)),
                     pl.BlockSpec((HEADS, tm // ATTN_VT, HEAD_DIM, ATTN_VT),
                                  lambda i: (0, i, 0, 0)),
                     kv_spec, kv_spec]
        kv_index = (3, 4)
    out_shape += [jax.ShapeDtypeStruct((m, CONV_WIDTH), F32),
                  jax.ShapeDtypeStruct((m, POOL_WIDTH), F32)]
    out_specs += [pl.BlockSpec((tm, CONV_WIDTH), lambda i: (i, 0)),
                  pl.BlockSpec((tm, POOL_WIDTH), lambda i: (i, 0))]
    in_specs = [pl.BlockSpec((tm, d), lambda i: (i, 0)),
                pl.BlockSpec((1, d), lambda i: (0, 0)),
                pl.BlockSpec((None, d, IN_COLS), lambda i: (layer, 0, 0),
                             pipeline_mode=pl.Buffered(1))]
    args = [h, g.reshape(1, d), w_in]
    aliases = {}
    for n, stack in enumerate(kv_stack):
        assert stack.shape == kv_shape.shape and stack.dtype == kv_shape.dtype
        aliases[len(args)] = kv_index[n]
        in_specs.append(pl.BlockSpec(memory_space=pl.ANY))
        args.append(stack)
    outs = pl.pallas_call(
        functools.partial(_inproj_kernel, tm=tm, sample=sample, n_alias=len(aliases)),
        grid=(m // tm,),
        in_specs=in_specs,
        out_specs=out_specs,
        out_shape=out_shape,
        input_output_aliases=aliases,
        compiler_params=_params("arbitrary"),
        name="inproj_sample" if sample else "inproj_prompt",
    )(*args)
    kv = tuple(outs[n] for n in kv_index)
    return [o for n, o in enumerate(outs) if n not in kv_index], kv


MASKED_LOGIT = -1e30
STICK_EXHAUSTED = 160.0


def _attn_stage1(k_ref, qt, p, i, masked, a_ref, sp_ref, *, tq, tk):
    for half in (1, 0):
        row0 = pl.multiple_of(p * (2 * tk) + half * tk, tk)
        rows = slice(half * tk, (half + 1) * tk)
        lane0 = tq - tk if masked and half == 1 else 0
        if lane0:
            a_ref[rows, 0:lane0] = jnp.full((tk, lane0), MASKED_LOGIT, F32)
            sp_ref[half, :, 0:lane0] = jnp.zeros((tk, lane0), BF16)
        z = _dot(k_ref[pl.ds(row0, tk), :], qt[:, lane0:])
        sp = _softplus2(z)
        a = z - sp
        if masked:
            s_idx = row0 + lax.broadcasted_iota(jnp.int32, z.shape, 0)
            t_idx = i * tq + lane0 + lax.broadcasted_iota(jnp.int32, z.shape, 1)
            valid = s_idx < t_idx
            sp = jnp.where(valid, sp, 0.0)
            a = jnp.where(valid, a, MASKED_LOGIT)
        a_ref[rows, lane0:] = a
        sp_ref[half, :, lane0:] = sp.astype(BF16)


def _attn_stage2(tri_ref, a_ref, sp_ref, half, carry, *, tk):
    sp = sp_ref[half]
    later = _dot(tri_ref[...], sp)
    w = jnp.exp2(a_ref[half * tk:(half + 1) * tk, :] - later - carry)
    return w.astype(BF16), carry + later[0:1, :] + sp[0:1, :].astype(F32)


ATTN_HEADS_PER_STEP = 4
ATTN_STAGE_BUFS = 2
PV_LAG = 4


def _attn_prompt_kernel(qt_ref, k_ref, vt_ref, tri_ref, o_ref, *scratch, tq, tk):
    i = pl.program_id(1)
    nh = ATTN_HEADS_PER_STEP
    heads = range(nh)
    acc = scratch[0:nh]
    ab = [scratch[nh + b * nh:nh + (b + 1) * nh] for b in range(ATTN_STAGE_BUFS)]
    sp0 = nh * (1 + ATTN_STAGE_BUFS)
    spb = [scratch[sp0 + b * nh:sp0 + (b + 1) * nh] for b in range(ATTN_STAGE_BUFS)]
    s1 = functools.partial(_attn_stage1, tq=tq, tk=tk)
    s2 = functools.partial(_attn_stage2, tk=tk)
    qts = [qt_ref[h] for h in heads]
    for h in heads:
        acc[h][...] = jnp.zeros_like(acc[h])

    def second(buf, p, carry):
        carry = list(carry)
        units = [(h, half) for h in heads for half in (1, 0)]
        ws = {}
        for n in range(len(units) + PV_LAG):
            if n < len(units):
                h, half = units[n]
                ws[units[n]], carry[h] = s2(tri_ref, ab[buf][h], spb[buf][h], half, carry[h])
            if n >= PV_LAG:
                h, half = units[n - PV_LAG]
                vt = vt_ref[h, p, :, half * tk:(half + 1) * tk]
                acc[h][...] += _dot(vt, ws.pop(units[n - PV_LAG]))
        return tuple(carry)

    def step(buf, p, carry):
        for h in heads:
            s1(k_ref.at[h], qts[h], jnp.maximum(p - 1, 0), i, False,
               ab[1 - buf][h], spb[1 - buf][h])
        return second(buf, p, carry)

    for h in heads:
        s1(k_ref.at[h], qts[h], i, i, True, ab[0][h], spb[0][h])

    def live(state):
        m, _, least = state
        return jnp.logical_and(m < (i + 1) // 2, least <= STICK_EXHAUSTED)

    def body(state):
        m, carry, _ = state
        p = i - 2 * m
        carry = step(1, p - 1, step(0, p, carry))
        least = functools.reduce(jnp.minimum, [jnp.min(c) for c in carry])
        return m + 1, carry, least

    zero = tuple(jnp.zeros((1, tq), F32) for _ in heads)
    _, carry, least = lax.while_loop(live, body, (jnp.int32(0), zero, jnp.float32(0.0)))

    @pl.when(jnp.logical_and(i % 2 == 0, least <= STICK_EXHAUSTED))
    def _():
        second(0, 0, carry)

    d = acc[0].shape[0]
    for h in heads:
        o_ref[:, h * d:(h + 1) * d] = acc[h][...].T.astype(o_ref.dtype)


def _tri(tk):
    s = lax.broadcasted_iota(jnp.int32, (tk, tk), 0)
    j = lax.broadcasted_iota(jnp.int32, (tk, tk), 1)
    return (j > s).astype(BF16)


def attn_prompt_call(qt, kb, vt):
    h, d, t = qt.shape
    tq = ATTN_VT
    tk = ATTN_TK
    tri = _tri(tk)
    nh = ATTN_HEADS_PER_STEP
    nstage = nh * ATTN_STAGE_BUFS
    return pl.pallas_call(
        functools.partial(_attn_prompt_kernel, tq=tq, tk=tk),
        grid=(h // nh, t // tq),
        in_specs=[pl.BlockSpec((nh, d, tq), lambda hh, i: (hh, 0, i)),
                  pl.BlockSpec((nh, t, d), lambda hh, i: (hh, 0, 0)),
                  pl.BlockSpec((nh, t // ATTN_VT, d, ATTN_VT), lambda hh, i: (hh, 0, 0, 0)),
                  pl.BlockSpec((tk, tk), lambda hh, i: (0, 0))],
        out_specs=pl.BlockSpec((tq, nh * d), lambda hh, i: (i, hh)),
        out_shape=jax.ShapeDtypeStruct((t, h * d), BF16),
        scratch_shapes=([pltpu.VMEM((d, tq), F32)] * nh
                        + [pltpu.VMEM((2 * tk, tq), F32)] * nstage
                        + [pltpu.VMEM((2, tk, tq), BF16)] * nstage),
        compiler_params=_params("arbitrary", "arbitrary"),
        name="attn_prompt",
    )(qt, kb, vt, tri)


SAMPLE_HEADS_PER_STEP = 8


def _sample_sweep(q_ref, kb_ref, vb_ref, nblk, mask_last, carry, tri_ref, acc_ref,
                  z_ref, sp_ref, cum_ref, w_ref, *, tk):
    nt = (((1,), (1,)), ((), ()))
    nh, ts, d = q_ref.shape
    rows = nh * ts
    tot = nblk * tk
    for h in range(nh):
        z_ref[h * ts:(h + 1) * ts, 0:tot] = lax.dot_general(
            q_ref[h], kb_ref[h, 0:tot, :], nt, preferred_element_type=F32)
    for b in range(nblk):
        cols = slice(b * tk, (b + 1) * tk)
        z = z_ref[:, cols]
        sp = _softplus2(z)
        a = z - sp
        if mask_last and b == nblk - 1:
            t_idx = lax.broadcasted_iota(jnp.int32, (rows, tk), 0) % ts
            s_idx = lax.broadcasted_iota(jnp.int32, (rows, tk), 1)
            valid = s_idx < t_idx
            sp = jnp.where(valid, sp, 0.0)
            a = jnp.where(valid, a, MASKED_LOGIT)
        z_ref[:, cols] = a
        sp_ref[b * rows:(b + 1) * rows, :] = sp.astype(BF16)
    cum_ref[0:nblk * rows, :] = _dot(sp_ref[0:nblk * rows, :], tri_ref[...])
    for b in range(nblk - 1, -1, -1):
        cols = slice(b * tk, (b + 1) * tk)
        later = cum_ref[b * rows:(b + 1) * rows, :]
        w_ref[:, cols] = jnp.exp2(z_ref[:, cols] - later - carry).astype(BF16)
        first = sp_ref[b * rows:(b + 1) * rows, 0:1].astype(F32)
        carry = carry + later[:, 0:1] + first
    for h in range(nh):
        acc_ref[h * ts:(h + 1) * ts, :] += _dot(w_ref[h * ts:(h + 1) * ts, 0:tot],
                                                 vb_ref[h, 0:tot, :])
    return carry


def _attn_sample_kernel(q_ref, kn_ref, vn_ref, kt_ref, vt_ref, ck_hbm, cv_hbm, tri_ref, o_ref,
                        kb_ref, vb_ref, ckf_ref, cvf_ref, z_ref, sp_ref, cum_ref, w_ref,
                        acc_ref, sem, *, past, tk, layer):
    nh, ts, d = q_ref.shape
    rows = nh * ts
    early = past - tk
    sweep = functools.partial(_sample_sweep, q_ref, kb_ref, vb_ref, tri_ref=tri_ref,
                              acc_ref=acc_ref, z_ref=z_ref, sp_ref=sp_ref, cum_ref=cum_ref,
                              w_ref=w_ref, tk=tk)
    acc_ref[...] = jnp.zeros_like(acc_ref)
    for tail, new, dst in ((kt_ref, kn_ref, kb_ref), (vt_ref, vn_ref, vb_ref)):
        for h in range(nh):
            dst[h, 0:tk, :] = tail[h].astype(BF16)
            dst[h, tk:tk + ts, :] = new[h].astype(BF16)
            dst[h, tk + ts:2 * tk, :] = jnp.zeros((tk - ts, d), BF16)
    carry = sweep(2, True, jnp.zeros((rows, 1), F32))

    @pl.when(jnp.min(carry) <= STICK_EXHAUSTED)
    def _():
        b = pl.program_id(0)
        h0 = pl.multiple_of(pl.program_id(1) * nh, nh)
        copies = [pltpu.make_async_copy(
            src.at[layer, b, pl.ds(h0, nh), pl.ds(0, early), :], dst, sem.at[n])
            for n, (src, dst) in enumerate(((ck_hbm, ckf_ref), (cv_hbm, cvf_ref)))]
        for c in copies:
            c.start()
        for c in copies:
            c.wait()
        for h in range(nh):
            kb_ref[h, 0:early, :] = ckf_ref[h].astype(BF16)
            vb_ref[h, 0:early, :] = cvf_ref[h].astype(BF16)
        sweep(early // tk, False, carry)

    for h in range(nh):
        o_ref[:, h * d:(h + 1) * d] = acc_ref[h * ts:(h + 1) * ts, :].astype(o_ref.dtype)


def attn_sample_call(q, k_stack, v_stack, cache_k, cache_v, layer):
    b, h, ts, d = q.shape
    past = cache_k.shape[3]
    tk = SAMPLE_TK
    nh = SAMPLE_HEADS_PER_STEP
    assert past % tk == 0 and past >= 2 * tk and ts <= tk
    rows = nh * ts
    early = past - tk
    span = max(early, 2 * tk)
    tri = _tri(tk).T
    q_spec = pl.BlockSpec((None, nh, ts, d), lambda bb, hh: (bb, hh, 0, 0))
    new_spec = pl.BlockSpec((None, None, nh, ts, d), lambda bb, hh: (layer, bb, hh, 0, 0))
    tail_spec = pl.BlockSpec((None, None, nh, tk, d),
                             lambda bb, hh: (layer, bb, hh, past // tk - 1, 0))
    any_spec = pl.BlockSpec(memory_space=pl.ANY)
    return pl.pallas_call(
        functools.partial(_attn_sample_kernel, past=past, tk=tk, layer=layer),
        grid=(b, h // nh),
        in_specs=[q_spec, new_spec, new_spec, tail_spec, tail_spec, any_spec, any_spec,
                  pl.BlockSpec((tk, tk), lambda bb, hh: (0, 0))],
        out_specs=pl.BlockSpec((ts, nh * d), lambda bb, hh: (bb, hh)),
        out_shape=jax.ShapeDtypeStruct((b * ts, h * d), BF16),
        scratch_shapes=[pltpu.VMEM((nh, span, d), BF16), pltpu.VMEM((nh, span, d), BF16),
                        pltpu.VMEM((nh, early, d), F32), pltpu.VMEM((nh, early, d), F32),
                        pltpu.VMEM((rows, span), F32),
                        pltpu.VMEM((span // tk * rows, tk), BF16),
                        pltpu.VMEM((span // tk * rows, tk), F32),
                        pltpu.VMEM((rows, span), BF16),
                        pltpu.VMEM((rows, d), F32),
                        pltpu.SemaphoreType.DMA((2,))],
        compiler_params=_params("arbitrary", "arbitrary"),
        name="attn_sample",
    )(q, k_stack, v_stack, cache_k, cache_v, cache_k, cache_v, tri)


def _mixers_kernel(glu_ref, up_ref, hc_ref, hp_ref, dww_ref, dwb_ref, ng_ref, nb_ref,
                   pww_ref, pwb_ref, plw_ref, pls_ref,
                   oc_ref, op_ref, cs_ref, ps_ref,
                   ext_ref, sh_ref, yn_ref, pext_ref, pooled_ref,
                   *, tm, carry, start_pos):
    i = pl.program_id(0)
    hdr = HIST_ROWS

    def load_hist():
        ext_ref[0:hdr, :] = hc_ref[...]
        pext_ref[0:hdr, :] = hp_ref[...]

    if carry:
        pl.when(i == 0)(load_hist)
    else:
        load_hist()

    ext_ref[hdr:hdr + tm, :] = glu_ref[...]
    pext_ref[hdr:hdr + tm, :] = up_ref[...]
    cs_ref[...] = ext_ref[pl.ds(hdr + tm - (CONV_KERNEL - 1), CONV_KERNEL - 1), :]
    ps_ref[...] = pext_ref[pl.ds(hdr + tm - POOL_HIST, POOL_HIST), :]

    sh_rows = sh_ref.shape[1]
    for r in range(1, 8):
        sh_ref[r - 1] = ext_ref[pl.ds(r, sh_rows), :]

    rc = min(CONV_CHUNK, tm)
    first = hdr - (CONV_KERNEL - 1)

    def conv_chunk(c, _):
        base = pl.multiple_of(c * rc, rc)
        acc = jnp.zeros((rc, CONV_WIDTH), F32)
        for r in range(8):
            taps = [(j, ((first + j) // 8) * 8) for j in range(CONV_KERNEL)
                    if (first + j) % 8 == r]
            lo = min(off for _, off in taps)
            hi = max(off for _, off in taps)
            src = ext_ref if r == 0 else sh_ref.at[r - 1]
            slab = src[pl.ds(base + lo, rc + hi - lo), :]
            for j, off in taps:
                acc = acc + slab[off - lo:off - lo + rc] * dww_ref[j:j + 1, :]
        y = acc + dwb_ref[...]
        gw = CONV_WIDTH // CONV_GROUPS
        for g in range(CONV_GROUPS):
            sl = slice(g * gw, (g + 1) * gw)
            yg = y[:, sl]
            mu = jnp.mean(yg, axis=-1, keepdims=True)
            dv = yg - mu
            var = jnp.mean(dv * dv, axis=-1, keepdims=True)
            yn = dv * lax.rsqrt(var + EPS) * ng_ref[:, sl] + nb_ref[:, sl]
            yn = yn * jax.nn.sigmoid(yn)
            yn_ref[pl.ds(base, rc), sl] = yn.astype(BF16)
        return 0

    lax.fori_loop(0, tm // rc, conv_chunk, 0)
    oc_ref[...] = (_dot(yn_ref[...], pww_ref[...]) + pwb_ref[...]).astype(oc_ref.dtype)

    pc = min(LANE, tm)
    for c in range(tm // pc):
        b0 = hdr + c * pc
        row = lax.broadcasted_iota(jnp.int32, (pc, POOL_GROUP), 0) + c * pc
        pos = row + (start_pos + (i * tm if carry else 0))
        for g, w in enumerate(POOL_WINDOWS):
            sl = slice(g * POOL_GROUP, (g + 1) * POOL_GROUP)
            u = pext_ref[b0:b0 + pc, sl]
            s = u
            for dlt in range(1, w):
                s = s + pext_ref[pl.ds(b0 - dlt, pc), sl]
            cnt = jnp.minimum(w, pos + 1).astype(F32)
            pooled_ref[c * pc:(c + 1) * pc, sl] = (s / cnt - u).astype(BF16)
    op_ref[...] = (_dot(pooled_ref[...], plw_ref[...]) * pls_ref[...]).astype(op_ref.dtype)

    if carry:
        ext_ref[0:hdr, :] = ext_ref[tm:tm + hdr, :]
        pext_ref[0:hdr, :] = pext_ref[tm:tm + hdr, :]


def mixers_call(glu, up, hist_c, hist_p, lw, carry, start_pos):
    m = glu.shape[0]
    nseq = hist_c.shape[0]
    tm = min(ROW_TILE, m) if carry else m // nseq
    row = lambda i: (i, 0)
    const = lambda i: (0, 0)
    seq = (lambda i: (0, 0, 0)) if carry else (lambda i: (i, 0, 0))
    vec = pl.BlockSpec((1, CONV_WIDTH), const)
    sq = pl.BlockSpec((CONV_WIDTH, CONV_WIDTH), const)
    return pl.pallas_call(
        functools.partial(_mixers_kernel, tm=tm, carry=carry, start_pos=start_pos),
        grid=(m // tm,),
        in_specs=[pl.BlockSpec((tm, CONV_WIDTH), row),
                  pl.BlockSpec((tm, POOL_WIDTH), row),
                  pl.BlockSpec((None, HIST_ROWS, CONV_WIDTH), seq),
                  pl.BlockSpec((None, HIST_ROWS, POOL_WIDTH), seq),
                  pl.BlockSpec((HIST_ROWS, CONV_WIDTH), const),
                  vec, vec, vec, sq, vec, sq, vec],
        out_specs=[pl.BlockSpec((tm, CONV_WIDTH), row),
                   pl.BlockSpec((tm, POOL_WIDTH), row),
                   pl.BlockSpec((None, CONV_KERNEL - 1, CONV_WIDTH), seq),
                   pl.BlockSpec((None, POOL_HIST, POOL_WIDTH), seq)],
        out_shape=[jax.ShapeDtypeStruct((m, CONV_WIDTH), BF16),
                   jax.ShapeDtypeStruct((m, POOL_WIDTH), BF16),
                   jax.ShapeDtypeStruct((nseq, CONV_KERNEL - 1, CONV_WIDTH), F32),
                   jax.ShapeDtypeStruct((nseq, POOL_HIST, POOL_WIDTH), F32)],
        scratch_shapes=[pltpu.VMEM((HIST_ROWS + tm + 8, CONV_WIDTH), F32),
                        pltpu.VMEM((7, tm + HIST_ROWS - 8, CONV_WIDTH), F32),
                        pltpu.VMEM((tm, CONV_WIDTH), BF16),
                        pltpu.VMEM((HIST_ROWS + tm, POOL_WIDTH), F32),
                        pltpu.VMEM((tm, POOL_WIDTH), BF16)],
        compiler_params=_params("arbitrary"),
        name="mixers_prompt" if carry else "mixers_sample",
    )(glu, up, hist_c, hist_p, lw["dw_w"], lw["dw_b"], lw["n_g"], lw["n_b"],
      lw["pw_w"], lw["pw_b"], lw["pool_w"], lw["pool_s"])


def _outproj_kernel(h_ref, osb_ref, oc_ref, op_ref, w_ref, g_ref, hn_ref, xn_ref):
    c1 = SB_WIDTH
    c2 = SB_WIDTH + CONV_WIDTH
    mix = (_dot(osb_ref[...], w_ref[0:c1, :]) + _dot(oc_ref[...], w_ref[c1:c2, :])
           + _dot(op_ref[...], w_ref[c2:, :]))
    h = h_ref[...] + mix
    hn_ref[...] = h
    xn_ref[...] = _rmsnorm_rows(h, g_ref[...]).astype(xn_ref.dtype)


def outproj_call(h, osb, oc, op, w_out, g, layer):
    m, d = h.shape
    tm = min(ROW_TILE, m)
    row = lambda i: (i, 0)
    return pl.pallas_call(
        _outproj_kernel,
        grid=(m // tm,),
        in_specs=[pl.BlockSpec((tm, d), row),
                  pl.BlockSpec((tm, SB_WIDTH), row),
                  pl.BlockSpec((tm, CONV_WIDTH), row),
                  pl.BlockSpec((tm, POOL_WIDTH), row),
                  pl.BlockSpec((None, d, d), lambda i: (layer, 0, 0),
                               pipeline_mode=pl.Buffered(1)),
                  pl.BlockSpec((1, d), lambda i: (0, 0))],
        out_specs=[pl.BlockSpec((tm, d), row), pl.BlockSpec((tm, d), row)],
        out_shape=[jax.ShapeDtypeStruct((m, d), F32), jax.ShapeDtypeStruct((m, d), BF16)],
        compiler_params=_params("arbitrary"),
        name="outproj",
    )(h, osb, oc, op, w_out, g.reshape(1, d))


def _mlp_kernel(xn_ref, h_ref, wu_ref, wd_ref, *rest, final):
    if final:
        g_ref, o_ref, acc_ref = rest
    else:
        o_ref, acc_ref = rest
    f = pl.program_id(1)

    @pl.when(f == 0)
    def _():
        acc_ref[...] = jnp.zeros_like(acc_ref)

    a = jnp.maximum(_dot(xn_ref[...], wu_ref[...]), 0.0)
    acc_ref[...] += _dot((a * a).astype(BF16), wd_ref[...])

    @pl.when(f == pl.num_programs(1) - 1)
    def _():
        h = h_ref[...] + acc_ref[...]
        o_ref[...] = _rmsnorm_rows(h, g_ref[...]) if final else h


def mlp_call(xn, h, w_up, w_down, layer, final_g=None):
    m, d = h.shape
    ff = w_up.shape[2]
    tm = min(ROW_TILE, m)
    tf = FF_TILE
    row = lambda i, f: (i, 0)
    final = final_g is not None
    in_specs = [pl.BlockSpec((tm, d), row),
                pl.BlockSpec((tm, d), row),
                pl.BlockSpec((None, d, tf), lambda i, f: (layer, 0, f)),
                pl.BlockSpec((None, tf, d), lambda i, f: (layer, f, 0))]
    args = [xn, h, w_up, w_down]
    if final:
        in_specs.append(pl.BlockSpec((1, d), lambda i, f: (0, 0)))
        args.append(final_g.reshape(1, d))
    return pl.pallas_call(
        functools.partial(_mlp_kernel, final=final),
        grid=(m // tm, ff // tf),
        in_specs=in_specs,
        out_specs=pl.BlockSpec((tm, d), row),
        out_shape=jax.ShapeDtypeStruct((m, d), F32),
        scratch_shapes=[pltpu.VMEM((tm, d), F32)],
        compiler_params=_params("arbitrary", "arbitrary"),
        name="mlp_final" if final else "mlp",
    )(*args)


def _pad_hist(hist, rows):
    return jnp.pad(hist, ((0, 0), (rows - hist.shape[1], 0), (0, 0)))


def _block_diag(w):
    g, a, b = w.shape
    eye = jnp.eye(g, dtype=w.dtype)
    return (eye[:, None, :, None] * w[:, :, None, :]).reshape(g * a, g * b)


def _group_trunk(x, layers, big, final_g, sample, caches):
    depth = len(layers)
    h = x
    cs, ps = [], []
    shape = kv_stack_shape(x.shape[0], sample, depth)
    kv = (jnp.zeros(shape, F32), jnp.zeros(shape, F32))
    for l, lw in enumerate(layers):
        if sample:
            cache_k, cache_v, cache_conv, state_pool = caches
            (q, glu, up), kv = inproj_call(h, lw["norm_mix_g"], big["w_in"], True, l, kv)
            osb = attn_sample_call(q, kv[0], kv[1], cache_k, cache_v, l)
            hist_c = _pad_hist(cache_conv[l], HIST_ROWS)
            hist_p = _pad_hist(state_pool[l], HIST_ROWS)
            start = cache_k.shape[3]
        else:
            (qt, kb, vt, glu, up), kv = inproj_call(h, lw["norm_mix_g"], big["w_in"], False,
                                                    l, kv)
            osb = attn_prompt_call(qt, kb, vt)
            hist_c = jnp.zeros((1, HIST_ROWS, CONV_WIDTH), F32)
            hist_p = jnp.zeros((1, HIST_ROWS, POOL_WIDTH), F32)
            start = 0
        oc, op, c_state, p_state = mixers_call(glu, up, hist_c, hist_p, lw, not sample, start)
        h, xn2 = outproj_call(h, osb, oc, op, big["w_out"], lw["norm_mlp_g"], l)
        h = mlp_call(xn2, h, big["w_up"], big["w_down"], l, final_g if l + 1 == depth else None)
        cs.append(c_state)
        ps.append(p_state)
    return h, kv[0], kv[1], jnp.stack(cs), jnp.stack(ps)


def kernel(x_prompt, x_sample, cache_k, cache_v, cache_conv, state_pool, norm_mix_g, w_in,
           conv_dw_w, conv_dw_b, conv_norm_g, conv_norm_b, conv_pw_w, conv_pw_b, pool_w,
           pool_scale, w_out, norm_mlp_g, w_up, w_down, final_norm_g):
    depth = w_in.shape[0]
    bp, seq, d = x_prompt.shape
    bs, dseq, _ = x_sample.shape
    assert bp == 1 and dseq == DEC_SEQ and d == D_MODEL
    layers = []
    for l in range(depth):
        layers.append(dict(
            norm_mix_g=norm_mix_g[l],
            dw_w=jnp.pad(conv_dw_w[l], ((0, HIST_ROWS - CONV_KERNEL), (0, 0))),
            dw_b=conv_dw_b[l].reshape(1, -1),
            n_g=conv_norm_g[l].reshape(1, -1),
            n_b=conv_norm_b[l].reshape(1, -1),
            pw_w=conv_pw_w[l].astype(BF16),
            pw_b=conv_pw_b[l].reshape(1, -1),
            pool_w=_block_diag(pool_w[l]).astype(BF16),
            pool_s=pool_scale[l].reshape(1, -1),
            norm_mlp_g=norm_mlp_g[l],
        ))
    big = dict(w_in=w_in.astype(BF16), w_out=w_out.astype(BF16),
               w_up=w_up.astype(BF16), w_down=w_down.astype(BF16))
    yp, kp, vp, cp, pp = _group_trunk(x_prompt.reshape(seq, d), layers, big, final_norm_g,
                                      False, None)
    ys, ks, vs, cs, ps = _group_trunk(x_sample.reshape(bs * dseq, d), layers, big, final_norm_g, True,
                                      (cache_k, cache_v, cache_conv, state_pool))
    return (yp.reshape(bp, seq, d), ys.reshape(bs, dseq, d),
            kp[:, None], vp[:, None], cp, pp, ks, vs, cs, ps)
```

```python
import functools

import jax
import jax.numpy as jnp
from jax import lax
from jax.experimental import pallas as pl
from jax.experimental.pallas import tpu as pltpu

F32 = jnp.float32
BF16 = jnp.bfloat16

D_MODEL = 2048
SB_WIDTH = 1024
HEAD_DIM = 128
HEADS = SB_WIDTH // HEAD_DIM
CONV_WIDTH = 512
CONV_KERNEL = 31
CONV_GROUPS = 4
POOL_WIDTH = 512
POOL_WINDOWS = (2, 4, 8, 16)
POOL_GROUP = POOL_WIDTH // len(POOL_WINDOWS)
POOL_HIST = max(POOL_WINDOWS) - 1
IN_COLS = 3 * SB_WIDTH + 2 * CONV_WIDTH + POOL_WIDTH
EPS = 1e-6
DEC_SEQ = 64

LOG2E = 1.4426950408889634
Q_SCALE = HEAD_DIM ** -0.5 * LOG2E

LANE = 128
HIST_ROWS = 32
ROW_TILE = 512
FF_TILE = 1024
ATTN_TK = 128
ATTN_VT = 2 * ATTN_TK
SAMPLE_TK = 256
CONV_CHUNK = 64
VMEM_LIMIT = 56 * 1024 * 1024


def _params(*sem):
    return pltpu.CompilerParams(dimension_semantics=sem, vmem_limit_bytes=VMEM_LIMIT)


def _dot(a, b):
    return jnp.dot(a, b, preferred_element_type=F32)


def _softplus2(z):
    e = jnp.exp2(-jnp.abs(z))
    return jnp.maximum(z, 0.0) + jnp.log(1.0 + e) * LOG2E


def _rmsnorm_rows(x, g):
    ms = jnp.mean(x * x, axis=-1, keepdims=True)
    return x * lax.rsqrt(ms + EPS) * g


def _inproj_kernel(h_ref, g_ref, w_ref, *rest, tm, sample, n_alias):
    outs = rest[n_alias:]
    x = _rmsnorm_rows(h_ref[...], g_ref[...]).astype(BF16)
    if sample:
        q_ref, kf_ref, vf_ref, glu_ref, up_ref = outs
        nb = tm // DEC_SEQ
    else:
        qt_ref, kb_ref, vt_ref, kf_ref, vf_ref, glu_ref, up_ref = outs
    pair = 2 * HEAD_DIM
    for hp in range(HEADS // 2):
        c0 = hp * pair
        q2 = _dot(x, w_ref[:, c0:c0 + pair]) * Q_SCALE
        k2 = _dot(x, w_ref[:, SB_WIDTH + c0:SB_WIDTH + c0 + pair])
        v2 = _dot(x, w_ref[:, 2 * SB_WIDTH + c0:2 * SB_WIDTH + c0 + pair])
        for u in range(2):
            h = 2 * hp + u
            sl = slice(u * HEAD_DIM, (u + 1) * HEAD_DIM)
            if sample:
                q_ref[:, h] = q2[:, sl].astype(BF16).reshape(nb, DEC_SEQ, HEAD_DIM)
                kf_ref[:, h] = k2[:, sl].reshape(nb, DEC_SEQ, HEAD_DIM)
                vf_ref[:, h] = v2[:, sl].reshape(nb, DEC_SEQ, HEAD_DIM)
            else:
                qt_ref[h] = q2[:, sl].T.astype(BF16)
                kb_ref[h] = k2[:, sl].astype(BF16)
                kf_ref[h] = k2[:, sl]
                vf_ref[h] = v2[:, sl]
                vt = v2[:, sl].T.astype(BF16)
                for c in range(tm // ATTN_VT):
                    vt_ref[h, c] = vt[:, c * ATTN_VT:(c + 1) * ATTN_VT]
    c0 = 3 * SB_WIDTH
    uc = _dot(x, w_ref[:, c0:c0 + 2 * CONV_WIDTH])
    glu_ref[...] = uc[:, 0:CONV_WIDTH] * jax.nn.sigmoid(uc[:, CONV_WIDTH:])
    up_ref[...] = _dot(x, w_ref[:, c0 + 2 * CONV_WIDTH:])


def kv_stack_shape(m, sample, depth):
    if sample:
        return (depth, m // DEC_SEQ, HEADS, DEC_SEQ, HEAD_DIM)
    return (depth, HEADS, m, HEAD_DIM)


def inproj_call(h, g, w_in, sample, layer, kv_stack):
    depth = kv_stack[0].shape[0]
    m, d = h.shape
    tm = min(ROW_TILE, m)
    if sample:
        nbt = m // DEC_SEQ
        nb = tm // DEC_SEQ
        hshape = (nbt, HEADS, DEC_SEQ, HEAD_DIM)
        hblock = (nb, HEADS, DEC_SEQ, HEAD_DIM)
        kv_shape = jax.ShapeDtypeStruct((depth,) + hshape, F32)
        kv_spec = pl.BlockSpec((None,) + hblock, lambda i: (layer, i, 0, 0, 0))
        out_shape = [jax.ShapeDtypeStruct(hshape, BF16), kv_shape, kv_shape]
        out_specs = [pl.BlockSpec(hblock, lambda i: (i, 0, 0, 0)), kv_spec, kv_spec]
        kv_index = (1, 2)
    else:
        nkb = m // ATTN_VT
        kshape = (HEADS, m, HEAD_DIM)
        kblock = (HEADS, tm, HEAD_DIM)
        kv_shape = jax.ShapeDtypeStruct((depth,) + kshape, F32)
        kv_spec = pl.BlockSpec((None,) + kblock, lambda i: (layer, 0, i, 0))
        out_shape = [jax.ShapeDtypeStruct((HEADS, HEAD_DIM, m), BF16),
                     jax.ShapeDtypeStruct(kshape, BF16),
                     jax.ShapeDtypeStruct((HEADS, nkb, HEAD_DIM, ATTN_VT), BF16),
                     kv_shape, kv_shape]
        out_specs = [pl.BlockSpec((HEADS, HEAD_DIM, tm), lambda i: (0, 0, i)),
                     pl.BlockSpec(kblock, lambda i: (0, i, 0)),
                     pl.BlockSpec((HEADS, tm // ATTN_VT, HEAD_DIM, ATTN_VT),
                                  lambda i: (0, i, 0, 0)),
                     kv_spec, kv_spec]
        kv_index = (3, 4)
    out_shape += [jax.ShapeDtypeStruct((m, CONV_WIDTH), F32),
                  jax.ShapeDtypeStruct((m, POOL_WIDTH), F32)]
    out_specs += [pl.BlockSpec((tm, CONV_WIDTH), lambda i: (i, 0)),
                  pl.BlockSpec((tm, POOL_WIDTH), lambda i: (i, 0))]
    in_specs = [pl.BlockSpec((tm, d), lambda i: (i, 0)),
                pl.BlockSpec((1, d), lambda i: (0, 0)),
                pl.BlockSpec((None, d, IN_COLS), lambda i: (layer, 0, 0),
                             pipeline_mode=pl.Buffered(1))]
    args = [h, g.reshape(1, d), w_in]
    aliases = {}
    for n, stack in enumerate(kv_stack):
        assert stack.shape == kv_shape.shape and stack.dtype == kv_shape.dtype
        aliases[len(args)] = kv_index[n]
        in_specs.append(pl.BlockSpec(memory_space=pl.ANY))
        args.append(stack)
    outs = pl.pallas_call(
        functools.partial(_inproj_kernel, tm=tm, sample=sample, n_alias=len(aliases)),
        grid=(m // tm,),
        in_specs=in_specs,
        out_specs=out_specs,
        out_shape=out_shape,
        input_output_aliases=aliases,
        compiler_params=_params("arbitrary"),
        name="inproj_sample" if sample else "inproj_prompt",
    )(*args)
    kv = tuple(outs[n] for n in kv_index)
    return [o for n, o in enumerate(outs) if n not in kv_index], kv


MASKED_LOGIT = -1e30
STICK_EXHAUSTED = 160.0


def _attn_stage1(k_ref, qt, p, i, masked, a_ref, sp_ref, *, tq, tk):
    for half in (1, 0):
        row0 = pl.multiple_of(p * (2 * tk) + half * tk, tk)
        rows = slice(half * tk, (half + 1) * tk)
        lane0 = tq - tk if masked and half == 1 else 0
        if lane0:
            a_ref[rows, 0:lane0] = jnp.full((tk, lane0), MASKED_LOGIT, F32)
            sp_ref[half, :, 0:lane0] = jnp.zeros((tk, lane0), BF16)
        z = _dot(k_ref[pl.ds(row0, tk), :], qt[:, lane0:])
        sp = _softplus2(z)
        a = z - sp
        if masked:
            s_idx = row0 + lax.broadcasted_iota(jnp.int32, z.shape, 0)
            t_idx = i * tq + lane0 + lax.broadcasted_iota(jnp.int32, z.shape, 1)
            valid = s_idx < t_idx
            sp = jnp.where(valid, sp, 0.0)
            a = jnp.where(valid, a, MASKED_LOGIT)
        a_ref[rows, lane0:] = a
        sp_ref[half, :, lane0:] = sp.astype(BF16)


def _attn_stage2(tri_ref, a_ref, sp_ref, half, carry, *, tk):
    sp = sp_ref[half]
    later = _dot(tri_ref[...], sp)
    w = jnp.exp2(a_ref[half * tk:(half + 1) * tk, :] - later - carry)
    return w.astype(BF16), carry + later[0:1, :] + sp[0:1, :].astype(F32)


ATTN_HEADS_PER_STEP = 8
ATTN_STAGE_BUFS = 2
PV_LAG = 4


def _attn_prompt_kernel(qt_ref, k_ref, vt_ref, tri_ref, o_ref, *scratch, tq, tk):
    i = pl.program_id(1)
    nh = ATTN_HEADS_PER_STEP
    heads = range(nh)
    acc = scratch[0:nh]
    ab = [scratch[nh + b * nh:nh + (b + 1) * nh] for b in range(ATTN_STAGE_BUFS)]
    sp0 = nh * (1 + ATTN_STAGE_BUFS)
    spb = [scratch[sp0 + b * nh:sp0 + (b + 1) * nh] for b in range(ATTN_STAGE_BUFS)]
    s1 = functools.partial(_attn_stage1, tq=tq, tk=tk)
    s2 = functools.partial(_attn_stage2, tk=tk)
    qts = [qt_ref[h] for h in heads]
    for h in heads:
        acc[h][...] = jnp.zeros_like(acc[h])

    def second(buf, p, carry):
        carry = list(carry)
        units = [(h, half) for h in heads for half in (1, 0)]
        ws = {}
        for n in range(len(units) + PV_LAG):
            if n < len(units):
                h, half = units[n]
                ws[units[n]], carry[h] = s2(tri_ref, ab[buf][h], spb[buf][h], half, carry[h])
            if n >= PV_LAG:
                h, half = units[n - PV_LAG]
                vt = vt_ref[h, p, :, half * tk:(half + 1) * tk]
                acc[h][...] += _dot(vt, ws.pop(units[n - PV_LAG]))
        return tuple(carry)

    def step(buf, p, carry):
        for h in heads:
            s1(k_ref.at[h], qts[h], jnp.maximum(p - 1, 0), i, False,
               ab[1 - buf][h], spb[1 - buf][h])
        return second(buf, p, carry)

    for h in heads:
        s1(k_ref.at[h], qts[h], i, i, True, ab[0][h], spb[0][h])

    def live(state):
        m, _, least = state
        return jnp.logical_and(m < (i + 1) // 2, least <= STICK_EXHAUSTED)

    def body(state):
        m, carry, _ = state
        p = i - 2 * m
        carry = step(1, p - 1, step(0, p, carry))
        least = functools.reduce(jnp.minimum, [jnp.min(c) for c in carry])
        return m + 1, carry, least

    zero = tuple(jnp.zeros((1, tq), F32) for _ in heads)
    _, carry, least = lax.while_loop(live, body, (jnp.int32(0), zero, jnp.float32(0.0)))

    @pl.when(jnp.logical_and(i % 2 == 0, least <= STICK_EXHAUSTED))
    def _():
        second(0, 0, carry)

    d = acc[0].shape[0]
    for h in heads:
        o_ref[:, h * d:(h + 1) * d] = acc[h][...].T.astype(o_ref.dtype)


def _tri(tk):
    s = lax.broadcasted_iota(jnp.int32, (tk, tk), 0)
    j = lax.broadcasted_iota(jnp.int32, (tk, tk), 1)
    return (j > s).astype(BF16)


def attn_prompt_call(qt, kb, vt):
    h, d, t = qt.shape
    tq = ATTN_VT
    tk = ATTN_TK
    tri = _tri(tk)
    nh = ATTN_HEADS_PER_STEP
    nstage = nh * ATTN_STAGE_BUFS
    return pl.pallas_call(
        functools.partial(_attn_prompt_kernel, tq=tq, tk=tk),
        grid=(h // nh, t // tq),
        in_specs=[pl.BlockSpec((nh, d, tq), lambda hh, i: (hh, 0, i)),
                  pl.BlockSpec((nh, t, d), lambda hh, i: (hh, 0, 0),
                               pipeline_mode=pl.Buffered(1)),
                  pl.BlockSpec((nh, t // ATTN_VT, d, ATTN_VT), lambda hh, i: (hh, 0, 0, 0),
                               pipeline_mode=pl.Buffered(1)),
                  pl.BlockSpec((tk, tk), lambda hh, i: (0, 0))],
        out_specs=pl.BlockSpec((tq, nh * d), lambda hh, i: (i, hh)),
        out_shape=jax.ShapeDtypeStruct((t, h * d), BF16),
        scratch_shapes=([pltpu.VMEM((d, tq), F32)] * nh
                        + [pltpu.VMEM((2 * tk, tq), F32)] * nstage
                        + [pltpu.VMEM((2, tk, tq), BF16)] * nstage),
        compiler_params=_params("arbitrary", "arbitrary"),
        name="attn_prompt",
    )(qt, kb, vt, tri)


SAMPLE_HEADS_PER_STEP = 8


def _sample_sweep(q_ref, kb_ref, vb_ref, nblk, mask_last, carry, tri_ref, acc_ref,
                  z_ref, sp_ref, cum_ref, w_ref, *, tk):
    nt = (((1,), (1,)), ((), ()))
    nh, ts, d = q_ref.shape
    rows = nh * ts
    tot = nblk * tk
    for h in range(nh):
        z_ref[h * ts:(h + 1) * ts, 0:tot] = lax.dot_general(
            q_ref[h], kb_ref[h, 0:tot, :], nt, preferred_element_type=F32)
    for b in range(nblk):
        cols = slice(b * tk, (b + 1) * tk)
        z = z_ref[:, cols]
        sp = _softplus2(z)
        a = z - sp
        if mask_last and b == nblk - 1:
            t_idx = lax.broadcasted_iota(jnp.int32, (rows, tk), 0) % ts
            s_idx = lax.broadcasted_iota(jnp.int32, (rows, tk), 1)
            valid = s_idx < t_idx
            sp = jnp.where(valid, sp, 0.0)
            a = jnp.where(valid, a, MASKED_LOGIT)
        z_ref[:, cols] = a
        sp_ref[b * rows:(b + 1) * rows, :] = sp.astype(BF16)
    cum_ref[0:nblk * rows, :] = _dot(sp_ref[0:nblk * rows, :], tri_ref[...])
    for b in range(nblk - 1, -1, -1):
        cols = slice(b * tk, (b + 1) * tk)
        later = cum_ref[b * rows:(b + 1) * rows, :]
        w_ref[:, cols] = jnp.exp2(z_ref[:, cols] - later - carry).astype(BF16)
        first = sp_ref[b * rows:(b + 1) * rows, 0:1].astype(F32)
        carry = carry + later[:, 0:1] + first
    for h in range(nh):
        acc_ref[h * ts:(h + 1) * ts, :] += _dot(w_ref[h * ts:(h + 1) * ts, 0:tot],
                                                 vb_ref[h, 0:tot, :])
    return carry


def _attn_sample_kernel(q_ref, kn_ref, vn_ref, kt_ref, vt_ref, ck_hbm, cv_hbm, tri_ref, o_ref,
                        kb_ref, vb_ref, ckf_ref, cvf_ref, z_ref, sp_ref, cum_ref, w_ref,
                        acc_ref, sem, *, past, tk, layer):
    nh, ts, d = q_ref.shape
    rows = nh * ts
    early = past - tk
    sweep = functools.partial(_sample_sweep, q_ref, kb_ref, vb_ref, tri_ref=tri_ref,
                              acc_ref=acc_ref, z_ref=z_ref, sp_ref=sp_ref, cum_ref=cum_ref,
                              w_ref=w_ref, tk=tk)
    acc_ref[...] = jnp.zeros_like(acc_ref)
    for tail, new, dst in ((kt_ref, kn_ref, kb_ref), (vt_ref, vn_ref, vb_ref)):
        for h in range(nh):
            dst[h, 0:tk, :] = tail[h].astype(BF16)
            dst[h, tk:tk + ts, :] = new[h].astype(BF16)
            dst[h, tk + ts:2 * tk, :] = jnp.zeros((tk - ts, d), BF16)
    carry = sweep(2, True, jnp.zeros((rows, 1), F32))

    @pl.when(jnp.min(carry) <= STICK_EXHAUSTED)
    def _():
        b = pl.program_id(0)
        h0 = pl.multiple_of(pl.program_id(1) * nh, nh)
        copies = [pltpu.make_async_copy(
            src.at[layer, b, pl.ds(h0, nh), pl.ds(0, early), :], dst, sem.at[n])
            for n, (src, dst) in enumerate(((ck_hbm, ckf_ref), (cv_hbm, cvf_ref)))]
        for c in copies:
            c.start()
        for c in copies:
            c.wait()
        for h in range(nh):
            kb_ref[h, 0:early, :] = ckf_ref[h].astype(BF16)
            vb_ref[h, 0:early, :] = cvf_ref[h].astype(BF16)
        sweep(early // tk, False, carry)

    for h in range(nh):
        o_ref[:, h * d:(h + 1) * d] = acc_ref[h * ts:(h + 1) * ts, :].astype(o_ref.dtype)


def attn_sample_call(q, k_stack, v_stack, cache_k, cache_v, layer):
    b, h, ts, d = q.shape
    past = cache_k.shape[3]
    tk = SAMPLE_TK
    nh = SAMPLE_HEADS_PER_STEP
    assert past % tk == 0 and past >= 2 * tk and ts <= tk
    rows = nh * ts
    early = past - tk
    span = max(early, 2 * tk)
    tri = _tri(tk).T
    q_spec = pl.BlockSpec((None, nh, ts, d), lambda bb, hh: (bb, hh, 0, 0))
    new_spec = pl.BlockSpec((None, None, nh, ts, d), lambda bb, hh: (layer, bb, hh, 0, 0))
    tail_spec = pl.BlockSpec((None, None, nh, tk, d),
                             lambda bb, hh: (layer, bb, hh, past // tk - 1, 0))
    any_spec = pl.BlockSpec(memory_space=pl.ANY)
    return pl.pallas_call(
        functools.partial(_attn_sample_kernel, past=past, tk=tk, layer=layer),
        grid=(b, h // nh),
        in_specs=[q_spec, new_spec, new_spec, tail_spec, tail_spec, any_spec, any_spec,
                  pl.BlockSpec((tk, tk), lambda bb, hh: (0, 0))],
        out_specs=pl.BlockSpec((ts, nh * d), lambda bb, hh: (bb, hh)),
        out_shape=jax.ShapeDtypeStruct((b * ts, h * d), BF16),
        scratch_shapes=[pltpu.VMEM((nh, span, d), BF16), pltpu.VMEM((nh, span, d), BF16),
                        pltpu.VMEM((nh, early, d), F32), pltpu.VMEM((nh, early, d), F32),
                        pltpu.VMEM((rows, span), F32),
                        pltpu.VMEM((span // tk * rows, tk), BF16),
                        pltpu.VMEM((span // tk * rows, tk), F32),
                        pltpu.VMEM((rows, span), BF16),
                        pltpu.VMEM((rows, d), F32),
                        pltpu.SemaphoreType.DMA((2,))],
        compiler_params=_params("arbitrary", "arbitrary"),
        name="attn_sample",
    )(q, k_stack, v_stack, cache_k, cache_v, cache_k, cache_v, tri)


def _mixers_kernel(glu_ref, up_ref, hc_ref, hp_ref, dww_ref, dwb_ref, ng_ref, nb_ref,
                   pww_ref, pwb_ref, plw_ref, pls_ref,
                   oc_ref, op_ref, cs_ref, ps_ref,
                   ext_ref, sh_ref, yn_ref, pext_ref, pooled_ref,
                   *, tm, carry, start_pos):
    i = pl.program_id(0)
    hdr = HIST_ROWS

    def load_hist():
        ext_ref[0:hdr, :] = hc_ref[...]
        pext_ref[0:hdr, :] = hp_ref[...]

    if carry:
        pl.when(i == 0)(load_hist)
    else:
        load_hist()

    ext_ref[hdr:hdr + tm, :] = glu_ref[...]
    pext_ref[hdr:hdr + tm, :] = up_ref[...]
    cs_ref[...] = ext_ref[pl.ds(hdr + tm - (CONV_KERNEL - 1), CONV_KERNEL - 1), :]
    ps_ref[...] = pext_ref[pl.ds(hdr + tm - POOL_HIST, POOL_HIST), :]

    sh_rows = sh_ref.shape[1]
    for r in range(1, 8):
        sh_ref[r - 1] = ext_ref[pl.ds(r, sh_rows), :]

    rc = min(CONV_CHUNK, tm)
    first = hdr - (CONV_KERNEL - 1)

    def conv_chunk(c, _):
        base = pl.multiple_of(c * rc, rc)
        acc = jnp.zeros((rc, CONV_WIDTH), F32)
        for r in range(8):
            taps = [(j, ((first + j) // 8) * 8) for j in range(CONV_KERNEL)
                    if (first + j) % 8 == r]
            lo = min(off for _, off in taps)
            hi = max(off for _, off in taps)
            src = ext_ref if r == 0 else sh_ref.at[r - 1]
            slab = src[pl.ds(base + lo, rc + hi - lo), :]
            for j, off in taps:
                acc = acc + slab[off - lo:off - lo + rc] * dww_ref[j:j + 1, :]
        y = acc + dwb_ref[...]
        gw = CONV_WIDTH // CONV_GROUPS
        for g in range(CONV_GROUPS):
            sl = slice(g * gw, (g + 1) * gw)
            yg = y[:, sl]
            mu = jnp.mean(yg, axis=-1, keepdims=True)
            dv = yg - mu
            var = jnp.mean(dv * dv, axis=-1, keepdims=True)
            yn = dv * lax.rsqrt(var + EPS) * ng_ref[:, sl] + nb_ref[:, sl]
            yn = yn * jax.nn.sigmoid(yn)
            yn_ref[pl.ds(base, rc), sl] = yn.astype(BF16)
        return 0

    lax.fori_loop(0, tm // rc, conv_chunk, 0)
    oc_ref[...] = (_dot(yn_ref[...], pww_ref[...]) + pwb_ref[...]).astype(oc_ref.dtype)

    pc = min(LANE, tm)
    for c in range(tm // pc):
        b0 = hdr + c * pc
        row = lax.broadcasted_iota(jnp.int32, (pc, POOL_GROUP), 0) + c * pc
        pos = row + (start_pos + (i * tm if carry else 0))
        for g, w in enumerate(POOL_WINDOWS):
            sl = slice(g * POOL_GROUP, (g + 1) * POOL_GROUP)
            u = pext_ref[b0:b0 + pc, sl]
            s = u
            for dlt in range(1, w):
                s = s + pext_ref[pl.ds(b0 - dlt, pc), sl]
            cnt = jnp.minimum(w, pos + 1).astype(F32)
            pooled_ref[c * pc:(c + 1) * pc, sl] = (s / cnt - u).astype(BF16)
    op_ref[...] = (_dot(pooled_ref[...], plw_ref[...]) * pls_ref[...]).astype(op_ref.dtype)

    if carry:
        ext_ref[0:hdr, :] = ext_ref[tm:tm + hdr, :]
        pext_ref[0:hdr, :] = pext_ref[tm:tm + hdr, :]


def mixers_call(glu, up, hist_c, hist_p, lw, carry, start_pos):
    m = glu.shape[0]
    nseq = hist_c.shape[0]
    tm = min(ROW_TILE, m) if carry else m // nseq
    row = lambda i: (i, 0)
    const = lambda i: (0, 0)
    seq = (lambda i: (0, 0, 0)) if carry else (lambda i: (i, 0, 0))
    vec = pl.BlockSpec((1, CONV_WIDTH), const)
    sq = pl.BlockSpec((CONV_WIDTH, CONV_WIDTH), const)
    return pl.pallas_call(
        functools.partial(_mixers_kernel, tm=tm, carry=carry, start_pos=start_pos),
        grid=(m // tm,),
        in_specs=[pl.BlockSpec((tm, CONV_WIDTH), row),
                  pl.BlockSpec((tm, POOL_WIDTH), row),
                  pl.BlockSpec((None, HIST_ROWS, CONV_WIDTH), seq),
                  pl.BlockSpec((None, HIST_ROWS, POOL_WIDTH), seq),
                  pl.BlockSpec((HIST_ROWS, CONV_WIDTH), const),
                  vec, vec, vec, sq, vec, sq, vec],
        out_specs=[pl.BlockSpec((tm, CONV_WIDTH), row),
                   pl.BlockSpec((tm, POOL_WIDTH), row),
                   pl.BlockSpec((None, CONV_KERNEL - 1, CONV_WIDTH), seq),
                   pl.BlockSpec((None, POOL_HIST, POOL_WIDTH), seq)],
        out_shape=[jax.ShapeDtypeStruct((m, CONV_WIDTH), BF16),
                   jax.ShapeDtypeStruct((m, POOL_WIDTH), BF16),
                   jax.ShapeDtypeStruct((nseq, CONV_KERNEL - 1, CONV_WIDTH), F32),
                   jax.ShapeDtypeStruct((nseq, POOL_HIST, POOL_WIDTH), F32)],
        scratch_shapes=[pltpu.VMEM((HIST_ROWS + tm + 8, CONV_WIDTH), F32),
                        pltpu.VMEM((7, tm + HIST_ROWS - 8, CONV_WIDTH), F32),
                        pltpu.VMEM((tm, CONV_WIDTH), BF16),
                        pltpu.VMEM((HIST_ROWS + tm, POOL_WIDTH), F32),
                        pltpu.VMEM((tm, POOL_WIDTH), BF16)],
        compiler_params=_params("arbitrary"),
        name="mixers_prompt" if carry else "mixers_sample",
    )(glu, up, hist_c, hist_p, lw["dw_w"], lw["dw_b"], lw["n_g"], lw["n_b"],
      lw["pw_w"], lw["pw_b"], lw["pool_w"], lw["pool_s"])


def _outproj_kernel(h_ref, osb_ref, oc_ref, op_ref, w_ref, g_ref, hn_ref, xn_ref):
    c1 = SB_WIDTH
    c2 = SB_WIDTH + CONV_WIDTH
    mix = (_dot(osb_ref[...], w_ref[0:c1, :]) + _dot(oc_ref[...], w_ref[c1:c2, :])
           + _dot(op_ref[...], w_ref[c2:, :]))
    h = h_ref[...] + mix
    hn_ref[...] = h
    xn_ref[...] = _rmsnorm_rows(h, g_ref[...]).astype(xn_ref.dtype)


def outproj_call(h, osb, oc, op, w_out, g, layer):
    m, d = h.shape
    tm = min(ROW_TILE, m)
    row = lambda i: (i, 0)
    return pl.pallas_call(
        _outproj_kernel,
        grid=(m // tm,),
        in_specs=[pl.BlockSpec((tm, d), row),
                  pl.BlockSpec((tm, SB_WIDTH), row),
                  pl.BlockSpec((tm, CONV_WIDTH), row),
                  pl.BlockSpec((tm, POOL_WIDTH), row),
                  pl.BlockSpec((None, d, d), lambda i: (layer, 0, 0),
                               pipeline_mode=pl.Buffered(1)),
                  pl.BlockSpec((1, d), lambda i: (0, 0))],
        out_specs=[pl.BlockSpec((tm, d), row), pl.BlockSpec((tm, d), row)],
        out_shape=[jax.ShapeDtypeStruct((m, d), F32), jax.ShapeDtypeStruct((m, d), BF16)],
        compiler_params=_params("arbitrary"),
        name="outproj",
    )(h, osb, oc, op, w_out, g.reshape(1, d))


def _mlp_kernel(xn_ref, h_ref, wu_ref, wd_ref, *rest, final):
    if final:
        g_ref, o_ref, acc_ref = rest
    else:
        o_ref, acc_ref = rest
    f = pl.program_id(1)

    @pl.when(f == 0)
    def _():
        acc_ref[...] = jnp.zeros_like(acc_ref)

    a = jnp.maximum(_dot(xn_ref[...], wu_ref[...]), 0.0)
    acc_ref[...] += _dot((a * a).astype(BF16), wd_ref[...])

    @pl.when(f == pl.num_programs(1) - 1)
    def _():
        h = h_ref[...] + acc_ref[...]
        o_ref[...] = _rmsnorm_rows(h, g_ref[...]) if final else h


def mlp_call(xn, h, w_up, w_down, layer, final_g=None):
    m, d = h.shape
    ff = w_up.shape[2]
    tm = min(ROW_TILE, m)
    tf = FF_TILE
    row = lambda i, f: (i, 0)
    final = final_g is not None
    in_specs = [pl.BlockSpec((tm, d), row),
                pl.BlockSpec((tm, d), row),
                pl.BlockSpec((None, d, tf), lambda i, f: (layer, 0, f)),
                pl.BlockSpec((None, tf, d), lambda i, f: (layer, f, 0))]
    args = [xn, h, w_up, w_down]
    if final:
        in_specs.append(pl.BlockSpec((1, d), lambda i, f: (0, 0)))
        args.append(final_g.reshape(1, d))
    return pl.pallas_call(
        functools.partial(_mlp_kernel, final=final),
        grid=(m // tm, ff // tf),
        in_specs=in_specs,
        out_specs=pl.BlockSpec((tm, d), row),
        out_shape=jax.ShapeDtypeStruct((m, d), F32),
        scratch_shapes=[pltpu.VMEM((tm, d), F32)],
        compiler_params=_params("arbitrary", "arbitrary"),
        name="mlp_final" if final else "mlp",
    )(*args)


def _pad_hist(hist, rows):
    return jnp.pad(hist, ((0, 0), (rows - hist.shape[1], 0), (0, 0)))


def _block_diag(w):
    g, a, b = w.shape
    eye = jnp.eye(g, dtype=w.dtype)
    return (eye[:, None, :, None] * w[:, :, None, :]).reshape(g * a, g * b)


def _group_trunk(x, layers, big, final_g, sample, caches):
    depth = len(layers)
    h = x
    cs, ps = [], []
    shape = kv_stack_shape(x.shape[0], sample, depth)
    kv = (jnp.zeros(shape, F32), jnp.zeros(shape, F32))
    for l, lw in enumerate(layers):
        if sample:
            cache_k, cache_v, cache_conv, state_pool = caches
            (q, glu, up), kv = inproj_call(h, lw["norm_mix_g"], big["w_in"], True, l, kv)
            osb = attn_sample_call(q, kv[0], kv[1], cache_k, cache_v, l)
            hist_c = _pad_hist(cache_conv[l], HIST_ROWS)
            hist_p = _pad_hist(state_pool[l], HIST_ROWS)
            start = cache_k.shape[3]
        else:
            (qt, kb, vt, glu, up), kv = inproj_call(h, lw["norm_mix_g"], big["w_in"], False,
                                                    l, kv)
            osb = attn_prompt_call(qt, kb, vt)
            hist_c = jnp.zeros((1, HIST_ROWS, CONV_WIDTH), F32)
            hist_p = jnp.zeros((1, HIST_ROWS, POOL_WIDTH), F32)
            start = 0
        oc, op, c_state, p_state = mixers_call(glu, up, hist_c, hist_p, lw, not sample, start)
        h, xn2 = outproj_call(h, osb, oc, op, big["w_out"], lw["norm_mlp_g"], l)
        h = mlp_call(xn2, h, big["w_up"], big["w_down"], l, final_g if l + 1 == depth else None)
        cs.append(c_state)
        ps.append(p_state)
    return h, kv[0], kv[1], jnp.stack(cs), jnp.stack(ps)


def kernel(x_prompt, x_sample, cache_k, cache_v, cache_conv, state_pool, norm_mix_g, w_in,
           conv_dw_w, conv_dw_b, conv_norm_g, conv_norm_b, conv_pw_w, conv_pw_b, pool_w,
           pool_scale, w_out, norm_mlp_g, w_up, w_down, final_norm_g):
    depth = w_in.shape[0]
    bp, seq, d = x_prompt.shape
    bs, dseq, _ = x_sample.shape
    assert bp == 1 and dseq == DEC_SEQ and d == D_MODEL
    layers = []
    for l in range(depth):
        layers.append(dict(
            norm_mix_g=norm_mix_g[l],
            dw_w=jnp.pad(conv_dw_w[l], ((0, HIST_ROWS - CONV_KERNEL), (0, 0))),
            dw_b=conv_dw_b[l].reshape(1, -1),
            n_g=conv_norm_g[l].reshape(1, -1),
            n_b=conv_norm_b[l].reshape(1, -1),
            pw_w=conv_pw_w[l].astype(BF16),
            pw_b=conv_pw_b[l].reshape(1, -1),
            pool_w=_block_diag(pool_w[l]).astype(BF16),
            pool_s=pool_scale[l].reshape(1, -1),
            norm_mlp_g=norm_mlp_g[l],
        ))
    big = dict(w_in=w_in.astype(BF16), w_out=w_out.astype(BF16),
               w_up=w_up.astype(BF16), w_down=w_down.astype(BF16))
    yp, kp, vp, cp, pp = _group_trunk(x_prompt.reshape(seq, d), layers, big, final_norm_g,
                                      False, None)
    ys, ks, vs, cs, ps = _group_trunk(x_sample.reshape(bs * dseq, d), layers, big, final_norm_g, True,
                                      (cache_k, cache_v, cache_conv, state_pool))
    return (yp.reshape(bp, seq, d), ys.reshape(bs, dseq, d),
            kp[:, None], vp[:, None], cp, pp, ks, vs, cs, ps)
```
